```python
import jax, jax.numpy as jnp
from jax import lax
import numpy as np

D_MODEL = 2048
BATCH = 8
SEQ = 2048
DEPTH = 1

RET_HEADS = 4
RET_HEAD_DIM = 256
RET_WIDTH = RET_HEADS * RET_HEAD_DIM
RET_CHUNK = 128
RET_ROPE_THETA = 10000.0
MOBA_HEADS = 8
MOBA_HEAD_DIM = 128
MOBA_WIDTH = MOBA_HEADS * MOBA_HEAD_DIM
MOBA_BLOCK = 256
MOBA_TOPK = 3
MOBA_Q_CHUNK = 64
ROPE_THETA = 500000.0
ROT_DIM = MOBA_HEAD_DIM // 4
D_FF = 5632
CONV_WIDTH = 3
EPS = 1e-6
IN_SIZES = (RET_WIDTH, RET_WIDTH, RET_WIDTH, RET_WIDTH,
            MOBA_WIDTH, MOBA_WIDTH, MOBA_WIDTH,
            D_MODEL, D_MODEL)
IN_TOTAL = sum(IN_SIZES)

kernel_name = "hybrid_retention_moba_convffn"


def rmsnorm(x, w):
    xf = x.astype(jnp.float32)
    y = xf * lax.rsqrt(jnp.mean(xf * xf, axis=-1, keepdims=True) + EPS)
    return (y * w.astype(jnp.float32)).astype(x.dtype)


def rotary(x, rot_dim, theta):
    S = x.shape[1]
    pos = jnp.arange(S, dtype=jnp.float32)
    inv_freq = jnp.asarray(theta, jnp.float32) ** (-jnp.arange(0, rot_dim, 2, dtype=jnp.float32) / rot_dim)
    ang = pos[:, None] * inv_freq[None, :]
    cos = jnp.cos(ang)[None, :, None, :]
    sin = jnp.sin(ang)[None, :, None, :]
    xr = x[..., :rot_dim].astype(jnp.float32)
    x1, x2 = jnp.split(xr, 2, axis=-1)
    rot = jnp.concatenate([x1 * cos - x2 * sin, x2 * cos + x1 * sin], axis=-1).astype(x.dtype)
    return jnp.concatenate([rot, x[..., rot_dim:]], axis=-1)


def retention(q, k, v):
    B, S, H, dk = q.shape
    dv = v.shape[-1]
    C = RET_CHUNK
    N = S // C
    q = q.astype(jnp.float32)
    k = k.astype(jnp.float32) * (dk ** -0.5)
    v = v.astype(jnp.float32)
    log_g = jnp.log1p(-jnp.exp2(-5.0 - jnp.arange(H, dtype=jnp.float32)))
    i = jnp.arange(C, dtype=jnp.float32)
    diff = i[:, None] - i[None, :]
    causal = diff >= 0
    decay = jnp.where(causal[None], jnp.exp(log_g[:, None, None] * jnp.where(causal, diff, 0.0)[None]), 0.0)
    xi = jnp.exp(log_g[:, None] * (i[None, :] + 1.0))
    zeta = jnp.exp(log_g[:, None] * (C - 1.0 - i[None, :]))
    g_chunk = jnp.exp(log_g * C)

    def to_chunks(t):
        return t.reshape(B, N, C, H, t.shape[-1]).transpose(1, 0, 3, 2, 4)

    def step(state, xs):
        qc, kc, vc = xs
        scores = jnp.einsum('bhqd,bhkd->bhqk', qc, kc) * decay[None]
        inner = jnp.einsum('bhqk,bhkd->bhqd', scores, vc)
        cross = jnp.einsum('bhqd,bhde->bhqe', qc, state) * xi[None, :, :, None]
        new_state = state * g_chunk[None, :, None, None] + jnp.einsum(
            'bhkd,bhke->bhde', kc * zeta[None, :, :, None], vc)
        return new_state, inner + cross

    init = jnp.zeros((B, H, dk, dv), jnp.float32)
    _, out = lax.scan(step, init, (to_chunks(q), to_chunks(k), to_chunks(v)))
    return out.transpose(1, 0, 3, 2, 4).reshape(B, S, H, dv)


def moba_attention(q, k, v):
    B, S, H, d = q.shape
    dtype = q.dtype
    BS = MOBA_BLOCK
    Cq = MOBA_Q_CHUNK
    NB = -(-S // BS)
    S_pad = NB * BS
    NQ = S // Cq
    topk = min(MOBA_TOPK, NB)
    scale = d ** -0.5
    qh = q.transpose(0, 2, 1, 3)
    pad = ((0, 0), (0, 0), (0, S_pad - S), (0, 0))
    kb = jnp.pad(k.transpose(0, 2, 1, 3), pad).reshape(B, H, NB, BS, d)
    vb = jnp.pad(v.transpose(0, 2, 1, 3), pad).reshape(B, H, NB, BS, d)
    kmean = jnp.mean(kb.astype(jnp.float32), axis=3)
    q_all = qh.reshape(B, H, NQ, Cq, d).transpose(0, 2, 1, 3, 4).reshape(B * NQ, H, Cq, d)
    b_idx = jnp.repeat(jnp.arange(B, dtype=jnp.int32), NQ)
    n_idx = jnp.tile(jnp.arange(NQ, dtype=jnp.int32), B)
    hh = jnp.arange(H)[:, None, None]

    def step(args):
        qc, b, n = args
        kb_b, vb_b, km_b = kb[b], vb[b], kmean[b]
        q0 = n * Cq
        cb = q0 // BS
        qpos = q0 + jnp.arange(Cq)
        bscore = jnp.einsum('hqd,hnd->hqn', qc.astype(jnp.float32), km_b)
        past = jnp.arange(NB) < cb
        bscore = jnp.where(past[None, None, :], bscore, -jnp.inf)
        _, idx = lax.top_k(bscore, topk)
        valid = idx < cb
        k_sel = kb_b[hh, idx]
        v_sel = vb_b[hh, idx]
        s_sel = jnp.einsum('hqd,hqnkd->hqnk', qc, k_sel, preferred_element_type=jnp.float32) * scale
        s_sel = jnp.where(valid[..., None], s_sel, -jnp.inf).reshape(H, Cq, topk * BS)
        k_own = lax.dynamic_index_in_dim(kb_b, cb, axis=1, keepdims=False)
        v_own = lax.dynamic_index_in_dim(vb_b, cb, axis=1, keepdims=False)
        s_own = jnp.einsum('hqd,hkd->hqk', qc, k_own, preferred_element_type=jnp.float32) * scale
        kpos = cb * BS + jnp.arange(BS)
        s_own = jnp.where((kpos[None, :] <= qpos[:, None])[None], s_own, -jnp.inf)
        p = jax.nn.softmax(jnp.concatenate([s_sel, s_own], axis=-1), axis=-1).astype(dtype)
        p_sel = p[..., :topk * BS].reshape(H, Cq, topk, BS)
        p_own = p[..., topk * BS:]
        o = jnp.einsum('hqnk,hqnkd->hqd', p_sel, v_sel) + jnp.einsum('hqk,hkd->hqd', p_own, v_own)
        return o.astype(dtype)

    out = lax.map(step, (q_all, b_idx, n_idx))
    return out.reshape(B, NQ, H, Cq, d).transpose(0, 1, 3, 2, 4).reshape(B, S, H * d)


def causal_dwconv(u, w, b):
    S = u.shape[1]
    up = jnp.pad(u, ((0, 0), (CONV_WIDTH - 1, 0), (0, 0)))
    y = b
    for j in range(CONV_WIDTH):
        y = y + w[j] * up[:, j:j + S]
    return y


def setup_inputs(seed: int = 0) -> dict:
    key = jax.random.key(seed)
    ks = jax.random.split(key, 14)
    f32 = jnp.float32

    def normal(k, shape, scale):
        return jax.random.normal(k, shape, f32) * scale

    def gain(k, shape):
        return 1.0 + 0.05 * jax.random.normal(k, shape, f32)

    return {
        "x": jax.random.normal(ks[0], (BATCH, SEQ, D_MODEL), f32),
        "attn_norm_w": gain(ks[1], (DEPTH, D_MODEL)),
        "w_in": normal(ks[2], (DEPTH, D_MODEL, IN_TOTAL), D_MODEL ** -0.5),
        "ret_norm_w": gain(ks[3], (DEPTH, RET_WIDTH)),
        "w_ret_up": normal(ks[4], (DEPTH, RET_WIDTH, D_MODEL), RET_WIDTH ** -0.5),
        "w_moba_up": normal(ks[5], (DEPTH, MOBA_WIDTH, D_MODEL), MOBA_WIDTH ** -0.5),
        "w_out": normal(ks[6], (DEPTH, D_MODEL, D_MODEL), D_MODEL ** -0.5),
        "ffn_norm_w": gain(ks[7], (DEPTH, D_MODEL)),
        "w_ffn_up": normal(ks[8], (DEPTH, D_MODEL, 2 * D_FF), D_MODEL ** -0.5),
        "conv_w": normal(ks[9], (DEPTH, CONV_WIDTH, 2 * D_FF), CONV_WIDTH ** -0.5),
        "conv_b": normal(ks[10], (DEPTH, 2 * D_FF), 0.01),
        "w_ffn_down": normal(ks[11], (DEPTH, D_FF, D_MODEL), D_FF ** -0.5),
        "final_norm_w": gain(ks[12], (D_MODEL,)),
    }


def reference(x, attn_norm_w, w_in, ret_norm_w, w_ret_up, w_moba_up, w_out,
              ffn_norm_w, w_ffn_up, conv_w, conv_b, w_ffn_down, final_norm_w):
    B, S, _ = x.shape
    split_at = [int(s) for s in np.cumsum(IN_SIZES)[:-1]]
    for l in range(DEPTH):
        h = rmsnorm(x, attn_norm_w[l])
        proj = h @ w_in[l]
        rq, rk, rv, rg, mq, mk, mv, g_ret, g_moba = jnp.split(proj, split_at, axis=-1)
        rq = rotary(rq.reshape(B, S, RET_HEADS, RET_HEAD_DIM), RET_HEAD_DIM, RET_ROPE_THETA)
        rk = rotary(rk.reshape(B, S, RET_HEADS, RET_HEAD_DIM), RET_HEAD_DIM, RET_ROPE_THETA)
        rv = rv.reshape(B, S, RET_HEADS, RET_HEAD_DIM)
        o_ret = retention(rq, rk, rv)
        o_ret = o_ret * lax.rsqrt(jnp.mean(o_ret * o_ret, axis=-1, keepdims=True) + EPS)
        o_ret = o_ret.reshape(B, S, RET_WIDTH).astype(x.dtype) * ret_norm_w[l]
        o_ret = jax.nn.silu(rg) * o_ret
        mq = rotary(mq.reshape(B, S, MOBA_HEADS, MOBA_HEAD_DIM), ROT_DIM, ROPE_THETA)
        mk = rotary(mk.reshape(B, S, MOBA_HEADS, MOBA_HEAD_DIM), ROT_DIM, ROPE_THETA)
        mv = mv.reshape(B, S, MOBA_HEADS, MOBA_HEAD_DIM)
        o_moba = moba_attention(mq, mk, mv)
        merged = jax.nn.sigmoid(g_ret) * (o_ret @ w_ret_up[l]) + jax.nn.sigmoid(g_moba) * (o_moba @ w_moba_up[l])
        x = x + merged @ w_out[l]
        h2 = rmsnorm(x, ffn_norm_w[l])
        up = causal_dwconv(h2 @ w_ffn_up[l], conv_w[l], conv_b[l])
        a, b = jnp.split(up, 2, axis=-1)
        x = x + (jax.nn.silu(a) * b) @ w_ffn_down[l]
    return rmsnorm(x, final_norm_w)
```

```python
import functools

import numpy as np
import jax
import jax.numpy as jnp
from jax import lax
from jax.experimental import pallas as pl
from jax.experimental.pallas import tpu as pltpu

RET_HEADS = 4
RET_HEAD_DIM = 256
RET_WIDTH = RET_HEADS * RET_HEAD_DIM
RET_ROPE_THETA = 10000.0
RET_TILE = 256
MOBA_HEADS = 8
MOBA_HEAD_DIM = 128
MOBA_WIDTH = MOBA_HEADS * MOBA_HEAD_DIM
MOBA_BLOCK = 256
MOBA_TOPK = 3
ROPE_THETA = 500000.0
ROT_DIM = MOBA_HEAD_DIM // 4
CONV_WIDTH = 3
EPS = 1e-6

LANES = 128
MASK_PENALTY = -1e30
BF16 = jnp.bfloat16
F32 = jnp.float32

NT_DIMS = (((1,), (1,)), ((), ()))
TN_DIMS = (((0,), (0,)), ((), ()))


def _params(semantics, vmem_mib):
    return pltpu.CompilerParams(dimension_semantics=semantics, vmem_limit_bytes=vmem_mib * 1024 * 1024)


def _rmsnorm_kernel(x_ref, w_ref, o_ref):
    x = x_ref[...]
    r = lax.rsqrt(jnp.mean(x * x, axis=-1, keepdims=True) + EPS)
    o_ref[...] = (x * r * w_ref[...]).astype(o_ref.dtype)


def _rmsnorm(x, w, out_dtype, tm=512):
    t, d = x.shape
    return pl.pallas_call(
        _rmsnorm_kernel,
        grid=(t // tm,),
        in_specs=[pl.BlockSpec((tm, d), lambda i: (i, 0)), pl.BlockSpec((1, d), lambda i: (0, 0))],
        out_specs=pl.BlockSpec((tm, d), lambda i: (i, 0)),
        out_shape=jax.ShapeDtypeStruct((t, d), out_dtype),
        compiler_params=_params(("parallel",), 32),
        name="rmsnorm",
    )(x, w.reshape(1, d))


def _proj_plain_kernel(h_ref, w_ref, o_ref):
    o_ref[...] = jnp.dot(h_ref[...], w_ref[...], preferred_element_type=F32).astype(o_ref.dtype)


def _proj_ret_rotary_kernel(h_ref, w_ref, cos_ref, sin_ref, o_ref):
    acc = jnp.dot(h_ref[...], w_ref[...], preferred_element_type=F32)
    cos = cos_ref[...]
    sin = sin_ref[...]
    half = RET_HEAD_DIM // 2
    for hh in range(acc.shape[1] // RET_HEAD_DIM):
        c0 = hh * RET_HEAD_DIM
        x1 = acc[:, c0:c0 + half]
        x2 = acc[:, c0 + half:c0 + RET_HEAD_DIM]
        o_ref[:, c0:c0 + half] = (x1 * cos - x2 * sin).astype(o_ref.dtype)
        o_ref[:, c0 + half:c0 + RET_HEAD_DIM] = (x2 * cos + x1 * sin).astype(o_ref.dtype)


def _proj_moba_rotary_kernel(h_ref, w_ref, c_ref, sa_ref, sb_ref, o_ref):
    acc = jnp.dot(h_ref[...], w_ref[...], preferred_element_type=F32)
    c = c_ref[...]
    sa = sa_ref[...]
    sb = sb_ref[...]
    half = ROT_DIM // 2
    for hh in range(acc.shape[1] // MOBA_HEAD_DIM):
        c0 = hh * MOBA_HEAD_DIM
        x = acc[:, c0:c0 + MOBA_HEAD_DIM]
        up = pltpu.roll(x, MOBA_HEAD_DIM - half, axis=1)
        dn = pltpu.roll(x, half, axis=1)
        o_ref[:, c0:c0 + MOBA_HEAD_DIM] = (x * c + up * sa + dn * sb).astype(o_ref.dtype)


def _in_proj(h, w, body, tables, table_specs, tm, tn, name):
    t, d = h.shape
    n = w.shape[1]
    return pl.pallas_call(
        body,
        grid=(t // tm, n // tn),
        in_specs=[pl.BlockSpec((tm, d), lambda i, j: (i, 0)),
                  pl.BlockSpec((d, tn), lambda i, j: (0, j))] + table_specs,
        out_specs=pl.BlockSpec((tm, tn), lambda i, j: (i, j)),
        out_shape=jax.ShapeDtypeStruct((t, n), BF16),
        compiler_params=_params(("parallel", "arbitrary"), 48),
        name=name,
    )(h, w, *tables)


def _rope_angles(seq, rot_dim, theta):
    pos = np.arange(seq, dtype=np.float64)
    inv = np.float64(theta) ** (-np.arange(0, rot_dim, 2, dtype=np.float64) / rot_dim)
    return pos[:, None] * inv[None, :]


def _ret_rope_tables(seq):
    ang = _rope_angles(seq, RET_HEAD_DIM, RET_ROPE_THETA)
    return jnp.asarray(np.cos(ang), F32), jnp.asarray(np.sin(ang), F32)


def _moba_rope_tables(seq):
    half = ROT_DIM // 2
    ang = _rope_angles(seq, ROT_DIM, ROPE_THETA)
    cos, sin = np.cos(ang), np.sin(ang)
    c = np.ones((seq, MOBA_HEAD_DIM))
    sa = np.zeros((seq, MOBA_HEAD_DIM))
    sb = np.zeros((seq, MOBA_HEAD_DIM))
    c[:, :half] = cos
    c[:, half:ROT_DIM] = cos
    sa[:, :half] = -sin
    sb[:, half:ROT_DIM] = sin
    scale = MOBA_HEAD_DIM ** -0.5
    stack = lambda a: jnp.asarray(np.stack([a * scale, a]), F32)
    return stack(c), stack(sa), stack(sb)


def _retention_kernel(q_ref, k_ref, v_ref, g_ref, nw_ref, o_ref, state_ref, dec_ref, xi_ref, zeta_ref):
    c = pl.program_id(1)
    tile = RET_TILE
    k_scale = RET_HEAD_DIM ** -0.5
    log_gs = [float(np.log1p(-np.exp2(-5.0 - hh))) for hh in range(RET_HEADS)]

    @pl.when(c == 0)
    def _():
        state_ref[...] = jnp.zeros_like(state_ref)
        row = lax.broadcasted_iota(jnp.int32, (tile, tile), 0).astype(F32)
        col = lax.broadcasted_iota(jnp.int32, (tile, tile), 1).astype(F32)
        diff = row - col
        causal = diff >= 0
        row_d = lax.broadcasted_iota(jnp.int32, (tile, RET_HEAD_DIM), 0).astype(F32)
        for hh, log_g in enumerate(log_gs):
            dec_ref[hh] = jnp.where(causal, jnp.exp(log_g * jnp.where(causal, diff, 0.0)), 0.0) * k_scale
            xi_ref[hh] = jnp.exp(log_g * (row_d + 1.0))
            zeta_ref[hh] = jnp.exp(log_g * (tile - 1.0 - row_d)) * k_scale

    for hh, log_g in enumerate(log_gs):
        g_chunk = float(np.exp(log_g * tile))
        sl = slice(hh * RET_HEAD_DIM, (hh + 1) * RET_HEAD_DIM)
        q = q_ref[:, sl]
        k = k_ref[:, sl]
        v = v_ref[:, sl]
        state = state_ref[hh]
        scores = lax.dot_general(q, k, NT_DIMS, preferred_element_type=F32) * dec_ref[hh]
        inner = jnp.dot(scores.astype(BF16), v, preferred_element_type=F32)
        cross = jnp.dot(q, state.astype(BF16), preferred_element_type=F32) * xi_ref[hh]
        kz = (k.astype(F32) * zeta_ref[hh]).astype(BF16)
        state_ref[hh] = state * g_chunk + lax.dot_general(kz, v, TN_DIMS, preferred_element_type=F32)
        o = inner + cross
        o = o * lax.rsqrt(jnp.mean(o * o, axis=-1, keepdims=True) + EPS)
        gate = g_ref[:, sl].astype(F32)
        silu = gate / (1.0 + jnp.exp(-gate))
        o_ref[:, sl] = (silu * (o * nw_ref[:, sl])).astype(o_ref.dtype)


def _retention(qk, plain, norm_w, batch, seq):
    t = qk.shape[0]
    n_tiles = seq // RET_TILE
    row = lambda b, c: b * n_tiles + c
    hd = RET_HEAD_DIM
    return pl.pallas_call(
        _retention_kernel,
        grid=(batch, n_tiles),
        in_specs=[pl.BlockSpec((RET_TILE, RET_WIDTH), lambda b, c: (row(b, c), 0)),
                  pl.BlockSpec((RET_TILE, RET_WIDTH), lambda b, c: (row(b, c), 1)),
                  pl.BlockSpec((RET_TILE, RET_WIDTH), lambda b, c: (row(b, c), 4)),
                  pl.BlockSpec((RET_TILE, RET_WIDTH), lambda b, c: (row(b, c), 5)),
                  pl.BlockSpec((1, RET_WIDTH), lambda b, c: (0, 0))],
        out_specs=pl.BlockSpec((RET_TILE, RET_WIDTH), lambda b, c: (row(b, c), 0)),
        out_shape=jax.ShapeDtypeStruct((t, RET_WIDTH), BF16),
        scratch_shapes=[pltpu.VMEM((RET_HEADS, hd, hd), F32),
                        pltpu.VMEM((RET_HEADS, RET_TILE, RET_TILE), F32),
                        pltpu.VMEM((RET_HEADS, RET_TILE, hd), F32),
                        pltpu.VMEM((RET_HEADS, RET_TILE, hd), F32)],
        compiler_params=_params(("parallel", "arbitrary"), 32),
        name="retention",
    )(qk, qk, plain, plain, norm_w.reshape(1, RET_WIDTH))


def _moba_kernel(q_ref, k_ref, v_ref, o_ref, qa_ref, ka_ref):
    seq, d = q_ref.shape
    bs = MOBA_BLOCK
    bs_shift = bs.bit_length() - 1
    assert bs == 1 << bs_shift
    nb = seq // bs
    q = q_ref[...]
    k = k_ref[...]

    kmean = jnp.sum(k.astype(F32).reshape(nb, bs, d), axis=1) * (1.0 / bs)
    km_hi = kmean.astype(BF16)
    km_lo = (kmean - km_hi.astype(F32)).astype(BF16)
    score = (lax.dot_general(km_hi, q, NT_DIMS, preferred_element_type=F32)
             + lax.dot_general(km_lo, q, NT_DIMS, preferred_element_type=F32))
    blk = lax.broadcasted_iota(jnp.int32, (nb, seq), 0)
    own = lax.broadcasted_iota(jnp.int32, (nb, seq), 1) >> bs_shift
    past = blk < own
    score = jnp.where(past, score, -jnp.inf)
    rank = jnp.zeros((nb, seq), jnp.int32)
    for r in range(1, nb):
        other = pltpu.roll(score, r, axis=0)
        beats = (other > score) | ((other == score) & (blk >= r))
        rank = rank + beats.astype(jnp.int32)
    keep = (past & (rank < MOBA_TOPK)) | (blk == own)
    pen_t = jnp.where(keep, 0.0, MASK_PENALTY)
    pen_t = jnp.concatenate([pen_t, jnp.zeros((LANES - nb, seq), F32)], axis=0)
    pen = pen_t.T.astype(BF16)

    qa_ref[:, :d] = q
    qa_ref[:, d:] = pen
    key_blk = lax.broadcasted_iota(jnp.int32, (seq, LANES), 0) >> bs_shift
    lane = lax.broadcasted_iota(jnp.int32, (seq, LANES), 1)
    ka_ref[:, :d] = k
    ka_ref[:, d:] = jnp.where(lane == key_blk, 1.0, 0.0).astype(BF16)

    qpos = lax.broadcasted_iota(jnp.int32, (bs, bs), 0)
    kpos = lax.broadcasted_iota(jnp.int32, (bs, bs), 1)
    for qi in range(nb):
        q0 = qi * bs
        qa = qa_ref[q0:q0 + bs, :]
        s_own = lax.dot_general(qa, ka_ref[q0:q0 + bs, :], NT_DIMS, preferred_element_type=F32)
        s_own = jnp.where(kpos <= qpos, s_own, -jnp.inf)
        m = jnp.max(s_own, axis=-1, keepdims=True)
        if qi > 0:
            s_past = lax.dot_general(qa, ka_ref[0:q0, :], NT_DIMS, preferred_element_type=F32)
            m = jnp.maximum(m, jnp.max(s_past, axis=-1, keepdims=True))
        p_own = jnp.exp(s_own - m)
        denom = jnp.sum(p_own, axis=-1, keepdims=True)
        acc = jnp.dot(p_own.astype(BF16), v_ref[q0:q0 + bs, :], preferred_element_type=F32)
        if qi > 0:
            p_past = jnp.exp(s_past - m)
            denom = denom + jnp.sum(p_past, axis=-1, keepdims=True)
            acc = acc + jnp.dot(p_past.astype(BF16), v_ref[0:q0, :], preferred_element_type=F32)
        o_ref[q0:q0 + bs, :] = (acc / denom).astype(o_ref.dtype)


def _moba(mqk, plain, batch, seq):
    t = mqk.shape[0]
    d = MOBA_HEAD_DIM
    v_blk0 = 6144 // d
    return pl.pallas_call(
        _moba_kernel,
        grid=(batch, MOBA_HEADS),
        in_specs=[pl.BlockSpec((seq, d), lambda b, h: (b, h)),
                  pl.BlockSpec((seq, d), lambda b, h: (b, MOBA_HEADS + h)),
                  pl.BlockSpec((seq, d), lambda b, h: (b, v_blk0 + h))],
        out_specs=pl.BlockSpec((seq, d), lambda b, h: (b, h)),
        out_shape=jax.ShapeDtypeStruct((t, MOBA_WIDTH), BF16),
        scratch_shapes=[pltpu.VMEM((seq, d + LANES), BF16), pltpu.VMEM((seq, d + LANES), BF16)],
        compiler_params=_params(("parallel", "parallel"), 48),
        name="moba",
    )(mqk, mqk, plain)


def _merge_kernel(yr_ref, ym_ref, gr_ref, gm_ref, x_ref, wr_ref, wm_ref, wo_ref, nw_ref, x1_ref, h2_ref):
    ret = jnp.dot(yr_ref[...], wr_ref[...], preferred_element_type=F32)
    mob = jnp.dot(ym_ref[...], wm_ref[...], preferred_element_type=F32)
    merged = jax.nn.sigmoid(gr_ref[...].astype(F32)) * ret + jax.nn.sigmoid(gm_ref[...].astype(F32)) * mob
    x1 = x_ref[...] + jnp.dot(merged.astype(BF16), wo_ref[...], preferred_element_type=F32)
    x1_ref[...] = x1
    r = lax.rsqrt(jnp.mean(x1 * x1, axis=-1, keepdims=True) + EPS)
    h2_ref[...] = (x1 * r * nw_ref[...]).astype(h2_ref.dtype)


def _merge(y_ret, y_moba, plain, x, w_ret_up, w_moba_up, w_out, norm_w, tm=256):
    t, d = x.shape
    resident = lambda shape: pl.BlockSpec(shape, lambda i: (0, 0), pipeline_mode=pl.Buffered(1))
    return pl.pallas_call(
        _merge_kernel,
        grid=(t // tm,),
        in_specs=[pl.BlockSpec((tm, RET_WIDTH), lambda i: (i, 0)),
                  pl.BlockSpec((tm, MOBA_WIDTH), lambda i: (i, 0)),
                  pl.BlockSpec((tm, d), lambda i: (i, 0)),
                  pl.BlockSpec((tm, d), lambda i: (i, 1)),
                  pl.BlockSpec((tm, d), lambda i: (i, 0)),
                  resident(w_ret_up.shape), resident(w_moba_up.shape), resident(w_out.shape),
                  pl.BlockSpec((1, d), lambda i: (0, 0))],
        out_specs=[pl.BlockSpec((tm, d), lambda i: (i, 0)), pl.BlockSpec((tm, d), lambda i: (i, 0))],
        out_shape=[jax.ShapeDtypeStruct((t, d), F32), jax.ShapeDtypeStruct((t, d), BF16)],
        compiler_params=_params(("parallel",), 56),
        name="merge_outproj",
    )(y_ret, y_moba, plain, plain, x, w_ret_up, w_moba_up, w_out, norm_w.reshape(1, d))


def _ffn_up_kernel(h_ref, wa_ref, wb_ref, cwa_ref, cwb_ref, cba_ref, cbb_ref, o_ref, ua_ref, ub_ref,
                   *, tiles_per_seq):
    tm = h_ref.shape[0]
    pad = 8
    i = pl.program_id(1)

    @pl.when(i % tiles_per_seq == 0)
    def _():
        ua_ref[0:pad, :] = jnp.zeros((pad, ua_ref.shape[1]), F32)
        ub_ref[0:pad, :] = jnp.zeros((pad, ub_ref.shape[1]), F32)

    h = h_ref[...]
    ua_ref[pad:pad + tm, :] = jnp.dot(h, wa_ref[...], preferred_element_type=F32)
    ub_ref[pad:pad + tm, :] = jnp.dot(h, wb_ref[...], preferred_element_type=F32)

    def conv(u_ref, cw_ref, cb_ref):
        y = cb_ref[...]
        for j in range(CONV_WIDTH):
            off = pad - (CONV_WIDTH - 1) + j
            y = y + cw_ref[j:j + 1, :] * u_ref[off:off + tm, :]
        return y

    a = conv(ua_ref, cwa_ref, cba_ref)
    b = conv(ub_ref, cwb_ref, cbb_ref)
    o_ref[...] = ((a / (1.0 + jnp.exp(-a))) * b).astype(o_ref.dtype)
    ua_ref[0:pad, :] = ua_ref[tm:tm + pad, :]
    ub_ref[0:pad, :] = ub_ref[tm:tm + pad, :]


def _ffn_up(h2, w_up, conv_w, conv_b, seq, tm=1024, tn=512):
    t, d = h2.shape
    d_ff = w_up.shape[1] // 2
    nj = d_ff // tn
    conv_b = conv_b.reshape(1, 2 * d_ff)
    return pl.pallas_call(
        functools.partial(_ffn_up_kernel, tiles_per_seq=seq // tm),
        grid=(nj, t // tm),
        in_specs=[pl.BlockSpec((tm, d), lambda j, i: (i, 0)),
                  pl.BlockSpec((d, tn), lambda j, i: (0, j)),
                  pl.BlockSpec((d, tn), lambda j, i: (0, nj + j)),
                  pl.BlockSpec((CONV_WIDTH, tn), lambda j, i: (0, j)),
                  pl.BlockSpec((CONV_WIDTH, tn), lambda j, i: (0, nj + j)),
                  pl.BlockSpec((1, tn), lambda j, i: (0, j)),
                  pl.BlockSpec((1, tn), lambda j, i: (0, nj + j))],
        out_specs=pl.BlockSpec((tm, tn), lambda j, i: (i, j)),
        out_shape=jax.ShapeDtypeStruct((t, d_ff), BF16),
        scratch_shapes=[pltpu.VMEM((tm + 8, tn), F32), pltpu.VMEM((tm + 8, tn), F32)],
        compiler_params=_params(("parallel", "arbitrary"), 48),
        name="ffn_up_conv_gate",
    )(h2, w_up, w_up, conv_w, conv_w, conv_b, conv_b)


def _ffn_down_kernel(a_ref, w_ref, x_ref, nw_ref, o_ref, acc_ref, *, final_norm):
    kk = pl.program_id(1)

    @pl.when(kk == 0)
    def _():
        acc_ref[...] = x_ref[...]

    acc_ref[...] += jnp.dot(a_ref[...], w_ref[...], preferred_element_type=F32)

    @pl.when(kk == pl.num_programs(1) - 1)
    def _():
        x2 = acc_ref[...]
        if final_norm:
            r = lax.rsqrt(jnp.mean(x2 * x2, axis=-1, keepdims=True) + EPS)
            x2 = x2 * r * nw_ref[...]
        o_ref[...] = x2


def _ffn_down(act, w_down, x1, norm_w, final_norm, tm=512, tk=512):
    t, d = x1.shape
    d_ff = act.shape[1]
    return pl.pallas_call(
        functools.partial(_ffn_down_kernel, final_norm=final_norm),
        grid=(t // tm, d_ff // tk),
        in_specs=[pl.BlockSpec((tm, tk), lambda i, k: (i, k)),
                  pl.BlockSpec((tk, d), lambda i, k: (k, 0)),
                  pl.BlockSpec((tm, d), lambda i, k: (i, 0)),
                  pl.BlockSpec((1, d), lambda i, k: (0, 0))],
        out_specs=pl.BlockSpec((tm, d), lambda i, k: (i, 0)),
        out_shape=jax.ShapeDtypeStruct((t, d), F32),
        scratch_shapes=[pltpu.VMEM((tm, d), F32)],
        compiler_params=_params(("parallel", "arbitrary"), 48),
        name="ffn_down",
    )(act, w_down, x1, norm_w.reshape(1, d))


def kernel(x, attn_norm_w, w_in, ret_norm_w, w_ret_up, w_moba_up, w_out, ffn_norm_w, w_ffn_up, conv_w,
           conv_b, w_ffn_down, final_norm_w):
    batch, seq, d = x.shape
    depth = w_in.shape[0]
    t = batch * seq
    xf = x.reshape(t, d)
    tm = 1024
    tiles_per_seq = seq // tm

    ret_cos, ret_sin = _ret_rope_tables(seq)
    mob_c, mob_sa, mob_sb = _moba_rope_tables(seq)
    ret_specs = [pl.BlockSpec((tm, LANES), lambda i, j: (i % tiles_per_seq, 0))] * 2
    mob_specs = [pl.BlockSpec((None, tm, LANES), lambda i, j: (j, i % tiles_per_seq, 0))] * 3

    r0, m0 = 4 * RET_WIDTH, 4 * RET_WIDTH + 3 * MOBA_WIDTH
    for l in range(depth):
        wl = w_in[l]
        w_retqk = wl[:, :2 * RET_WIDTH].astype(BF16)
        w_mobqk = wl[:, r0:r0 + 2 * MOBA_WIDTH].astype(BF16)
        w_plain = jnp.concatenate([wl[:, m0:], wl[:, 2 * RET_WIDTH:r0], wl[:, r0 + 2 * MOBA_WIDTH:m0]],
                                  axis=1).astype(BF16)

        h = _rmsnorm(xf, attn_norm_w[l], BF16)
        ret_qk = _in_proj(h, w_retqk, _proj_ret_rotary_kernel, (ret_cos, ret_sin), ret_specs,
                          tm, RET_WIDTH, "in_proj_ret")
        mob_qk = _in_proj(h, w_mobqk, _proj_moba_rotary_kernel, (mob_c, mob_sa, mob_sb), mob_specs,
                          tm, MOBA_WIDTH, "in_proj_moba")
        plain = _in_proj(h, w_plain, _proj_plain_kernel, (), [], tm, 1024, "in_proj_plain")

        y_ret = _retention(ret_qk, plain, ret_norm_w[l], batch, seq)
        y_moba = _moba(mob_qk, plain, batch, seq)
        x1, h2 = _merge(y_ret, y_moba, plain, xf, w_ret_up[l].astype(BF16), w_moba_up[l].astype(BF16),
                        w_out[l].astype(BF16), ffn_norm_w[l])
        act = _ffn_up(h2, w_ffn_up[l].astype(BF16), conv_w[l], conv_b[l], seq)
        last = l == depth - 1
        xf = _ffn_down(act, w_ffn_down[l].astype(BF16), x1, final_norm_w, final_norm=last)
    return xf.reshape(batch, seq, d)
```

```python
import functools

import numpy as np
import jax
import jax.numpy as jnp
from jax import lax
from jax.experimental import pallas as pl
from jax.experimental.pallas import tpu as pltpu

RET_HEADS = 4
RET_HEAD_DIM = 256
RET_WIDTH = RET_HEADS * RET_HEAD_DIM
RET_ROPE_THETA = 10000.0
RET_TILE = 256
MOBA_HEADS = 8
MOBA_HEAD_DIM = 128
MOBA_WIDTH = MOBA_HEADS * MOBA_HEAD_DIM
MOBA_BLOCK = 256
MOBA_TOPK = 3
ROPE_THETA = 500000.0
ROT_DIM = MOBA_HEAD_DIM // 4
CONV_WIDTH = 3
EPS = 1e-6

LANES = 128
MASK_PENALTY = -1e30
BF16 = jnp.bfloat16
F32 = jnp.float32

NT_DIMS = (((1,), (1,)), ((), ()))
TN_DIMS = (((0,), (0,)), ((), ()))


def _params(semantics, vmem_mib):
    return pltpu.CompilerParams(dimension_semantics=semantics, vmem_limit_bytes=vmem_mib * 1024 * 1024)


def _rmsnorm_kernel(x_ref, w_ref, o_ref):
    x = x_ref[...]
    r = lax.rsqrt(jnp.mean(x * x, axis=-1, keepdims=True) + EPS)
    o_ref[...] = (x * r * w_ref[...]).astype(o_ref.dtype)


def _rmsnorm(x, w, out_dtype, tm=512):
    t, d = x.shape
    return pl.pallas_call(
        _rmsnorm_kernel,
        grid=(t // tm,),
        in_specs=[pl.BlockSpec((tm, d), lambda i: (i, 0)), pl.BlockSpec((1, d), lambda i: (0, 0))],
        out_specs=pl.BlockSpec((tm, d), lambda i: (i, 0)),
        out_shape=jax.ShapeDtypeStruct((t, d), out_dtype),
        compiler_params=_params(("parallel",), 32),
        name="rmsnorm",
    )(x, w.reshape(1, d))


def _plain_epilogue(acc, o_ref):
    o_ref[...] = acc.astype(o_ref.dtype)


def _ret_rotary_epilogue(acc, cos_ref, sin_ref, o_ref):
    cos = cos_ref[...]
    sin = sin_ref[...]
    half = RET_HEAD_DIM // 2
    for hh in range(acc.shape[1] // RET_HEAD_DIM):
        c0 = hh * RET_HEAD_DIM
        x1 = acc[:, c0:c0 + half]
        x2 = acc[:, c0 + half:c0 + RET_HEAD_DIM]
        o_ref[:, c0:c0 + half] = (x1 * cos - x2 * sin).astype(o_ref.dtype)
        o_ref[:, c0 + half:c0 + RET_HEAD_DIM] = (x2 * cos + x1 * sin).astype(o_ref.dtype)


def _moba_rotary_epilogue(acc, c_ref, sa_ref, sb_ref, o_ref):
    c = c_ref[...]
    sa = sa_ref[...]
    sb = sb_ref[...]
    half = ROT_DIM // 2
    for hh in range(acc.shape[1] // MOBA_HEAD_DIM):
        c0 = hh * MOBA_HEAD_DIM
        x = acc[:, c0:c0 + MOBA_HEAD_DIM]
        up = pltpu.roll(x, MOBA_HEAD_DIM - half, axis=1)
        dn = pltpu.roll(x, half, axis=1)
        o_ref[:, c0:c0 + MOBA_HEAD_DIM] = (x * c + up * sa + dn * sb).astype(o_ref.dtype)


def _in_proj_kernel(h_ref, w_ref, *rest, epilogue):
    wb_ref = rest[-1]

    @pl.when(pl.program_id(1) == 0)
    def _():
        wb_ref[...] = w_ref[...].astype(wb_ref.dtype)

    acc = jnp.dot(h_ref[...], wb_ref[...], preferred_element_type=F32)
    epilogue(acc, *rest[:-1])


def _in_proj(h, w, col_block, n_out, epilogue, tables, table_specs, tm, tn, name):
    t, d = h.shape
    return pl.pallas_call(
        functools.partial(_in_proj_kernel, epilogue=epilogue),
        grid=(n_out // tn, t // tm),
        in_specs=[pl.BlockSpec((tm, d), lambda j, i: (i, 0)),
                  pl.BlockSpec((d, tn), lambda j, i: (0, col_block(j)))] + table_specs,
        out_specs=pl.BlockSpec((tm, tn), lambda j, i: (i, j)),
        out_shape=jax.ShapeDtypeStruct((t, n_out), BF16),
        scratch_shapes=[pltpu.VMEM((d, tn), BF16)],
        compiler_params=_params(("arbitrary", "arbitrary"), 52),
        name=name,
    )(h, w, *tables)


def _rope_angles(seq, rot_dim, theta):
    pos = np.arange(seq, dtype=np.float64)
    inv = np.float64(theta) ** (-np.arange(0, rot_dim, 2, dtype=np.float64) / rot_dim)
    return pos[:, None] * inv[None, :]


def _ret_rope_tables(seq):
    ang = _rope_angles(seq, RET_HEAD_DIM, RET_ROPE_THETA)
    return jnp.asarray(np.cos(ang), F32), jnp.asarray(np.sin(ang), F32)


def _moba_rope_tables(seq):
    half = ROT_DIM // 2
    ang = _rope_angles(seq, ROT_DIM, ROPE_THETA)
    cos, sin = np.cos(ang), np.sin(ang)
    c = np.ones((seq, MOBA_HEAD_DIM))
    sa = np.zeros((seq, MOBA_HEAD_DIM))
    sb = np.zeros((seq, MOBA_HEAD_DIM))
    c[:, :half] = cos
    c[:, half:ROT_DIM] = cos
    sa[:, :half] = -sin
    sb[:, half:ROT_DIM] = sin
    scale = MOBA_HEAD_DIM ** -0.5
    stack = lambda a: jnp.asarray(np.stack([a * scale, a]), F32)
    return stack(c), stack(sa), stack(sb)


def _retention_kernel(q_ref, k_ref, v_ref, g_ref, nw_ref, o_ref, state_ref, dec_ref, xi_ref, zeta_ref):
    c = pl.program_id(1)
    tile = RET_TILE
    k_scale = RET_HEAD_DIM ** -0.5
    log_gs = [float(np.log1p(-np.exp2(-5.0 - hh))) for hh in range(RET_HEADS)]

    @pl.when(c == 0)
    def _():
        state_ref[...] = jnp.zeros_like(state_ref)
        row = lax.broadcasted_iota(jnp.int32, (tile, tile), 0).astype(F32)
        col = lax.broadcasted_iota(jnp.int32, (tile, tile), 1).astype(F32)
        diff = row - col
        causal = diff >= 0
        row_d = lax.broadcasted_iota(jnp.int32, (tile, RET_HEAD_DIM), 0).astype(F32)
        for hh, log_g in enumerate(log_gs):
            dec_ref[hh] = jnp.where(causal, jnp.exp(log_g * jnp.where(causal, diff, 0.0)), 0.0) * k_scale
            xi_ref[hh] = jnp.exp(log_g * (row_d + 1.0))
            zeta_ref[hh] = jnp.exp(log_g * (tile - 1.0 - row_d)) * k_scale

    for hh, log_g in enumerate(log_gs):
        g_chunk = float(np.exp(log_g * tile))
        sl = slice(hh * RET_HEAD_DIM, (hh + 1) * RET_HEAD_DIM)
        q = q_ref[:, sl]
        k = k_ref[:, sl]
        v = v_ref[:, sl]
        state = state_ref[hh]
        scores = lax.dot_general(q, k, NT_DIMS, preferred_element_type=F32) * dec_ref[hh]
        inner = jnp.dot(scores.astype(BF16), v, preferred_element_type=F32)
        cross = jnp.dot(q, state.astype(BF16), preferred_element_type=F32) * xi_ref[hh]
        kz = (k.astype(F32) * zeta_ref[hh]).astype(BF16)
        state_ref[hh] = state * g_chunk + lax.dot_general(kz, v, TN_DIMS, preferred_element_type=F32)
        o = inner + cross
        o = o * lax.rsqrt(jnp.mean(o * o, axis=-1, keepdims=True) + EPS)
        gate = g_ref[:, sl].astype(F32)
        silu = gate / (1.0 + jnp.exp(-gate))
        o_ref[:, sl] = (silu * (o * nw_ref[:, sl])).astype(o_ref.dtype)


def _retention(qk, plain, norm_w, batch, seq):
    t = qk.shape[0]
    n_tiles = seq // RET_TILE
    row = lambda b, c: b * n_tiles + c
    hd = RET_HEAD_DIM
    return pl.pallas_call(
        _retention_kernel,
        grid=(batch, n_tiles),
        in_specs=[pl.BlockSpec((RET_TILE, RET_WIDTH), lambda b, c: (row(b, c), 0)),
                  pl.BlockSpec((RET_TILE, RET_WIDTH), lambda b, c: (row(b, c), 1)),
                  pl.BlockSpec((RET_TILE, RET_WIDTH), lambda b, c: (row(b, c), 4)),
                  pl.BlockSpec((RET_TILE, RET_WIDTH), lambda b, c: (row(b, c), 5)),
                  pl.BlockSpec((1, RET_WIDTH), lambda b, c: (0, 0))],
        out_specs=pl.BlockSpec((RET_TILE, RET_WIDTH), lambda b, c: (row(b, c), 0)),
        out_shape=jax.ShapeDtypeStruct((t, RET_WIDTH), BF16),
        scratch_shapes=[pltpu.VMEM((RET_HEADS, hd, hd), F32),
                        pltpu.VMEM((RET_HEADS, RET_TILE, RET_TILE), F32),
                        pltpu.VMEM((RET_HEADS, RET_TILE, hd), F32),
                        pltpu.VMEM((RET_HEADS, RET_TILE, hd), F32)],
        compiler_params=_params(("parallel", "arbitrary"), 32),
        name="retention",
    )(qk, qk, plain, plain, norm_w.reshape(1, RET_WIDTH))


def _moba_kernel(q_ref, k_ref, v_ref, o_ref, qa_ref, ka_ref):
    seq, d = q_ref.shape
    bs = MOBA_BLOCK
    bs_shift = bs.bit_length() - 1
    assert bs == 1 << bs_shift
    nb = seq // bs
    q = q_ref[...]
    k = k_ref[...]

    kmean = jnp.sum(k.astype(F32).reshape(nb, bs, d), axis=1) * (1.0 / bs)
    km_hi = kmean.astype(BF16)
    km_lo = (kmean - km_hi.astype(F32)).astype(BF16)
    score = (lax.dot_general(km_hi, q, NT_DIMS, preferred_element_type=F32)
             + lax.dot_general(km_lo, q, NT_DIMS, preferred_element_type=F32))
    blk = lax.broadcasted_iota(jnp.int32, (nb, seq), 0)
    own = lax.broadcasted_iota(jnp.int32, (nb, seq), 1) >> bs_shift
    past = blk < own
    score = jnp.where(past, score, -jnp.inf)
    rank = jnp.zeros((nb, seq), jnp.int32)
    for r in range(1, nb):
        other = pltpu.roll(score, r, axis=0)
        beats = (other > score) | ((other == score) & (blk >= r))
        rank = rank + beats.astype(jnp.int32)
    keep = (past & (rank < MOBA_TOPK)) | (blk == own)
    pen_t = jnp.where(keep, 0.0, MASK_PENALTY)
    pen_t = jnp.concatenate([pen_t, jnp.zeros((LANES - nb, seq), F32)], axis=0)
    pen = pen_t.T.astype(BF16)

    qa_ref[:, :d] = q
    qa_ref[:, d:] = pen
    key_blk = lax.broadcasted_iota(jnp.int32, (seq, LANES), 0) >> bs_shift
    lane = lax.broadcasted_iota(jnp.int32, (seq, LANES), 1)
    ka_ref[:, :d] = k
    ka_ref[:, d:] = jnp.where(lane == key_blk, 1.0, 0.0).astype(BF16)

    qpos = lax.broadcasted_iota(jnp.int32, (bs, bs), 0)
    kpos = lax.broadcasted_iota(jnp.int32, (bs, bs), 1)
    for qi in range(nb):
        q0 = qi * bs
        qa = qa_ref[q0:q0 + bs, :]
        s_own = lax.dot_general(qa, ka_ref[q0:q0 + bs, :], NT_DIMS, preferred_element_type=F32)
        s_own = jnp.where(kpos <= qpos, s_own, -jnp.inf)
        m = jnp.max(s_own, axis=-1, keepdims=True)
        if qi > 0:
            s_past = lax.dot_general(qa, ka_ref[0:q0, :], NT_DIMS, preferred_element_type=F32)
            m = jnp.maximum(m, jnp.max(s_past, axis=-1, keepdims=True))
        p_own = jnp.exp(s_own - m)
        denom = jnp.sum(p_own, axis=-1, keepdims=True)
        acc = jnp.dot(p_own.astype(BF16), v_ref[q0:q0 + bs, :], preferred_element_type=F32)
        if qi > 0:
            p_past = jnp.exp(s_past - m)
            denom = denom + jnp.sum(p_past, axis=-1, keepdims=True)
            acc = acc + jnp.dot(p_past.astype(BF16), v_ref[0:q0, :], preferred_element_type=F32)
        o_ref[q0:q0 + bs, :] = (acc / denom).astype(o_ref.dtype)


def _moba(mqk, plain, batch, seq):
    t = mqk.shape[0]
    d = MOBA_HEAD_DIM
    v_blk0 = 6144 // d
    return pl.pallas_call(
        _moba_kernel,
        grid=(batch, MOBA_HEADS),
        in_specs=[pl.BlockSpec((seq, d), lambda b, h: (b, h)),
                  pl.BlockSpec((seq, d), lambda b, h: (b, MOBA_HEADS + h)),
                  pl.BlockSpec((seq, d), lambda b, h: (b, v_blk0 + h))],
        out_specs=pl.BlockSpec((seq, d), lambda b, h: (b, h)),
        out_shape=jax.ShapeDtypeStruct((t, MOBA_WIDTH), BF16),
        scratch_shapes=[pltpu.VMEM((seq, d + LANES), BF16), pltpu.VMEM((seq, d + LANES), BF16)],
        compiler_params=_params(("parallel", "parallel"), 48),
        name="moba",
    )(mqk, mqk, plain)


def _merge_kernel(yr_ref, ym_ref, gr_ref, gm_ref, x_ref, wr_ref, wm_ref, wo_ref, nw_ref, x1_ref, h2_ref):
    ret = jnp.dot(yr_ref[...], wr_ref[...], preferred_element_type=F32)
    mob = jnp.dot(ym_ref[...], wm_ref[...], preferred_element_type=F32)
    merged = jax.nn.sigmoid(gr_ref[...].astype(F32)) * ret + jax.nn.sigmoid(gm_ref[...].astype(F32)) * mob
    x1 = x_ref[...] + jnp.dot(merged.astype(BF16), wo_ref[...], preferred_element_type=F32)
    x1_ref[...] = x1
    r = lax.rsqrt(jnp.mean(x1 * x1, axis=-1, keepdims=True) + EPS)
    h2_ref[...] = (x1 * r * nw_ref[...]).astype(h2_ref.dtype)


def _merge(y_ret, y_moba, plain, x, w_ret_up, w_moba_up, w_out, norm_w, tm=256):
    t, d = x.shape
    resident = lambda shape: pl.BlockSpec(shape, lambda i: (0, 0), pipeline_mode=pl.Buffered(1))
    return pl.pallas_call(
        _merge_kernel,
        grid=(t // tm,),
        in_specs=[pl.BlockSpec((tm, RET_WIDTH), lambda i: (i, 0)),
                  pl.BlockSpec((tm, MOBA_WIDTH), lambda i: (i, 0)),
                  pl.BlockSpec((tm, d), lambda i: (i, 0)),
                  pl.BlockSpec((tm, d), lambda i: (i, 1)),
                  pl.BlockSpec((tm, d), lambda i: (i, 0)),
                  resident(w_ret_up.shape), resident(w_moba_up.shape), resident(w_out.shape),
                  pl.BlockSpec((1, d), lambda i: (0, 0))],
        out_specs=[pl.BlockSpec((tm, d), lambda i: (i, 0)), pl.BlockSpec((tm, d), lambda i: (i, 0))],
        out_shape=[jax.ShapeDtypeStruct((t, d), F32), jax.ShapeDtypeStruct((t, d), BF16)],
        compiler_params=_params(("parallel",), 56),
        name="merge_outproj",
    )(y_ret, y_moba, plain, plain, x, w_ret_up, w_moba_up, w_out, norm_w.reshape(1, d))


def _ffn_up_kernel(h_ref, wa_ref, wb_ref, cwa_ref, cwb_ref, cba_ref, cbb_ref, o_ref, ua_ref, ub_ref,
                   wa16_ref, wb16_ref, *, tiles_per_seq):
    tm = h_ref.shape[0]
    pad = 8
    i = pl.program_id(1)

    @pl.when(i == 0)
    def _():
        wa16_ref[...] = wa_ref[...].astype(wa16_ref.dtype)
        wb16_ref[...] = wb_ref[...].astype(wb16_ref.dtype)

    @pl.when(i % tiles_per_seq == 0)
    def _():
        ua_ref[0:pad, :] = jnp.zeros((pad, ua_ref.shape[1]), F32)
        ub_ref[0:pad, :] = jnp.zeros((pad, ub_ref.shape[1]), F32)

    h = h_ref[...]
    ua_ref[pad:pad + tm, :] = jnp.dot(h, wa16_ref[...], preferred_element_type=F32)
    ub_ref[pad:pad + tm, :] = jnp.dot(h, wb16_ref[...], preferred_element_type=F32)

    def conv(u_ref, cw_ref, cb_ref):
        y = cb_ref[...]
        for j in range(CONV_WIDTH):
            off = pad - (CONV_WIDTH - 1) + j
            y = y + cw_ref[j:j + 1, :] * u_ref[off:off + tm, :]
        return y

    a = conv(ua_ref, cwa_ref, cba_ref)
    b = conv(ub_ref, cwb_ref, cbb_ref)
    o_ref[...] = ((a / (1.0 + jnp.exp(-a))) * b).astype(o_ref.dtype)
    ua_ref[0:pad, :] = ua_ref[tm:tm + pad, :]
    ub_ref[0:pad, :] = ub_ref[tm:tm + pad, :]


def _ffn_up(h2, w_up, conv_w, conv_b, seq, tm=1024, tn=512):
    t, d = h2.shape
    d_ff = w_up.shape[1] // 2
    nj = d_ff // tn
    conv_b = conv_b.reshape(1, 2 * d_ff)
    return pl.pallas_call(
        functools.partial(_ffn_up_kernel, tiles_per_seq=seq // tm),
        grid=(nj, t // tm),
        in_specs=[pl.BlockSpec((tm, d), lambda j, i: (i, 0)),
                  pl.BlockSpec((d, tn), lambda j, i: (0, j)),
                  pl.BlockSpec((d, tn), lambda j, i: (0, nj + j)),
                  pl.BlockSpec((CONV_WIDTH, tn), lambda j, i: (0, j)),
                  pl.BlockSpec((CONV_WIDTH, tn), lambda j, i: (0, nj + j)),
                  pl.BlockSpec((1, tn), lambda j, i: (0, j)),
                  pl.BlockSpec((1, tn), lambda j, i: (0, nj + j))],
        out_specs=pl.BlockSpec((tm, tn), lambda j, i: (i, j)),
        out_shape=jax.ShapeDtypeStruct((t, d_ff), BF16),
        scratch_shapes=[pltpu.VMEM((tm + 8, tn), F32), pltpu.VMEM((tm + 8, tn), F32),
                        pltpu.VMEM((d, tn), BF16), pltpu.VMEM((d, tn), BF16)],
        compiler_params=_params(("arbitrary", "arbitrary"), 52),
        name="ffn_up_conv_gate",
    )(h2, w_up, w_up, conv_w, conv_w, conv_b, conv_b)


def _ffn_down_kernel(a_ref, w_ref, x_ref, nw_ref, o_ref, *, final_norm):
    x2 = x_ref[...] + jnp.dot(a_ref[...], w_ref[...], preferred_element_type=F32)
    if final_norm:
        r = lax.rsqrt(jnp.mean(x2 * x2, axis=-1, keepdims=True) + EPS)
        x2 = x2 * r * nw_ref[...]
    o_ref[...] = x2


def _ffn_down(act, w_down, x1, norm_w, final_norm, tm=256):
    t, d = x1.shape
    d_ff = act.shape[1]
    return pl.pallas_call(
        functools.partial(_ffn_down_kernel, final_norm=final_norm),
        grid=(t // tm,),
        in_specs=[pl.BlockSpec((tm, d_ff), lambda i: (i, 0)),
                  pl.BlockSpec((d_ff, d), lambda i: (0, 0), pipeline_mode=pl.Buffered(1)),
                  pl.BlockSpec((tm, d), lambda i: (i, 0)),
                  pl.BlockSpec((1, d), lambda i: (0, 0))],
        out_specs=pl.BlockSpec((tm, d), lambda i: (i, 0)),
        out_shape=jax.ShapeDtypeStruct((t, d), F32),
        compiler_params=_params(("parallel",), 56),
        name="ffn_down",
    )(act, w_down, x1, norm_w.reshape(1, d))


def kernel(x, attn_norm_w, w_in, ret_norm_w, w_ret_up, w_moba_up, w_out, ffn_norm_w, w_ffn_up, conv_w,
           conv_b, w_ffn_down, final_norm_w):
    batch, seq, d = x.shape
    depth = w_in.shape[0]
    t = batch * seq
    xf = x.reshape(t, d)
    tm = 1024
    tiles_per_seq = seq // tm

    ret_cos, ret_sin = _ret_rope_tables(seq)
    mob_c, mob_sa, mob_sb = _moba_rope_tables(seq)
    ret_specs = [pl.BlockSpec((tm, LANES), lambda j, i: (i % tiles_per_seq, 0))] * 2
    mob_specs = [pl.BlockSpec((None, tm, LANES), lambda j, i: (j, i % tiles_per_seq, 0))] * 3

    tn = 1024
    assert RET_WIDTH == tn and MOBA_WIDTH == tn and d == 2 * tn
    plain_block = lambda j: jnp.where(j < 4, j + 7, jnp.where(j < 6, j - 2, 6))
    for l in range(depth):
        h = _rmsnorm(xf, attn_norm_w[l], BF16)
        ret_qk = _in_proj(h, w_in[l], lambda j: j, 2 * tn, _ret_rotary_epilogue, (ret_cos, ret_sin),
                          ret_specs, tm, tn, "in_proj_ret")
        mob_qk = _in_proj(h, w_in[l], lambda j: j + 4, 2 * tn, _moba_rotary_epilogue,
                          (mob_c, mob_sa, mob_sb), mob_specs, tm, tn, "in_proj_moba")
        plain = _in_proj(h, w_in[l], plain_block, 7 * tn, _plain_epilogue, (), [], tm, tn, "in_proj_plain")

        y_ret = _retention(ret_qk, plain, ret_norm_w[l], batch, seq)
        y_moba = _moba(mob_qk, plain, batch, seq)
        x1, h2 = _merge(y_ret, y_moba, plain, xf, w_ret_up[l].astype(BF16), w_moba_up[l].astype(BF16),
                        w_out[l].astype(BF16), ffn_norm_w[l])
        act = _ffn_up(h2, w_ffn_up[l], conv_w[l], conv_b[l], seq)
        last = l == depth - 1
        xf = _ffn_down(act, w_ffn_down[l].astype(BF16), x1, final_norm_w, final_norm=last)
    return xf.reshape(batch, seq, d)
```

```python
import functools
import math

import numpy as np
import jax
import jax.numpy as jnp
from jax import lax
from jax.experimental import pallas as pl
from jax.experimental.pallas import tpu as pltpu

RET_HEADS = 4
RET_HEAD_DIM = 256
RET_WIDTH = RET_HEADS * RET_HEAD_DIM
RET_ROPE_THETA = 10000.0
RET_TILE = 256
MOBA_HEADS = 8
MOBA_HEAD_DIM = 128
MOBA_WIDTH = MOBA_HEADS * MOBA_HEAD_DIM
MOBA_BLOCK = 256
MOBA_TOPK = 3
ROPE_THETA = 500000.0
ROT_DIM = MOBA_HEAD_DIM // 4
CONV_WIDTH = 3
EPS = 1e-6

LANES = 128
MASK_PENALTY = -1e30
BF16 = jnp.bfloat16
F32 = jnp.float32

NT_DIMS = (((1,), (1,)), ((), ()))
TN_DIMS = (((0,), (0,)), ((), ()))


def _params(semantics, vmem_mib):
    return pltpu.CompilerParams(dimension_semantics=semantics, vmem_limit_bytes=vmem_mib * 1024 * 1024)


def _sigmoid(x):
    return 0.5 * jnp.tanh(0.5 * x) + 0.5


def _silu(x):
    h = 0.5 * x
    return h + h * jnp.tanh(h)


def _rmsnorm_kernel(x_ref, w_ref, o_ref):
    x = x_ref[...]
    r = lax.rsqrt(jnp.mean(x * x, axis=-1, keepdims=True) + EPS)
    o_ref[...] = (x * r * w_ref[...]).astype(o_ref.dtype)


def _rmsnorm(x, w, out_dtype, tm=512):
    t, d = x.shape
    return pl.pallas_call(
        _rmsnorm_kernel,
        grid=(t // tm,),
        in_specs=[pl.BlockSpec((tm, d), lambda i: (i, 0)), pl.BlockSpec((1, d), lambda i: (0, 0))],
        out_specs=pl.BlockSpec((tm, d), lambda i: (i, 0)),
        out_shape=jax.ShapeDtypeStruct((t, d), out_dtype),
        compiler_params=_params(("parallel",), 32),
        name="rmsnorm",
    )(x, w.reshape(1, d))


def _plain_epilogue(acc, o_ref):
    o_ref[...] = acc.astype(o_ref.dtype)


def _ret_rotary_epilogue(acc, cos_ref, sin_ref, o_ref):
    cos = cos_ref[...]
    sin = sin_ref[...]
    half = RET_HEAD_DIM // 2
    for hh in range(acc.shape[1] // RET_HEAD_DIM):
        c0 = hh * RET_HEAD_DIM
        x1 = acc[:, c0:c0 + half]
        x2 = acc[:, c0 + half:c0 + RET_HEAD_DIM]
        o_ref[:, c0:c0 + half] = (x1 * cos - x2 * sin).astype(o_ref.dtype)
        o_ref[:, c0 + half:c0 + RET_HEAD_DIM] = (x2 * cos + x1 * sin).astype(o_ref.dtype)


def _moba_rotary_epilogue(acc, c_ref, sa_ref, sb_ref, o_ref):
    c = c_ref[...]
    sa = sa_ref[...]
    sb = sb_ref[...]
    half = ROT_DIM // 2
    for hh in range(acc.shape[1] // MOBA_HEAD_DIM):
        c0 = hh * MOBA_HEAD_DIM
        x = acc[:, c0:c0 + MOBA_HEAD_DIM]
        up = pltpu.roll(x, MOBA_HEAD_DIM - half, axis=1)
        dn = pltpu.roll(x, half, axis=1)
        o_ref[:, c0:c0 + MOBA_HEAD_DIM] = (x * c + up * sa + dn * sb).astype(o_ref.dtype)


def _in_proj_kernel(h_ref, w_ref, *rest, epilogue):
    wb_ref = rest[-1]

    @pl.when(pl.program_id(1) == 0)
    def _():
        wb_ref[...] = w_ref[...].astype(wb_ref.dtype)

    acc = jnp.dot(h_ref[...], wb_ref[...], preferred_element_type=F32)
    epilogue(acc, *rest[:-1])


def _in_proj(h, w, col_block, n_out, epilogue, tables, table_specs, tm, tn, name):
    t, d = h.shape
    return pl.pallas_call(
        functools.partial(_in_proj_kernel, epilogue=epilogue),
        grid=(n_out // tn, t // tm),
        in_specs=[pl.BlockSpec((tm, d), lambda j, i: (i, 0)),
                  pl.BlockSpec((d, tn), lambda j, i: (0, col_block(j)))] + table_specs,
        out_specs=pl.BlockSpec((tm, tn), lambda j, i: (i, j)),
        out_shape=jax.ShapeDtypeStruct((t, n_out), BF16),
        scratch_shapes=[pltpu.VMEM((d, tn), BF16)],
        compiler_params=_params(("arbitrary", "arbitrary"), 52),
        name=name,
    )(h, w, *tables)


def _rope_angles(seq, rot_dim, theta):
    pos = np.arange(seq, dtype=np.float64)
    inv = np.float64(theta) ** (-np.arange(0, rot_dim, 2, dtype=np.float64) / rot_dim)
    return pos[:, None] * inv[None, :]


def _ret_rope_tables(seq):
    ang = _rope_angles(seq, RET_HEAD_DIM, RET_ROPE_THETA)
    return jnp.asarray(np.cos(ang), F32), jnp.asarray(np.sin(ang), F32)


def _moba_rope_tables(seq):
    half = ROT_DIM // 2
    ang = _rope_angles(seq, ROT_DIM, ROPE_THETA)
    cos, sin = np.cos(ang), np.sin(ang)
    c = np.ones((seq, MOBA_HEAD_DIM))
    sa = np.zeros((seq, MOBA_HEAD_DIM))
    sb = np.zeros((seq, MOBA_HEAD_DIM))
    c[:, :half] = cos
    c[:, half:ROT_DIM] = cos
    sa[:, :half] = -sin
    sb[:, half:ROT_DIM] = sin
    q_scale = MOBA_HEAD_DIM ** -0.5 * math.log2(math.e)
    stack = lambda a: jnp.asarray(np.stack([a * q_scale, a]), F32)
    return stack(c), stack(sa), stack(sb)


def _retention_kernel(q_ref, k_ref, v_ref, g_ref, nw_ref, o_ref, state_ref, dec_ref, xi_ref, zeta_ref):
    c = pl.program_id(1)
    tile = RET_TILE
    k_scale = RET_HEAD_DIM ** -0.5
    log_gs = [float(np.log1p(-np.exp2(-5.0 - hh))) for hh in range(RET_HEADS)]

    @pl.when(c == 0)
    def _():
        state_ref[...] = jnp.zeros_like(state_ref)
        row = lax.broadcasted_iota(jnp.int32, (tile, tile), 0).astype(F32)
        col = lax.broadcasted_iota(jnp.int32, (tile, tile), 1).astype(F32)
        diff = row - col
        causal = diff >= 0
        row_d = lax.broadcasted_iota(jnp.int32, (tile, RET_HEAD_DIM), 0).astype(F32)
        for hh, log_g in enumerate(log_gs):
            dec_ref[hh] = jnp.where(causal, jnp.exp(log_g * jnp.where(causal, diff, 0.0)), 0.0) * k_scale
            xi_ref[hh] = jnp.exp(log_g * (row_d + 1.0))
            zeta_ref[hh] = jnp.exp(log_g * (tile - 1.0 - row_d)) * k_scale

    for hh, log_g in enumerate(log_gs):
        g_chunk = float(np.exp(log_g * tile))
        sl = slice(hh * RET_HEAD_DIM, (hh + 1) * RET_HEAD_DIM)
        q = q_ref[:, sl]
        k = k_ref[:, sl]
        v = v_ref[:, sl]
        state = state_ref[hh]
        scores = lax.dot_general(q, k, NT_DIMS, preferred_element_type=F32) * dec_ref[hh]
        inner = jnp.dot(scores.astype(BF16), v, preferred_element_type=F32)
        cross = jnp.dot(q, state.astype(BF16), preferred_element_type=F32) * xi_ref[hh]
        kz = (k.astype(F32) * zeta_ref[hh]).astype(BF16)
        state_ref[hh] = state * g_chunk + lax.dot_general(kz, v, TN_DIMS, preferred_element_type=F32)
        o = inner + cross
        o = o * lax.rsqrt(jnp.mean(o * o, axis=-1, keepdims=True) + EPS)
        o_ref[:, sl] = (_silu(g_ref[:, sl].astype(F32)) * (o * nw_ref[:, sl])).astype(o_ref.dtype)


def _retention(qk, plain, norm_w, batch, seq):
    t = qk.shape[0]
    n_tiles = seq // RET_TILE
    row = lambda b, c: b * n_tiles + c
    hd = RET_HEAD_DIM
    return pl.pallas_call(
        _retention_kernel,
        grid=(batch, n_tiles),
        in_specs=[pl.BlockSpec((RET_TILE, RET_WIDTH), lambda b, c: (row(b, c), 0)),
                  pl.BlockSpec((RET_TILE, RET_WIDTH), lambda b, c: (row(b, c), 1)),
                  pl.BlockSpec((RET_TILE, RET_WIDTH), lambda b, c: (row(b, c), 4)),
                  pl.BlockSpec((RET_TILE, RET_WIDTH), lambda b, c: (row(b, c), 5)),
                  pl.BlockSpec((1, RET_WIDTH), lambda b, c: (0, 0))],
        out_specs=pl.BlockSpec((RET_TILE, RET_WIDTH), lambda b, c: (row(b, c), 0)),
        out_shape=jax.ShapeDtypeStruct((t, RET_WIDTH), BF16),
        scratch_shapes=[pltpu.VMEM((RET_HEADS, hd, hd), F32),
                        pltpu.VMEM((RET_HEADS, RET_TILE, RET_TILE), F32),
                        pltpu.VMEM((RET_HEADS, RET_TILE, hd), F32),
                        pltpu.VMEM((RET_HEADS, RET_TILE, hd), F32)],
        compiler_params=_params(("parallel", "arbitrary"), 32),
        name="retention",
    )(qk, qk, plain, plain, norm_w.reshape(1, RET_WIDTH))


def _moba_kernel(q_ref, k_ref, v_ref, o_ref, qa_ref, ka_ref):
    seq, d = q_ref.shape
    bs = MOBA_BLOCK
    bs_shift = bs.bit_length() - 1
    assert bs == 1 << bs_shift
    nb = seq // bs
    q = q_ref[...]
    k = k_ref[...]

    kmean = jnp.sum(k.astype(F32).reshape(nb, bs, d), axis=1) * (1.0 / bs)
    km_hi = kmean.astype(BF16)
    km_lo = (kmean - km_hi.astype(F32)).astype(BF16)
    score = (lax.dot_general(km_hi, q, NT_DIMS, preferred_element_type=F32)
             + lax.dot_general(km_lo, q, NT_DIMS, preferred_element_type=F32))
    blk = lax.broadcasted_iota(jnp.int32, (nb, seq), 0)
    own = lax.broadcasted_iota(jnp.int32, (nb, seq), 1) >> bs_shift
    past = blk < own
    score = jnp.where(past, score, -jnp.inf)
    rank = jnp.zeros((nb, seq), jnp.int32)
    for r in range(1, nb):
        other = pltpu.roll(score, r, axis=0)
        beats = (other > score) | ((other == score) & (blk >= r))
        rank = rank + beats.astype(jnp.int32)
    keep = (past & (rank < MOBA_TOPK)) | (blk == own)
    pen_t = jnp.where(keep, 0.0, MASK_PENALTY)
    pen_t = jnp.concatenate([pen_t, jnp.zeros((LANES - nb, seq), F32)], axis=0)
    pen = pen_t.T.astype(BF16)

    qa_ref[:, :d] = q
    qa_ref[:, d:] = pen
    key_blk = lax.broadcasted_iota(jnp.int32, (seq, LANES), 0) >> bs_shift
    lane = lax.broadcasted_iota(jnp.int32, (seq, LANES), 1)
    ka_ref[:, :d] = k
    ka_ref[:, d:] = jnp.where(lane == key_blk, 1.0, 0.0).astype(BF16)

    qpos = lax.broadcasted_iota(jnp.int32, (bs, bs), 0)
    kpos = lax.broadcasted_iota(jnp.int32, (bs, bs), 1)
    for qi in range(nb):
        q0 = qi * bs
        n_keys = q0 + bs
        s = lax.dot_general(qa_ref[q0:n_keys, :], ka_ref[0:n_keys, :], NT_DIMS, preferred_element_type=F32)
        s_own = jnp.where(kpos <= qpos, s[:, q0:], -jnp.inf)
        s = s_own if qi == 0 else jnp.concatenate([s[:, :q0], s_own], axis=1)
        p = jnp.exp2(s - jnp.max(s, axis=-1, keepdims=True))
        denom = jnp.sum(p, axis=-1, keepdims=True)
        acc = jnp.dot(p.astype(BF16), v_ref[0:n_keys, :], preferred_element_type=F32)
        o_ref[q0:n_keys, :] = (acc / denom).astype(o_ref.dtype)


def _moba(mqk, plain, batch, seq):
    t = mqk.shape[0]
    d = MOBA_HEAD_DIM
    v_blk0 = 6144 // d
    return pl.pallas_call(
        _moba_kernel,
        grid=(batch, MOBA_HEADS),
        in_specs=[pl.BlockSpec((seq, d), lambda b, h: (b, h)),
                  pl.BlockSpec((seq, d), lambda b, h: (b, MOBA_HEADS + h)),
                  pl.BlockSpec((seq, d), lambda b, h: (b, v_blk0 + h))],
        out_specs=pl.BlockSpec((seq, d), lambda b, h: (b, h)),
        out_shape=jax.ShapeDtypeStruct((t, MOBA_WIDTH), BF16),
        scratch_shapes=[pltpu.VMEM((seq, d + LANES), BF16), pltpu.VMEM((seq, d + LANES), BF16)],
        compiler_params=_params(("parallel", "parallel"), 48),
        name="moba",
    )(mqk, mqk, plain)


def _merge_kernel(yr_ref, ym_ref, gr_ref, gm_ref, x_ref, wr_ref, wm_ref, wo_ref, nw_ref, x1_ref, h2_ref):
    ret = jnp.dot(yr_ref[...], wr_ref[...], preferred_element_type=F32)
    mob = jnp.dot(ym_ref[...], wm_ref[...], preferred_element_type=F32)
    merged = _sigmoid(gr_ref[...].astype(F32)) * ret + _sigmoid(gm_ref[...].astype(F32)) * mob
    x1 = x_ref[...] + jnp.dot(merged.astype(BF16), wo_ref[...], preferred_element_type=F32)
    x1_ref[...] = x1
    r = lax.rsqrt(jnp.mean(x1 * x1, axis=-1, keepdims=True) + EPS)
    h2_ref[...] = (x1 * r * nw_ref[...]).astype(h2_ref.dtype)


def _merge(y_ret, y_moba, plain, x, w_ret_up, w_moba_up, w_out, norm_w, tm=256):
    t, d = x.shape
    resident = lambda shape: pl.BlockSpec(shape, lambda i: (0, 0), pipeline_mode=pl.Buffered(1))
    return pl.pallas_call(
        _merge_kernel,
        grid=(t // tm,),
        in_specs=[pl.BlockSpec((tm, RET_WIDTH), lambda i: (i, 0)),
                  pl.BlockSpec((tm, MOBA_WIDTH), lambda i: (i, 0)),
                  pl.BlockSpec((tm, d), lambda i: (i, 0)),
                  pl.BlockSpec((tm, d), lambda i: (i, 1)),
                  pl.BlockSpec((tm, d), lambda i: (i, 0)),
                  resident(w_ret_up.shape), resident(w_moba_up.shape), resident(w_out.shape),
                  pl.BlockSpec((1, d), lambda i: (0, 0))],
        out_specs=[pl.BlockSpec((tm, d), lambda i: (i, 0)), pl.BlockSpec((tm, d), lambda i: (i, 0))],
        out_shape=[jax.ShapeDtypeStruct((t, d), F32), jax.ShapeDtypeStruct((t, d), BF16)],
        compiler_params=_params(("parallel",), 56),
        name="merge_outproj",
    )(y_ret, y_moba, plain, plain, x, w_ret_up, w_moba_up, w_out, norm_w.reshape(1, d))


def _ffn_up_kernel(h_ref, wa_ref, wb_ref, cwa_ref, cwb_ref, cba_ref, cbb_ref, o_ref, ua_ref, ub_ref,
                   wa16_ref, wb16_ref, *, tiles_per_seq):
    tm = h_ref.shape[0]
    pad = 8
    i = pl.program_id(1)

    @pl.when(i == 0)
    def _():
        wa16_ref[...] = wa_ref[...].astype(wa16_ref.dtype)
        wb16_ref[...] = wb_ref[...].astype(wb16_ref.dtype)

    @pl.when(i % tiles_per_seq == 0)
    def _():
        ua_ref[0:pad, :] = jnp.zeros((pad, ua_ref.shape[1]), F32)
        ub_ref[0:pad, :] = jnp.zeros((pad, ub_ref.shape[1]), F32)

    h = h_ref[...]
    ua_ref[pad:pad + tm, :] = jnp.dot(h, wa16_ref[...], preferred_element_type=F32)
    ub_ref[pad:pad + tm, :] = jnp.dot(h, wb16_ref[...], preferred_element_type=F32)

    def conv(u_ref, cw_ref, cb_ref):
        y = cb_ref[...]
        for j in range(CONV_WIDTH):
            off = pad - (CONV_WIDTH - 1) + j
            y = y + cw_ref[j:j + 1, :] * u_ref[off:off + tm, :]
        return y

    a = conv(ua_ref, cwa_ref, cba_ref)
    b = conv(ub_ref, cwb_ref, cbb_ref)
    o_ref[...] = (_silu(a) * b).astype(o_ref.dtype)
    ua_ref[0:pad, :] = ua_ref[tm:tm + pad, :]
    ub_ref[0:pad, :] = ub_ref[tm:tm + pad, :]


def _ffn_up(h2, w_up, conv_w, conv_b, seq, tm=1024, tn=512):
    t, d = h2.shape
    d_ff = w_up.shape[1] // 2
    nj = d_ff // tn
    conv_b = conv_b.reshape(1, 2 * d_ff)
    return pl.pallas_call(
        functools.partial(_ffn_up_kernel, tiles_per_seq=seq // tm),
        grid=(nj, t // tm),
        in_specs=[pl.BlockSpec((tm, d), lambda j, i: (i, 0)),
                  pl.BlockSpec((d, tn), lambda j, i: (0, j)),
                  pl.BlockSpec((d, tn), lambda j, i: (0, nj + j)),
                  pl.BlockSpec((CONV_WIDTH, tn), lambda j, i: (0, j)),
                  pl.BlockSpec((CONV_WIDTH, tn), lambda j, i: (0, nj + j)),
                  pl.BlockSpec((1, tn), lambda j, i: (0, j)),
                  pl.BlockSpec((1, tn), lambda j, i: (0, nj + j))],
        out_specs=pl.BlockSpec((tm, tn), lambda j, i: (i, j)),
        out_shape=jax.ShapeDtypeStruct((t, d_ff), BF16),
        scratch_shapes=[pltpu.VMEM((tm + 8, tn), F32), pltpu.VMEM((tm + 8, tn), F32),
                        pltpu.VMEM((d, tn), BF16), pltpu.VMEM((d, tn), BF16)],
        compiler_params=_params(("arbitrary", "arbitrary"), 52),
        name="ffn_up_conv_gate",
    )(h2, w_up, w_up, conv_w, conv_w, conv_b, conv_b)


def _ffn_down_kernel(a_ref, w_ref, x_ref, nw_ref, o_ref, *, final_norm):
    x2 = x_ref[...] + jnp.dot(a_ref[...], w_ref[...], preferred_element_type=F32)
    if final_norm:
        r = lax.rsqrt(jnp.mean(x2 * x2, axis=-1, keepdims=True) + EPS)
        x2 = x2 * r * nw_ref[...]
    o_ref[...] = x2


def _ffn_down(act, w_down, x1, norm_w, final_norm, tm=256):
    t, d = x1.shape
    d_ff = act.shape[1]
    return pl.pallas_call(
        functools.partial(_ffn_down_kernel, final_norm=final_norm),
        grid=(t // tm,),
        in_specs=[pl.BlockSpec((tm, d_ff), lambda i: (i, 0)),
                  pl.BlockSpec((d_ff, d), lambda i: (0, 0), pipeline_mode=pl.Buffered(1)),
                  pl.BlockSpec((tm, d), lambda i: (i, 0)),
                  pl.BlockSpec((1, d), lambda i: (0, 0))],
        out_specs=pl.BlockSpec((tm, d), lambda i: (i, 0)),
        out_shape=jax.ShapeDtypeStruct((t, d), F32),
        compiler_params=_params(("parallel",), 56),
        name="ffn_down",
    )(act, w_down, x1, norm_w.reshape(1, d))


def kernel(x, attn_norm_w, w_in, ret_norm_w, w_ret_up, w_moba_up, w_out, ffn_norm_w, w_ffn_up, conv_w,
           conv_b, w_ffn_down, final_norm_w):
    batch, seq, d = x.shape
    depth = w_in.shape[0]
    t = batch * seq
    xf = x.reshape(t, d)
    tm = 1024
    tiles_per_seq = seq // tm

    ret_cos, ret_sin = _ret_rope_tables(seq)
    mob_c, mob_sa, mob_sb = _moba_rope_tables(seq)
    ret_specs = [pl.BlockSpec((tm, LANES), lambda j, i: (i % tiles_per_seq, 0))] * 2
    mob_specs = [pl.BlockSpec((None, tm, LANES), lambda j, i: (j, i % tiles_per_seq, 0))] * 3

    tn = 1024
    assert RET_WIDTH == tn and MOBA_WIDTH == tn and d == 2 * tn
    plain_block = lambda j: jnp.where(j < 4, j + 7, jnp.where(j < 6, j - 2, 6))
    for l in range(depth):
        h = _rmsnorm(xf, attn_norm_w[l], BF16)
        ret_qk = _in_proj(h, w_in[l], lambda j: j, 2 * tn, _ret_rotary_epilogue, (ret_cos, ret_sin),
                          ret_specs, tm, tn, "in_proj_ret")
        mob_qk = _in_proj(h, w_in[l], lambda j: j + 4, 2 * tn, _moba_rotary_epilogue,
                          (mob_c, mob_sa, mob_sb), mob_specs, tm, tn, "in_proj_moba")
        plain = _in_proj(h, w_in[l], plain_block, 7 * tn, _plain_epilogue, (), [], tm, tn, "in_proj_plain")

        y_ret = _retention(ret_qk, plain, ret_norm_w[l], batch, seq)
        y_moba = _moba(mob_qk, plain, batch, seq)
        x1, h2 = _merge(y_ret, y_moba, plain, xf, w_ret_up[l].astype(BF16), w_moba_up[l].astype(BF16),
                        w_out[l].astype(BF16), ffn_norm_w[l])
        act = _ffn_up(h2, w_ffn_up[l], conv_w[l], conv_b[l], seq)
        last = l == depth - 1
        xf = _ffn_down(act, w_ffn_down[l].astype(BF16), x1, final_norm_w, final_norm=last)
    return xf.reshape(batch, seq, d)
```

```python
import functools
import math

import numpy as np
import jax
import jax.numpy as jnp
from jax import lax
from jax.experimental import pallas as pl
from jax.experimental.pallas import tpu as pltpu

RET_HEADS = 4
RET_HEAD_DIM = 256
RET_WIDTH = RET_HEADS * RET_HEAD_DIM
RET_ROPE_THETA = 10000.0
RET_TILE = 256
MOBA_HEADS = 8
MOBA_HEAD_DIM = 128
MOBA_WIDTH = MOBA_HEADS * MOBA_HEAD_DIM
MOBA_BLOCK = 256
MOBA_TOPK = 3
ROPE_THETA = 500000.0
ROT_DIM = MOBA_HEAD_DIM // 4
CONV_WIDTH = 3
EPS = 1e-6

LANES = 128
MASK_PENALTY = -1e30
BF16 = jnp.bfloat16
F32 = jnp.float32

NT_DIMS = (((1,), (1,)), ((), ()))
TN_DIMS = (((0,), (0,)), ((), ()))


def _params(semantics, vmem_mib):
    return pltpu.CompilerParams(dimension_semantics=semantics, vmem_limit_bytes=vmem_mib * 1024 * 1024)


def _sigmoid(x):
    return 0.5 * jnp.tanh(0.5 * x) + 0.5


def _silu(x):
    h = 0.5 * x
    return h + h * jnp.tanh(h)


def _rmsnorm_kernel(x_ref, w_ref, o_ref):
    x = x_ref[...]
    r = lax.rsqrt(jnp.mean(x * x, axis=-1, keepdims=True) + EPS)
    o_ref[...] = (x * r * w_ref[...]).astype(o_ref.dtype)


def _rmsnorm(x, w, out_dtype, tm=512):
    t, d = x.shape
    return pl.pallas_call(
        _rmsnorm_kernel,
        grid=(t // tm,),
        in_specs=[pl.BlockSpec((tm, d), lambda i: (i, 0)), pl.BlockSpec((1, d), lambda i: (0, 0))],
        out_specs=pl.BlockSpec((tm, d), lambda i: (i, 0)),
        out_shape=jax.ShapeDtypeStruct((t, d), out_dtype),
        compiler_params=_params(("parallel",), 32),
        name="rmsnorm",
    )(x, w.reshape(1, d))


def _plain_epilogue(acc, o_ref):
    o_ref[...] = acc.astype(o_ref.dtype)


def _ret_rotary_epilogue(acc, cos_ref, sin_ref, o_ref):
    cos = cos_ref[...]
    sin = sin_ref[...]
    half = RET_HEAD_DIM // 2
    for hh in range(acc.shape[1] // RET_HEAD_DIM):
        c0 = hh * RET_HEAD_DIM
        x1 = acc[:, c0:c0 + half]
        x2 = acc[:, c0 + half:c0 + RET_HEAD_DIM]
        o_ref[:, c0:c0 + half] = (x1 * cos - x2 * sin).astype(o_ref.dtype)
        o_ref[:, c0 + half:c0 + RET_HEAD_DIM] = (x2 * cos + x1 * sin).astype(o_ref.dtype)


def _moba_rotary_epilogue(acc, c_ref, sa_ref, sb_ref, o_ref):
    c = c_ref[...]
    sa = sa_ref[...]
    sb = sb_ref[...]
    half = ROT_DIM // 2
    for hh in range(acc.shape[1] // MOBA_HEAD_DIM):
        c0 = hh * MOBA_HEAD_DIM
        x = acc[:, c0:c0 + MOBA_HEAD_DIM]
        up = pltpu.roll(x, MOBA_HEAD_DIM - half, axis=1)
        dn = pltpu.roll(x, half, axis=1)
        o_ref[:, c0:c0 + MOBA_HEAD_DIM] = (x * c + up * sa + dn * sb).astype(o_ref.dtype)


def _in_proj_kernel(h_ref, w_ref, *rest, epilogue):
    wb_ref = rest[-1]

    @pl.when(pl.program_id(1) == 0)
    def _():
        wb_ref[...] = w_ref[...].astype(wb_ref.dtype)

    acc = jnp.dot(h_ref[...], wb_ref[...], preferred_element_type=F32)
    epilogue(acc, *rest[:-1])


def _in_proj(h, w, col_block, n_out, epilogue, tables, table_specs, tm, tn, name):
    t, d = h.shape
    return pl.pallas_call(
        functools.partial(_in_proj_kernel, epilogue=epilogue),
        grid=(n_out // tn, t // tm),
        in_specs=[pl.BlockSpec((tm, d), lambda j, i: (i, 0)),
                  pl.BlockSpec((d, tn), lambda j, i: (0, col_block(j)))] + table_specs,
        out_specs=pl.BlockSpec((tm, tn), lambda j, i: (i, j)),
        out_shape=jax.ShapeDtypeStruct((t, n_out), BF16),
        scratch_shapes=[pltpu.VMEM((d, tn), BF16)],
        compiler_params=_params(("arbitrary", "arbitrary"), 52),
        name=name,
    )(h, w, *tables)


def _rope_angles(seq, rot_dim, theta):
    pos = np.arange(seq, dtype=np.float64)
    inv = np.float64(theta) ** (-np.arange(0, rot_dim, 2, dtype=np.float64) / rot_dim)
    return pos[:, None] * inv[None, :]


def _ret_rope_tables(seq):
    ang = _rope_angles(seq, RET_HEAD_DIM, RET_ROPE_THETA)
    return jnp.asarray(np.cos(ang), F32), jnp.asarray(np.sin(ang), F32)


def _moba_rope_tables(seq):
    half = ROT_DIM // 2
    ang = _rope_angles(seq, ROT_DIM, ROPE_THETA)
    cos, sin = np.cos(ang), np.sin(ang)
    c = np.ones((seq, MOBA_HEAD_DIM))
    sa = np.zeros((seq, MOBA_HEAD_DIM))
    sb = np.zeros((seq, MOBA_HEAD_DIM))
    c[:, :half] = cos
    c[:, half:ROT_DIM] = cos
    sa[:, :half] = -sin
    sb[:, half:ROT_DIM] = sin
    q_scale = MOBA_HEAD_DIM ** -0.5 * math.log2(math.e)
    stack = lambda a: jnp.asarray(np.stack([a * q_scale, a]), F32)
    return stack(c), stack(sa), stack(sb)


def _retention_kernel(q_ref, k_ref, v_ref, g_ref, nw_ref, o_ref, state_ref, dec_ref, xi_ref, zeta_ref):
    c = pl.program_id(1)
    tile = RET_TILE
    k_scale = RET_HEAD_DIM ** -0.5
    log_gs = [float(np.log1p(-np.exp2(-5.0 - hh))) for hh in range(RET_HEADS)]

    @pl.when(c == 0)
    def _():
        state_ref[...] = jnp.zeros_like(state_ref)
        row = lax.broadcasted_iota(jnp.int32, (tile, tile), 0).astype(F32)
        col = lax.broadcasted_iota(jnp.int32, (tile, tile), 1).astype(F32)
        diff = row - col
        causal = diff >= 0
        row_d = lax.broadcasted_iota(jnp.int32, (tile, RET_HEAD_DIM), 0).astype(F32)
        for hh, log_g in enumerate(log_gs):
            dec_ref[hh] = jnp.where(causal, jnp.exp(log_g * jnp.where(causal, diff, 0.0)), 0.0) * k_scale
            xi_ref[hh] = jnp.exp(log_g * (row_d + 1.0))
            zeta_ref[hh] = jnp.exp(log_g * (tile - 1.0 - row_d)) * k_scale

    for hh, log_g in enumerate(log_gs):
        g_chunk = float(np.exp(log_g * tile))
        sl = slice(hh * RET_HEAD_DIM, (hh + 1) * RET_HEAD_DIM)
        q = q_ref[:, sl]
        k = k_ref[:, sl]
        v = v_ref[:, sl]
        state = state_ref[hh]
        scores = lax.dot_general(q, k, NT_DIMS, preferred_element_type=F32) * dec_ref[hh]
        inner = jnp.dot(scores.astype(BF16), v, preferred_element_type=F32)
        cross = jnp.dot(q, state.astype(BF16), preferred_element_type=F32) * xi_ref[hh]
        kz = (k.astype(F32) * zeta_ref[hh]).astype(BF16)
        state_ref[hh] = state * g_chunk + lax.dot_general(kz, v, TN_DIMS, preferred_element_type=F32)
        o = inner + cross
        o = o * lax.rsqrt(jnp.mean(o * o, axis=-1, keepdims=True) + EPS)
        o_ref[:, sl] = (_silu(g_ref[:, sl].astype(F32)) * (o * nw_ref[:, sl])).astype(o_ref.dtype)


def _retention(qk, plain, norm_w, batch, seq):
    t = qk.shape[0]
    n_tiles = seq // RET_TILE
    row = lambda b, c: b * n_tiles + c
    hd = RET_HEAD_DIM
    return pl.pallas_call(
        _retention_kernel,
        grid=(batch, n_tiles),
        in_specs=[pl.BlockSpec((RET_TILE, RET_WIDTH), lambda b, c: (row(b, c), 0)),
                  pl.BlockSpec((RET_TILE, RET_WIDTH), lambda b, c: (row(b, c), 1)),
                  pl.BlockSpec((RET_TILE, RET_WIDTH), lambda b, c: (row(b, c), 4)),
                  pl.BlockSpec((RET_TILE, RET_WIDTH), lambda b, c: (row(b, c), 5)),
                  pl.BlockSpec((1, RET_WIDTH), lambda b, c: (0, 0))],
        out_specs=pl.BlockSpec((RET_TILE, RET_WIDTH), lambda b, c: (row(b, c), 0)),
        out_shape=jax.ShapeDtypeStruct((t, RET_WIDTH), BF16),
        scratch_shapes=[pltpu.VMEM((RET_HEADS, hd, hd), F32),
                        pltpu.VMEM((RET_HEADS, RET_TILE, RET_TILE), F32),
                        pltpu.VMEM((RET_HEADS, RET_TILE, hd), F32),
                        pltpu.VMEM((RET_HEADS, RET_TILE, hd), F32)],
        compiler_params=_params(("parallel", "arbitrary"), 32),
        name="retention",
    )(qk, qk, plain, plain, norm_w.reshape(1, RET_WIDTH))


def _moba_kernel(q_ref, k_ref, v_ref, o_ref, qa_ref, ka_ref, va_ref):
    seq, d = q_ref.shape
    bs = MOBA_BLOCK
    bs_shift = bs.bit_length() - 1
    assert bs == 1 << bs_shift
    nb = seq // bs
    q = q_ref[...]
    k = k_ref[...]

    kmean = jnp.sum(k.astype(F32).reshape(nb, bs, d), axis=1) * (1.0 / bs)
    km_hi = kmean.astype(BF16)
    km_lo = (kmean - km_hi.astype(F32)).astype(BF16)
    score = (lax.dot_general(km_hi, q, NT_DIMS, preferred_element_type=F32)
             + lax.dot_general(km_lo, q, NT_DIMS, preferred_element_type=F32))
    blk = lax.broadcasted_iota(jnp.int32, (nb, seq), 0)
    own = lax.broadcasted_iota(jnp.int32, (nb, seq), 1) >> bs_shift
    past = blk < own
    score = jnp.where(past, score, -jnp.inf)
    rank = jnp.zeros((nb, seq), jnp.int32)
    for r in range(1, nb):
        other = pltpu.roll(score, r, axis=0)
        beats = (other > score) | ((other == score) & (blk >= r))
        rank = rank + beats.astype(jnp.int32)
    keep = (past & (rank < MOBA_TOPK)) | (blk == own)
    pen_t = jnp.where(keep, 0.0, MASK_PENALTY)
    pen_t = jnp.concatenate([pen_t, jnp.zeros((LANES - nb, seq), F32)], axis=0)
    pen = pen_t.T.astype(BF16)

    qa_ref[:, :d] = q
    qa_ref[:, d:] = pen
    key_blk = lax.broadcasted_iota(jnp.int32, (seq, LANES), 0) >> bs_shift
    lane = lax.broadcasted_iota(jnp.int32, (seq, LANES), 1)
    ka_ref[:, :d] = k
    ka_ref[:, d:] = jnp.where(lane == key_blk, 1.0, 0.0).astype(BF16)
    va_ref[:, :d] = v_ref[...]
    va_ref[:, d:] = jnp.ones((seq, LANES), BF16)

    qpos = lax.broadcasted_iota(jnp.int32, (bs, bs), 0)
    kpos = lax.broadcasted_iota(jnp.int32, (bs, bs), 1)
    def scores(qi):
        n_keys = (qi + 1) * bs
        return lax.dot_general(qa_ref[qi * bs:n_keys, :], ka_ref[0:n_keys, :], NT_DIMS,
                               preferred_element_type=F32)

    s_next = scores(0)
    for qi in range(nb):
        q0 = qi * bs
        n_keys = q0 + bs
        s = s_next
        if qi + 1 < nb:
            s_next = scores(qi + 1)
        s_own = jnp.where(kpos <= qpos, s[:, q0:], -jnp.inf)
        s = s_own if qi == 0 else jnp.concatenate([s[:, :q0], s_own], axis=1)
        p = jnp.exp2((s - jnp.max(s, axis=-1, keepdims=True)).astype(BF16))
        acc = jnp.dot(p, va_ref[0:n_keys, :], preferred_element_type=F32)
        o_ref[q0:n_keys, :] = (acc[:, :d] / acc[:, d:]).astype(o_ref.dtype)


def _moba(mqk, plain, batch, seq):
    t = mqk.shape[0]
    d = MOBA_HEAD_DIM
    v_blk0 = 6144 // d
    return pl.pallas_call(
        _moba_kernel,
        grid=(batch, MOBA_HEADS),
        in_specs=[pl.BlockSpec((seq, d), lambda b, h: (b, h)),
                  pl.BlockSpec((seq, d), lambda b, h: (b, MOBA_HEADS + h)),
                  pl.BlockSpec((seq, d), lambda b, h: (b, v_blk0 + h))],
        out_specs=pl.BlockSpec((seq, d), lambda b, h: (b, h)),
        out_shape=jax.ShapeDtypeStruct((t, MOBA_WIDTH), BF16),
        scratch_shapes=[pltpu.VMEM((seq, d + LANES), BF16)] * 3,
        compiler_params=_params(("parallel", "parallel"), 48),
        name="moba",
    )(mqk, mqk, plain)


def _merge_kernel(yr_ref, ym_ref, gr_ref, gm_ref, x_ref, wr_ref, wm_ref, wo_ref, nw_ref, x1_ref, h2_ref):
    ret = jnp.dot(yr_ref[...], wr_ref[...], preferred_element_type=F32)
    mob = jnp.dot(ym_ref[...], wm_ref[...], preferred_element_type=F32)
    merged = _sigmoid(gr_ref[...].astype(F32)) * ret + _sigmoid(gm_ref[...].astype(F32)) * mob
    x1 = x_ref[...] + jnp.dot(merged.astype(BF16), wo_ref[...], preferred_element_type=F32)
    x1_ref[...] = x1
    r = lax.rsqrt(jnp.mean(x1 * x1, axis=-1, keepdims=True) + EPS)
    h2_ref[...] = (x1 * r * nw_ref[...]).astype(h2_ref.dtype)


def _merge(y_ret, y_moba, plain, x, w_ret_up, w_moba_up, w_out, norm_w, tm=256):
    t, d = x.shape
    resident = lambda shape: pl.BlockSpec(shape, lambda i: (0, 0), pipeline_mode=pl.Buffered(1))
    return pl.pallas_call(
        _merge_kernel,
        grid=(t // tm,),
        in_specs=[pl.BlockSpec((tm, RET_WIDTH), lambda i: (i, 0)),
                  pl.BlockSpec((tm, MOBA_WIDTH), lambda i: (i, 0)),
                  pl.BlockSpec((tm, d), lambda i: (i, 0)),
                  pl.BlockSpec((tm, d), lambda i: (i, 1)),
                  pl.BlockSpec((tm, d), lambda i: (i, 0)),
                  resident(w_ret_up.shape), resident(w_moba_up.shape), resident(w_out.shape),
                  pl.BlockSpec((1, d), lambda i: (0, 0))],
        out_specs=[pl.BlockSpec((tm, d), lambda i: (i, 0)), pl.BlockSpec((tm, d), lambda i: (i, 0))],
        out_shape=[jax.ShapeDtypeStruct((t, d), F32), jax.ShapeDtypeStruct((t, d), BF16)],
        compiler_params=_params(("parallel",), 56),
        name="merge_outproj",
    )(y_ret, y_moba, plain, plain, x, w_ret_up, w_moba_up, w_out, norm_w.reshape(1, d))


def _ffn_up_kernel(h_ref, wa_ref, wb_ref, cwa_ref, cwb_ref, cba_ref, cbb_ref, o_ref, ua_ref, ub_ref,
                   wa16_ref, wb16_ref, *, tiles_per_seq):
    tm = h_ref.shape[0]
    pad = 8
    i = pl.program_id(1)

    @pl.when(i == 0)
    def _():
        wa16_ref[...] = wa_ref[...].astype(wa16_ref.dtype)
        wb16_ref[...] = wb_ref[...].astype(wb16_ref.dtype)

    @pl.when(i % tiles_per_seq == 0)
    def _():
        ua_ref[0:pad, :] = jnp.zeros((pad, ua_ref.shape[1]), F32)
        ub_ref[0:pad, :] = jnp.zeros((pad, ub_ref.shape[1]), F32)

    h = h_ref[...]
    ua_ref[pad:pad + tm, :] = jnp.dot(h, wa16_ref[...], preferred_element_type=F32)
    ub_ref[pad:pad + tm, :] = jnp.dot(h, wb16_ref[...], preferred_element_type=F32)

    def conv(u_ref, cw_ref, cb_ref):
        y = cb_ref[...]
        for j in range(CONV_WIDTH):
            off = pad - (CONV_WIDTH - 1) + j
            y = y + cw_ref[j:j + 1, :] * u_ref[off:off + tm, :]
        return y

    a = conv(ua_ref, cwa_ref, cba_ref)
    b = conv(ub_ref, cwb_ref, cbb_ref)
    o_ref[...] = (_silu(a) * b).astype(o_ref.dtype)
    ua_ref[0:pad, :] = ua_ref[tm:tm + pad, :]
    ub_ref[0:pad, :] = ub_ref[tm:tm + pad, :]


def _ffn_up(h2, w_up, conv_w, conv_b, seq, tm=1024, tn=512):
    t, d = h2.shape
    d_ff = w_up.shape[1] // 2
    nj = d_ff // tn
    conv_b = conv_b.reshape(1, 2 * d_ff)
    return pl.pallas_call(
        functools.partial(_ffn_up_kernel, tiles_per_seq=seq // tm),
        grid=(nj, t // tm),
        in_specs=[pl.BlockSpec((tm, d), lambda j, i: (i, 0)),
                  pl.BlockSpec((d, tn), lambda j, i: (0, j)),
                  pl.BlockSpec((d, tn), lambda j, i: (0, nj + j)),
                  pl.BlockSpec((CONV_WIDTH, tn), lambda j, i: (0, j)),
                  pl.BlockSpec((CONV_WIDTH, tn), lambda j, i: (0, nj + j)),
                  pl.BlockSpec((1, tn), lambda j, i: (0, j)),
                  pl.BlockSpec((1, tn), lambda j, i: (0, nj + j))],
        out_specs=pl.BlockSpec((tm, tn), lambda j, i: (i, j)),
        out_shape=jax.ShapeDtypeStruct((t, d_ff), BF16),
        scratch_shapes=[pltpu.VMEM((tm + 8, tn), F32), pltpu.VMEM((tm + 8, tn), F32),
                        pltpu.VMEM((d, tn), BF16), pltpu.VMEM((d, tn), BF16)],
        compiler_params=_params(("arbitrary", "arbitrary"), 52),
        name="ffn_up_conv_gate",
    )(h2, w_up, w_up, conv_w, conv_w, conv_b, conv_b)


def _ffn_down_kernel(a_ref, w_ref, x_ref, nw_ref, o_ref, *, final_norm):
    x2 = x_ref[...] + jnp.dot(a_ref[...], w_ref[...], preferred_element_type=F32)
    if final_norm:
        r = lax.rsqrt(jnp.mean(x2 * x2, axis=-1, keepdims=True) + EPS)
        x2 = x2 * r * nw_ref[...]
    o_ref[...] = x2


def _ffn_down(act, w_down, x1, norm_w, final_norm, tm=256):
    t, d = x1.shape
    d_ff = act.shape[1]
    return pl.pallas_call(
        functools.partial(_ffn_down_kernel, final_norm=final_norm),
        grid=(t // tm,),
        in_specs=[pl.BlockSpec((tm, d_ff), lambda i: (i, 0)),
                  pl.BlockSpec((d_ff, d), lambda i: (0, 0), pipeline_mode=pl.Buffered(1)),
                  pl.BlockSpec((tm, d), lambda i: (i, 0)),
                  pl.BlockSpec((1, d), lambda i: (0, 0))],
        out_specs=pl.BlockSpec((tm, d), lambda i: (i, 0)),
        out_shape=jax.ShapeDtypeStruct((t, d), F32),
        compiler_params=_params(("parallel",), 56),
        name="ffn_down",
    )(act, w_down, x1, norm_w.reshape(1, d))


def kernel(x, attn_norm_w, w_in, ret_norm_w, w_ret_up, w_moba_up, w_out, ffn_norm_w, w_ffn_up, conv_w,
           conv_b, w_ffn_down, final_norm_w):
    batch, seq, d = x.shape
    depth = w_in.shape[0]
    t = batch * seq
    xf = x.reshape(t, d)
    tm = 1024
    tiles_per_seq = seq // tm

    ret_cos, ret_sin = _ret_rope_tables(seq)
    mob_c, mob_sa, mob_sb = _moba_rope_tables(seq)
    ret_specs = [pl.BlockSpec((tm, LANES), lambda j, i: (i % tiles_per_seq, 0))] * 2
    mob_specs = [pl.BlockSpec((None, tm, LANES), lambda j, i: (j, i % tiles_per_seq, 0))] * 3

    tn = 1024
    assert RET_WIDTH == tn and MOBA_WIDTH == tn and d == 2 * tn
    plain_block = lambda j: jnp.where(j < 4, j + 7, jnp.where(j < 6, j - 2, 6))
    for l in range(depth):
        h = _rmsnorm(xf, attn_norm_w[l], BF16)
        ret_qk = _in_proj(h, w_in[l], lambda j: j, 2 * tn, _ret_rotary_epilogue, (ret_cos, ret_sin),
                          ret_specs, tm, tn, "in_proj_ret")
        mob_qk = _in_proj(h, w_in[l], lambda j: j + 4, 2 * tn, _moba_rotary_epilogue,
                          (mob_c, mob_sa, mob_sb), mob_specs, tm, tn, "in_proj_moba")
        plain = _in_proj(h, w_in[l], plain_block, 7 * tn, _plain_epilogue, (), [], tm, tn, "in_proj_plain")

        y_ret = _retention(ret_qk, plain, ret_norm_w[l], batch, seq)
        y_moba = _moba(mob_qk, plain, batch, seq)
        x1, h2 = _merge(y_ret, y_moba, plain, xf, w_ret_up[l].astype(BF16), w_moba_up[l].astype(BF16),
                        w_out[l].astype(BF16), ffn_norm_w[l])
        act = _ffn_up(h2, w_ffn_up[l], conv_w[l], conv_b[l], seq)
        last = l == depth - 1
        xf = _ffn_down(act, w_ffn_down[l].astype(BF16), x1, final_norm_w, final_norm=last)
    return xf.reshape(batch, seq, d)
```

```python
import functools
import math

import numpy as np
import jax
import jax.numpy as jnp
from jax import lax
from jax.experimental import pallas as pl
from jax.experimental.pallas import tpu as pltpu

RET_HEADS = 4
RET_HEAD_DIM = 256
RET_WIDTH = RET_HEADS * RET_HEAD_DIM
RET_ROPE_THETA = 10000.0
RET_TILE = 256
RET_STEP_ROWS = 1024
MOBA_HEADS = 8
MOBA_HEAD_DIM = 128
MOBA_WIDTH = MOBA_HEADS * MOBA_HEAD_DIM
MOBA_BLOCK = 256
MOBA_TOPK = 3
ROPE_THETA = 500000.0
ROT_DIM = MOBA_HEAD_DIM // 4
CONV_WIDTH = 3
EPS = 1e-6

LANES = 128
MASK_PENALTY = -1e30
BF16 = jnp.bfloat16
F32 = jnp.float32

NT_DIMS = (((1,), (1,)), ((), ()))
TN_DIMS = (((0,), (0,)), ((), ()))


def _params(semantics, vmem_mib):
    return pltpu.CompilerParams(dimension_semantics=semantics, vmem_limit_bytes=vmem_mib * 1024 * 1024)


def _sigmoid(x):
    return 0.5 * jnp.tanh(0.5 * x) + 0.5


def _silu(x):
    h = 0.5 * x
    return h + h * jnp.tanh(h)


def _rmsnorm_kernel(x_ref, w_ref, o_ref):
    x = x_ref[...]
    r = lax.rsqrt(jnp.mean(x * x, axis=-1, keepdims=True) + EPS)
    o_ref[...] = (x * r * w_ref[...]).astype(o_ref.dtype)


def _rmsnorm(x, w, out_dtype, tm=512):
    t, d = x.shape
    return pl.pallas_call(
        _rmsnorm_kernel,
        grid=(t // tm,),
        in_specs=[pl.BlockSpec((tm, d), lambda i: (i, 0)), pl.BlockSpec((1, d), lambda i: (0, 0))],
        out_specs=pl.BlockSpec((tm, d), lambda i: (i, 0)),
        out_shape=jax.ShapeDtypeStruct((t, d), out_dtype),
        compiler_params=_params(("parallel",), 32),
        name="rmsnorm",
    )(x, w.reshape(1, d))


def _plain_epilogue(acc, o_ref):
    o_ref[...] = acc.astype(o_ref.dtype)


def _ret_rotary_epilogue(acc, cos_ref, sin_ref, o_ref):
    cos = cos_ref[...]
    sin = sin_ref[...]
    half = RET_HEAD_DIM // 2
    for hh in range(acc.shape[1] // RET_HEAD_DIM):
        c0 = hh * RET_HEAD_DIM
        x1 = acc[:, c0:c0 + half]
        x2 = acc[:, c0 + half:c0 + RET_HEAD_DIM]
        o_ref[:, c0:c0 + half] = (x1 * cos - x2 * sin).astype(o_ref.dtype)
        o_ref[:, c0 + half:c0 + RET_HEAD_DIM] = (x2 * cos + x1 * sin).astype(o_ref.dtype)


def _moba_rotary_epilogue(acc, c_ref, sa_ref, sb_ref, o_ref):
    c = c_ref[...]
    sa = sa_ref[...]
    sb = sb_ref[...]
    half = ROT_DIM // 2
    for hh in range(acc.shape[1] // MOBA_HEAD_DIM):
        c0 = hh * MOBA_HEAD_DIM
        x = acc[:, c0:c0 + MOBA_HEAD_DIM]
        up = pltpu.roll(x, MOBA_HEAD_DIM - half, axis=1)
        dn = pltpu.roll(x, half, axis=1)
        o_ref[:, c0:c0 + MOBA_HEAD_DIM] = (x * c + up * sa + dn * sb).astype(o_ref.dtype)


def _in_proj_kernel(h_ref, w_ref, *rest, epilogue):
    wb_ref = rest[-1]

    @pl.when(pl.program_id(1) == 0)
    def _():
        wb_ref[...] = w_ref[...].astype(wb_ref.dtype)

    acc = jnp.dot(h_ref[...], wb_ref[...], preferred_element_type=F32)
    epilogue(acc, *rest[:-1])


def _in_proj(h, w, col_block, n_out, epilogue, tables, table_specs, tm, tn, name):
    t, d = h.shape
    return pl.pallas_call(
        functools.partial(_in_proj_kernel, epilogue=epilogue),
        grid=(n_out // tn, t // tm),
        in_specs=[pl.BlockSpec((tm, d), lambda j, i: (i, 0)),
                  pl.BlockSpec((d, tn), lambda j, i: (0, col_block(j)))] + table_specs,
        out_specs=pl.BlockSpec((tm, tn), lambda j, i: (i, j)),
        out_shape=jax.ShapeDtypeStruct((t, n_out), BF16),
        scratch_shapes=[pltpu.VMEM((d, tn), BF16)],
        compiler_params=_params(("arbitrary", "arbitrary"), 52),
        name=name,
    )(h, w, *tables)


def _rope_angles(seq, rot_dim, theta):
    pos = np.arange(seq, dtype=np.float64)
    inv = np.float64(theta) ** (-np.arange(0, rot_dim, 2, dtype=np.float64) / rot_dim)
    return pos[:, None] * inv[None, :]


def _ret_rope_tables(seq):
    ang = _rope_angles(seq, RET_HEAD_DIM, RET_ROPE_THETA)
    return jnp.asarray(np.cos(ang), F32), jnp.asarray(np.sin(ang), F32)


def _moba_rope_tables(seq):
    half = ROT_DIM // 2
    ang = _rope_angles(seq, ROT_DIM, ROPE_THETA)
    cos, sin = np.cos(ang), np.sin(ang)
    c = np.ones((seq, MOBA_HEAD_DIM))
    sa = np.zeros((seq, MOBA_HEAD_DIM))
    sb = np.zeros((seq, MOBA_HEAD_DIM))
    c[:, :half] = cos
    c[:, half:ROT_DIM] = cos
    sa[:, :half] = -sin
    sb[:, half:ROT_DIM] = sin
    q_scale = MOBA_HEAD_DIM ** -0.5 * math.log2(math.e)
    stack = lambda a: jnp.asarray(np.stack([a * q_scale, a]), F32)
    return stack(c), stack(sa), stack(sb)


def _retention_kernel(q_ref, k_ref, v_ref, g_ref, nw_ref, o_ref, state_ref, dec_ref, xi_ref, zeta_ref):
    c = pl.program_id(1)
    tile = RET_TILE
    k_scale = RET_HEAD_DIM ** -0.5
    log_gs = [float(np.log1p(-np.exp2(-5.0 - hh))) for hh in range(RET_HEADS)]

    @pl.when(c == 0)
    def _():
        state_ref[...] = jnp.zeros_like(state_ref)
        row = lax.broadcasted_iota(jnp.int32, (tile, tile), 0).astype(F32)
        col = lax.broadcasted_iota(jnp.int32, (tile, tile), 1).astype(F32)
        diff = row - col
        causal = diff >= 0
        row_d = lax.broadcasted_iota(jnp.int32, (tile, RET_HEAD_DIM), 0).astype(F32)
        for hh, log_g in enumerate(log_gs):
            dec_ref[hh] = jnp.where(causal, jnp.exp(log_g * jnp.where(causal, diff, 0.0)), 0.0) * k_scale
            xi_ref[hh] = jnp.exp(log_g * (row_d + 1.0))
            zeta_ref[hh] = jnp.exp(log_g * (tile - 1.0 - row_d)) * k_scale

    def recurrence(rows):
        outs = []
        for hh, log_g in enumerate(log_gs):
            g_chunk = float(np.exp(log_g * tile))
            sl = slice(hh * RET_HEAD_DIM, (hh + 1) * RET_HEAD_DIM)
            q = q_ref[rows, sl]
            k = k_ref[rows, sl]
            v = v_ref[rows, sl]
            state = state_ref[hh]
            scores = lax.dot_general(q, k, NT_DIMS, preferred_element_type=F32) * dec_ref[hh]
            inner = jnp.dot(scores.astype(BF16), v, preferred_element_type=F32)
            cross = jnp.dot(q, state.astype(BF16), preferred_element_type=F32) * xi_ref[hh]
            kz = (k.astype(F32) * zeta_ref[hh]).astype(BF16)
            state_ref[hh] = state * g_chunk + lax.dot_general(kz, v, TN_DIMS, preferred_element_type=F32)
            outs.append(inner + cross)
        return outs

    def epilogue(rows, outs):
        for hh, o in enumerate(outs):
            sl = slice(hh * RET_HEAD_DIM, (hh + 1) * RET_HEAD_DIM)
            o = o * lax.rsqrt(jnp.mean(o * o, axis=-1, keepdims=True) + EPS)
            o_ref[rows, sl] = (_silu(g_ref[rows, sl].astype(F32)) * (o * nw_ref[:, sl])).astype(o_ref.dtype)

    pending = None
    for sub in range(q_ref.shape[0] // tile):
        rows = slice(sub * tile, (sub + 1) * tile)
        outs = recurrence(rows)
        if pending is not None:
            epilogue(*pending)
        pending = (rows, outs)
    epilogue(*pending)


def _retention(qk, plain, norm_w, batch, seq):
    t = qk.shape[0]
    blk = RET_STEP_ROWS
    n_blk = seq // blk
    row = lambda b, c: b * n_blk + c
    hd = RET_HEAD_DIM
    return pl.pallas_call(
        _retention_kernel,
        grid=(batch, n_blk),
        in_specs=[pl.BlockSpec((blk, RET_WIDTH), lambda b, c: (row(b, c), 0)),
                  pl.BlockSpec((blk, RET_WIDTH), lambda b, c: (row(b, c), 1)),
                  pl.BlockSpec((blk, RET_WIDTH), lambda b, c: (row(b, c), 4)),
                  pl.BlockSpec((blk, RET_WIDTH), lambda b, c: (row(b, c), 5)),
                  pl.BlockSpec((1, RET_WIDTH), lambda b, c: (0, 0))],
        out_specs=pl.BlockSpec((blk, RET_WIDTH), lambda b, c: (row(b, c), 0)),
        out_shape=jax.ShapeDtypeStruct((t, RET_WIDTH), BF16),
        scratch_shapes=[pltpu.VMEM((RET_HEADS, hd, hd), F32),
                        pltpu.VMEM((RET_HEADS, RET_TILE, RET_TILE), F32),
                        pltpu.VMEM((RET_HEADS, RET_TILE, hd), F32),
                        pltpu.VMEM((RET_HEADS, RET_TILE, hd), F32)],
        compiler_params=_params(("parallel", "arbitrary"), 40),
        name="retention",
    )(qk, qk, plain, plain, norm_w.reshape(1, RET_WIDTH))


def _moba_kernel(q_ref, k_ref, v_ref, blk_onehot_ref, o_ref, qa_ref, ka_ref, va_ref):
    seq, d = q_ref.shape
    bs = MOBA_BLOCK
    bs_shift = bs.bit_length() - 1
    assert bs == 1 << bs_shift
    nb = seq // bs
    q = q_ref[...]
    k = k_ref[...]

    kmean = jnp.sum(k.astype(F32).reshape(nb, bs, d), axis=1) * (1.0 / bs)
    km_hi = kmean.astype(BF16)
    km_lo = (kmean - km_hi.astype(F32)).astype(BF16)
    score = (lax.dot_general(km_hi, q, NT_DIMS, preferred_element_type=F32)
             + lax.dot_general(km_lo, q, NT_DIMS, preferred_element_type=F32))
    blk = lax.broadcasted_iota(jnp.int32, (nb, seq), 0)
    own = lax.broadcasted_iota(jnp.int32, (nb, seq), 1) >> bs_shift
    past = blk < own
    score = jnp.where(past, score, -jnp.inf)
    rank = jnp.zeros((nb, seq), jnp.int32)
    for r in range(1, nb):
        other = pltpu.roll(score, r, axis=0)
        beats = (other > score) | ((other == score) & (blk >= r))
        rank = rank + beats.astype(jnp.int32)
    keep = (past & (rank < MOBA_TOPK)) | (blk == own)
    pen_t = jnp.where(keep, 0.0, MASK_PENALTY)
    pen_t = jnp.concatenate([pen_t, jnp.zeros((LANES - nb, seq), F32)], axis=0)
    pen = pen_t.T.astype(BF16)

    qa_ref[:, :d] = q
    qa_ref[:, d:] = pen
    ka_ref[:, :d] = k
    ka_ref[:, d:] = blk_onehot_ref[...]
    va_ref[:, :d] = v_ref[...]
    va_ref[:, d:] = jnp.ones((seq, LANES), BF16)

    qpos = lax.broadcasted_iota(jnp.int32, (bs, bs), 0)
    kpos = lax.broadcasted_iota(jnp.int32, (bs, bs), 1)

    def scores(qi):
        n_keys = (qi + 1) * bs
        return lax.dot_general(qa_ref[qi * bs:n_keys, :], ka_ref[0:n_keys, :], NT_DIMS,
                               preferred_element_type=F32)

    order = list(range(nb - 1, -1, -1))
    s_next = scores(order[0])
    for pos, qi in enumerate(order):
        q0 = qi * bs
        n_keys = q0 + bs
        s = s_next
        if pos + 1 < nb:
            s_next = scores(order[pos + 1])
        s_own = jnp.where(kpos <= qpos, s[:, q0:], -jnp.inf)
        s = s_own if qi == 0 else jnp.concatenate([s[:, :q0], s_own], axis=1)
        p = jnp.exp2((s - jnp.max(s, axis=-1, keepdims=True)).astype(BF16))
        acc = jnp.dot(p, va_ref[0:n_keys, :], preferred_element_type=F32)
        o_ref[q0:n_keys, :] = (acc[:, :d] / acc[:, d:]).astype(o_ref.dtype)


def _moba(mqk, plain, batch, seq):
    t = mqk.shape[0]
    d = MOBA_HEAD_DIM
    v_blk0 = 6144 // d
    blk_onehot = jnp.asarray(np.arange(seq)[:, None] // MOBA_BLOCK == np.arange(LANES)[None, :], BF16)
    return pl.pallas_call(
        _moba_kernel,
        grid=(batch, MOBA_HEADS),
        in_specs=[pl.BlockSpec((seq, d), lambda b, h: (b, h)),
                  pl.BlockSpec((seq, d), lambda b, h: (b, MOBA_HEADS + h)),
                  pl.BlockSpec((seq, d), lambda b, h: (b, v_blk0 + h)),
                  pl.BlockSpec((seq, LANES), lambda b, h: (0, 0))],
        out_specs=pl.BlockSpec((seq, d), lambda b, h: (b, h)),
        out_shape=jax.ShapeDtypeStruct((t, MOBA_WIDTH), BF16),
        scratch_shapes=[pltpu.VMEM((seq, d + LANES), BF16)] * 3,
        compiler_params=_params(("parallel", "parallel"), 48),
        name="moba",
    )(mqk, mqk, plain, blk_onehot)


def _merge_kernel(yr_ref, ym_ref, gr_ref, gm_ref, x_ref, wr_ref, wm_ref, wo_ref, nw_ref, x1_ref, h2_ref):
    ret = jnp.dot(yr_ref[...], wr_ref[...], preferred_element_type=F32)
    mob = jnp.dot(ym_ref[...], wm_ref[...], preferred_element_type=F32)
    merged = _sigmoid(gr_ref[...].astype(F32)) * ret + _sigmoid(gm_ref[...].astype(F32)) * mob
    x1 = x_ref[...] + jnp.dot(merged.astype(BF16), wo_ref[...], preferred_element_type=F32)
    x1_ref[...] = x1
    r = lax.rsqrt(jnp.mean(x1 * x1, axis=-1, keepdims=True) + EPS)
    h2_ref[...] = (x1 * r * nw_ref[...]).astype(h2_ref.dtype)


def _merge(y_ret, y_moba, plain, x, w_ret_up, w_moba_up, w_out, norm_w, tm=512):
    t, d = x.shape
    resident = lambda shape: pl.BlockSpec(shape, lambda i: (0, 0), pipeline_mode=pl.Buffered(1))
    return pl.pallas_call(
        _merge_kernel,
        grid=(t // tm,),
        in_specs=[pl.BlockSpec((tm, RET_WIDTH), lambda i: (i, 0)),
                  pl.BlockSpec((tm, MOBA_WIDTH), lambda i: (i, 0)),
                  pl.BlockSpec((tm, d), lambda i: (i, 0)),
                  pl.BlockSpec((tm, d), lambda i: (i, 1)),
                  pl.BlockSpec((tm, d), lambda i: (i, 0)),
                  resident(w_ret_up.shape), resident(w_moba_up.shape), resident(w_out.shape),
                  pl.BlockSpec((1, d), lambda i: (0, 0))],
        out_specs=[pl.BlockSpec((tm, d), lambda i: (i, 0)), pl.BlockSpec((tm, d), lambda i: (i, 0))],
        out_shape=[jax.ShapeDtypeStruct((t, d), F32), jax.ShapeDtypeStruct((t, d), BF16)],
        compiler_params=_params(("parallel",), 62),
        name="merge_outproj",
    )(y_ret, y_moba, plain, plain, x, w_ret_up, w_moba_up, w_out, norm_w.reshape(1, d))


def _ffn_up_kernel(h_ref, wa_ref, wb_ref, cwa_ref, cwb_ref, cba_ref, cbb_ref, o_ref, ua_ref, ub_ref,
                   wa16_ref, wb16_ref, *, tiles_per_seq):
    tm = h_ref.shape[0]
    pad = 8
    i = pl.program_id(1)

    @pl.when(i == 0)
    def _():
        wa16_ref[...] = wa_ref[...].astype(wa16_ref.dtype)
        wb16_ref[...] = wb_ref[...].astype(wb16_ref.dtype)

    @pl.when(i % tiles_per_seq == 0)
    def _():
        ua_ref[0:pad, :] = jnp.zeros((pad, ua_ref.shape[1]), F32)
        ub_ref[0:pad, :] = jnp.zeros((pad, ub_ref.shape[1]), F32)

    h = h_ref[...]
    ua_ref[pad:pad + tm, :] = jnp.dot(h, wa16_ref[...], preferred_element_type=F32)
    ub_ref[pad:pad + tm, :] = jnp.dot(h, wb16_ref[...], preferred_element_type=F32)

    def conv(u_ref, cw_ref, cb_ref):
        y = cb_ref[...]
        for j in range(CONV_WIDTH):
            off = pad - (CONV_WIDTH - 1) + j
            y = y + cw_ref[j:j + 1, :] * u_ref[off:off + tm, :]
        return y

    a = conv(ua_ref, cwa_ref, cba_ref)
    b = conv(ub_ref, cwb_ref, cbb_ref)
    o_ref[...] = (_silu(a) * b).astype(o_ref.dtype)
    ua_ref[0:pad, :] = ua_ref[tm:tm + pad, :]
    ub_ref[0:pad, :] = ub_ref[tm:tm + pad, :]


def _ffn_up(h2, w_up, conv_w, conv_b, seq, tm=1024, tn=512):
    t, d = h2.shape
    d_ff = w_up.shape[1] // 2
    nj = d_ff // tn
    conv_b = conv_b.reshape(1, 2 * d_ff)
    return pl.pallas_call(
        functools.partial(_ffn_up_kernel, tiles_per_seq=seq // tm),
        grid=(nj, t // tm),
        in_specs=[pl.BlockSpec((tm, d), lambda j, i: (i, 0)),
                  pl.BlockSpec((d, tn), lambda j, i: (0, j)),
                  pl.BlockSpec((d, tn), lambda j, i: (0, nj + j)),
                  pl.BlockSpec((CONV_WIDTH, tn), lambda j, i: (0, j)),
                  pl.BlockSpec((CONV_WIDTH, tn), lambda j, i: (0, nj + j)),
                  pl.BlockSpec((1, tn), lambda j, i: (0, j)),
                  pl.BlockSpec((1, tn), lambda j, i: (0, nj + j))],
        out_specs=pl.BlockSpec((tm, tn), lambda j, i: (i, j)),
        out_shape=jax.ShapeDtypeStruct((t, d_ff), BF16),
        scratch_shapes=[pltpu.VMEM((tm + 8, tn), F32), pltpu.VMEM((tm + 8, tn), F32),
                        pltpu.VMEM((d, tn), BF16), pltpu.VMEM((d, tn), BF16)],
        compiler_params=_params(("arbitrary", "arbitrary"), 52),
        name="ffn_up_conv_gate",
    )(h2, w_up, w_up, conv_w, conv_w, conv_b, conv_b)


def _ffn_down_kernel(a_ref, w_ref, x_ref, nw_ref, o_ref, *, final_norm):
    x2 = x_ref[...] + jnp.dot(a_ref[...], w_ref[...], preferred_element_type=F32)
    if final_norm:
        r = lax.rsqrt(jnp.mean(x2 * x2, axis=-1, keepdims=True) + EPS)
        x2 = x2 * r * nw_ref[...]
    o_ref[...] = x2


def _ffn_down(act, w_down, x1, norm_w, final_norm, tm=256):
    t, d = x1.shape
    d_ff = act.shape[1]
    return pl.pallas_call(
        functools.partial(_ffn_down_kernel, final_norm=final_norm),
        grid=(t // tm,),
        in_specs=[pl.BlockSpec((tm, d_ff), lambda i: (i, 0)),
                  pl.BlockSpec((d_ff, d), lambda i: (0, 0), pipeline_mode=pl.Buffered(1)),
                  pl.BlockSpec((tm, d), lambda i: (i, 0)),
                  pl.BlockSpec((1, d), lambda i: (0, 0))],
        out_specs=pl.BlockSpec((tm, d), lambda i: (i, 0)),
        out_shape=jax.ShapeDtypeStruct((t, d), F32),
        compiler_params=_params(("parallel",), 56),
        name="ffn_down",
    )(act, w_down, x1, norm_w.reshape(1, d))


def kernel(x, attn_norm_w, w_in, ret_norm_w, w_ret_up, w_moba_up, w_out, ffn_norm_w, w_ffn_up, conv_w,
           conv_b, w_ffn_down, final_norm_w):
    batch, seq, d = x.shape
    depth = w_in.shape[0]
    t = batch * seq
    xf = x.reshape(t, d)
    tm = 1024
    tiles_per_seq = seq // tm

    ret_cos, ret_sin = _ret_rope_tables(seq)
    mob_c, mob_sa, mob_sb = _moba_rope_tables(seq)
    ret_specs = [pl.BlockSpec((tm, LANES), lambda j, i: (i % tiles_per_seq, 0))] * 2
    mob_specs = [pl.BlockSpec((None, tm, LANES), lambda j, i: (j, i % tiles_per_seq, 0))] * 3

    tn = 1024
    assert RET_WIDTH == tn and MOBA_WIDTH == tn and d == 2 * tn
    plain_block = lambda j: jnp.where(j < 4, j + 7, jnp.where(j < 6, j - 2, 6))
    for l in range(depth):
        h = _rmsnorm(xf, attn_norm_w[l], BF16)
        ret_qk = _in_proj(h, w_in[l], lambda j: j, 2 * tn, _ret_rotary_epilogue, (ret_cos, ret_sin),
                          ret_specs, tm, tn, "in_proj_ret")
        mob_qk = _in_proj(h, w_in[l], lambda j: j + 4, 2 * tn, _moba_rotary_epilogue,
                          (mob_c, mob_sa, mob_sb), mob_specs, tm, tn, "in_proj_moba")
        plain = _in_proj(h, w_in[l], plain_block, 7 * tn, _plain_epilogue, (), [], tm, tn, "in_proj_plain")

        y_ret = _retention(ret_qk, plain, ret_norm_w[l], batch, seq)
        y_moba = _moba(mob_qk, plain, batch, seq)
        x1, h2 = _merge(y_ret, y_moba, plain, xf, w_ret_up[l].astype(BF16), w_moba_up[l].astype(BF16),
                        w_out[l].astype(BF16), ffn_norm_w[l])
        act = _ffn_up(h2, w_ffn_up[l], conv_w[l], conv_b[l], seq)
        last = l == depth - 1
        xf = _ffn_down(act, w_ffn_down[l].astype(BF16), x1, final_norm_w, final_norm=last)
    return xf.reshape(batch, seq, d)
```

```python
import functools
import math

import numpy as np
import jax
import jax.numpy as jnp
from jax import lax
from jax.experimental import pallas as pl
from jax.experimental.pallas import tpu as pltpu

RET_HEADS = 4
RET_HEAD_DIM = 256
RET_WIDTH = RET_HEADS * RET_HEAD_DIM
RET_ROPE_THETA = 10000.0
RET_TILE = 256
RET_STEP_ROWS = 1024
MOBA_HEADS = 8
MOBA_HEAD_DIM = 128
MOBA_WIDTH = MOBA_HEADS * MOBA_HEAD_DIM
MOBA_BLOCK = 256
MOBA_TOPK = 3
MOBA_HEADS_PER_STEP = 2
ROPE_THETA = 500000.0
ROT_DIM = MOBA_HEAD_DIM // 4
CONV_WIDTH = 3
EPS = 1e-6

LANES = 128
MASK_PENALTY = -1e30
BF16 = jnp.bfloat16
F32 = jnp.float32

NT_DIMS = (((1,), (1,)), ((), ()))
TN_DIMS = (((0,), (0,)), ((), ()))


def _params(semantics, vmem_mib):
    return pltpu.CompilerParams(dimension_semantics=semantics, vmem_limit_bytes=vmem_mib * 1024 * 1024)


def _sigmoid(x):
    return 0.5 * jnp.tanh(0.5 * x) + 0.5


def _silu(x):
    h = 0.5 * x
    return h + h * jnp.tanh(h)


def _rmsnorm_kernel(x_ref, w_ref, o_ref):
    x = x_ref[...]
    r = lax.rsqrt(jnp.mean(x * x, axis=-1, keepdims=True) + EPS)
    o_ref[...] = (x * r * w_ref[...]).astype(o_ref.dtype)


def _rmsnorm(x, w, out_dtype, tm=512):
    t, d = x.shape
    return pl.pallas_call(
        _rmsnorm_kernel,
        grid=(t // tm,),
        in_specs=[pl.BlockSpec((tm, d), lambda i: (i, 0)), pl.BlockSpec((1, d), lambda i: (0, 0))],
        out_specs=pl.BlockSpec((tm, d), lambda i: (i, 0)),
        out_shape=jax.ShapeDtypeStruct((t, d), out_dtype),
        compiler_params=_params(("parallel",), 32),
        name="rmsnorm",
    )(x, w.reshape(1, d))


def _plain_epilogue(acc, o_ref):
    o_ref[...] = acc.astype(o_ref.dtype)


def _ret_rotary_epilogue(acc, cos_ref, sin_ref, o_ref):
    cos = cos_ref[...]
    sin = sin_ref[...]
    half = RET_HEAD_DIM // 2
    for hh in range(acc.shape[1] // RET_HEAD_DIM):
        c0 = hh * RET_HEAD_DIM
        x1 = acc[:, c0:c0 + half]
        x2 = acc[:, c0 + half:c0 + RET_HEAD_DIM]
        o_ref[:, c0:c0 + half] = (x1 * cos - x2 * sin).astype(o_ref.dtype)
        o_ref[:, c0 + half:c0 + RET_HEAD_DIM] = (x2 * cos + x1 * sin).astype(o_ref.dtype)


def _moba_rotary_epilogue(acc, c_ref, sa_ref, sb_ref, o_ref):
    c = c_ref[...]
    sa = sa_ref[...]
    sb = sb_ref[...]
    half = ROT_DIM // 2
    for hh in range(acc.shape[1] // MOBA_HEAD_DIM):
        c0 = hh * MOBA_HEAD_DIM
        x = acc[:, c0:c0 + MOBA_HEAD_DIM]
        up = pltpu.roll(x, MOBA_HEAD_DIM - half, axis=1)
        dn = pltpu.roll(x, half, axis=1)
        o_ref[:, c0:c0 + MOBA_HEAD_DIM] = (x * c + up * sa + dn * sb).astype(o_ref.dtype)


def _in_proj_kernel(h_ref, w_ref, *rest, epilogue):
    wb_ref = rest[-1]

    @pl.when(pl.program_id(1) == 0)
    def _():
        wb_ref[...] = w_ref[...].astype(wb_ref.dtype)

    acc = jnp.dot(h_ref[...], wb_ref[...], preferred_element_type=F32)
    epilogue(acc, *rest[:-1])


def _in_proj(h, w, col_block, n_out, epilogue, tables, table_specs, tm, tn, name):
    t, d = h.shape
    return pl.pallas_call(
        functools.partial(_in_proj_kernel, epilogue=epilogue),
        grid=(n_out // tn, t // tm),
        in_specs=[pl.BlockSpec((tm, d), lambda j, i: (i, 0)),
                  pl.BlockSpec((d, tn), lambda j, i: (0, col_block(j)))] + table_specs,
        out_specs=pl.BlockSpec((tm, tn), lambda j, i: (i, j)),
        out_shape=jax.ShapeDtypeStruct((t, n_out), BF16),
        scratch_shapes=[pltpu.VMEM((d, tn), BF16)],
        compiler_params=_params(("arbitrary", "arbitrary"), 52),
        name=name,
    )(h, w, *tables)


def _rope_angles(seq, rot_dim, theta):
    pos = np.arange(seq, dtype=np.float64)
    inv = np.float64(theta) ** (-np.arange(0, rot_dim, 2, dtype=np.float64) / rot_dim)
    return pos[:, None] * inv[None, :]


def _ret_rope_tables(seq):
    ang = _rope_angles(seq, RET_HEAD_DIM, RET_ROPE_THETA)
    return jnp.asarray(np.cos(ang), F32), jnp.asarray(np.sin(ang), F32)


def _moba_rope_tables(seq):
    half = ROT_DIM // 2
    ang = _rope_angles(seq, ROT_DIM, ROPE_THETA)
    cos, sin = np.cos(ang), np.sin(ang)
    c = np.ones((seq, MOBA_HEAD_DIM))
    sa = np.zeros((seq, MOBA_HEAD_DIM))
    sb = np.zeros((seq, MOBA_HEAD_DIM))
    c[:, :half] = cos
    c[:, half:ROT_DIM] = cos
    sa[:, :half] = -sin
    sb[:, half:ROT_DIM] = sin
    q_scale = MOBA_HEAD_DIM ** -0.5 * math.log2(math.e)
    stack = lambda a: jnp.asarray(np.stack([a * q_scale, a]), F32)
    return stack(c), stack(sa), stack(sb)


def _retention_kernel(q_ref, k_ref, v_ref, g_ref, nw_ref, o_ref, state_ref, dec_ref, xi_ref, zeta_ref):
    c = pl.program_id(1)
    tile = RET_TILE
    k_scale = RET_HEAD_DIM ** -0.5
    log_gs = [float(np.log1p(-np.exp2(-5.0 - hh))) for hh in range(RET_HEADS)]

    @pl.when(c == 0)
    def _():
        state_ref[...] = jnp.zeros_like(state_ref)
        row = lax.broadcasted_iota(jnp.int32, (tile, tile), 0).astype(F32)
        col = lax.broadcasted_iota(jnp.int32, (tile, tile), 1).astype(F32)
        diff = row - col
        causal = diff >= 0
        row_d = lax.broadcasted_iota(jnp.int32, (tile, RET_HEAD_DIM), 0).astype(F32)
        for hh, log_g in enumerate(log_gs):
            dec_ref[hh] = jnp.where(causal, jnp.exp(log_g * jnp.where(causal, diff, 0.0)), 0.0) * k_scale
            xi_ref[hh] = jnp.exp(log_g * (row_d + 1.0))
            zeta_ref[hh] = jnp.exp(log_g * (tile - 1.0 - row_d)) * k_scale

    def recurrence(rows):
        outs = []
        for hh, log_g in enumerate(log_gs):
            g_chunk = float(np.exp(log_g * tile))
            sl = slice(hh * RET_HEAD_DIM, (hh + 1) * RET_HEAD_DIM)
            q = q_ref[rows, sl]
            k = k_ref[rows, sl]
            v = v_ref[rows, sl]
            state = state_ref[hh]
            scores = lax.dot_general(q, k, NT_DIMS, preferred_element_type=F32) * dec_ref[hh]
            inner = jnp.dot(scores.astype(BF16), v, preferred_element_type=F32)
            cross = jnp.dot(q, state.astype(BF16), preferred_element_type=F32) * xi_ref[hh]
            kz = (k.astype(F32) * zeta_ref[hh]).astype(BF16)
            state_ref[hh] = state * g_chunk + lax.dot_general(kz, v, TN_DIMS, preferred_element_type=F32)
            outs.append(inner + cross)
        return outs

    def epilogue(rows, outs):
        for hh, o in enumerate(outs):
            sl = slice(hh * RET_HEAD_DIM, (hh + 1) * RET_HEAD_DIM)
            o = o * lax.rsqrt(jnp.mean(o * o, axis=-1, keepdims=True) + EPS)
            o_ref[rows, sl] = (_silu(g_ref[rows, sl].astype(F32)) * (o * nw_ref[:, sl])).astype(o_ref.dtype)

    pending = None
    for sub in range(q_ref.shape[0] // tile):
        rows = slice(sub * tile, (sub + 1) * tile)
        outs = recurrence(rows)
        if pending is not None:
            epilogue(*pending)
        pending = (rows, outs)
    epilogue(*pending)


def _retention(qk, plain, norm_w, batch, seq):
    t = qk.shape[0]
    blk = RET_STEP_ROWS
    n_blk = seq // blk
    row = lambda b, c: b * n_blk + c
    hd = RET_HEAD_DIM
    return pl.pallas_call(
        _retention_kernel,
        grid=(batch, n_blk),
        in_specs=[pl.BlockSpec((blk, RET_WIDTH), lambda b, c: (row(b, c), 0)),
                  pl.BlockSpec((blk, RET_WIDTH), lambda b, c: (row(b, c), 1)),
                  pl.BlockSpec((blk, RET_WIDTH), lambda b, c: (row(b, c), 4)),
                  pl.BlockSpec((blk, RET_WIDTH), lambda b, c: (row(b, c), 5)),
                  pl.BlockSpec((1, RET_WIDTH), lambda b, c: (0, 0))],
        out_specs=pl.BlockSpec((blk, RET_WIDTH), lambda b, c: (row(b, c), 0)),
        out_shape=jax.ShapeDtypeStruct((t, RET_WIDTH), BF16),
        scratch_shapes=[pltpu.VMEM((RET_HEADS, hd, hd), F32),
                        pltpu.VMEM((RET_HEADS, RET_TILE, RET_TILE), F32),
                        pltpu.VMEM((RET_HEADS, RET_TILE, hd), F32),
                        pltpu.VMEM((RET_HEADS, RET_TILE, hd), F32)],
        compiler_params=_params(("parallel", "arbitrary"), 40),
        name="retention",
    )(qk, qk, plain, plain, norm_w.reshape(1, RET_WIDTH))


def _moba_kernel(q_ref, k_ref, v_ref, blk_onehot_ref, o_ref, *scratch):
    seq = q_ref.shape[0]
    d = MOBA_HEAD_DIM
    bs = MOBA_BLOCK
    bs_shift = bs.bit_length() - 1
    assert bs == 1 << bs_shift
    nb = seq // bs
    blk = lax.broadcasted_iota(jnp.int32, (nb, seq), 0)
    own = lax.broadcasted_iota(jnp.int32, (nb, seq), 1) >> bs_shift
    past = blk < own
    qpos = lax.broadcasted_iota(jnp.int32, (bs, bs), 0)
    kpos = lax.broadcasted_iota(jnp.int32, (bs, bs), 1)

    def prepare(hd):
        qa_ref, ka_ref, va_ref = scratch[3 * hd:3 * hd + 3]
        cols = slice(hd * d, (hd + 1) * d)
        q = q_ref[:, cols]
        k = k_ref[:, cols]
        kmean = jnp.sum(k.astype(F32).reshape(nb, bs, d), axis=1) * (1.0 / bs)
        km_hi = kmean.astype(BF16)
        km_lo = (kmean - km_hi.astype(F32)).astype(BF16)
        score = (lax.dot_general(km_hi, q, NT_DIMS, preferred_element_type=F32)
                 + lax.dot_general(km_lo, q, NT_DIMS, preferred_element_type=F32))
        score = jnp.where(past, score, -jnp.inf)
        rank = jnp.zeros((nb, seq), jnp.int32)
        for r in range(1, nb):
            other = pltpu.roll(score, r, axis=0)
            beats = (other > score) | ((other == score) & (blk >= r))
            rank = rank + beats.astype(jnp.int32)
        keep = (past & (rank < MOBA_TOPK)) | (blk == own)
        pen_t = jnp.where(keep, 0.0, MASK_PENALTY)
        pen_t = jnp.concatenate([pen_t, jnp.zeros((LANES - nb, seq), F32)], axis=0)
        qa_ref[:, :d] = q
        qa_ref[:, d:] = pen_t.T.astype(BF16)
        ka_ref[:, :d] = k
        ka_ref[:, d:] = blk_onehot_ref[...]
        va_ref[:, :d] = v_ref[:, cols]
        va_ref[:, d:] = jnp.ones((seq, LANES), BF16)

    def attend(hd):
        qa_ref, ka_ref, va_ref = scratch[3 * hd:3 * hd + 3]

        def scores(qi):
            n_keys = (qi + 1) * bs
            return lax.dot_general(qa_ref[qi * bs:n_keys, :], ka_ref[0:n_keys, :], NT_DIMS,
                                   preferred_element_type=F32)

        order = list(range(nb - 1, -1, -1))
        s_next = scores(order[0])
        for pos, qi in enumerate(order):
            q0 = qi * bs
            n_keys = q0 + bs
            s = s_next
            if pos + 1 < nb:
                s_next = scores(order[pos + 1])
            s_own = jnp.where(kpos <= qpos, s[:, q0:], -jnp.inf)
            s = s_own if qi == 0 else jnp.concatenate([s[:, :q0], s_own], axis=1)
            p = jnp.exp2((s - jnp.max(s, axis=-1, keepdims=True)).astype(BF16))
            acc = jnp.dot(p, va_ref[0:n_keys, :], preferred_element_type=F32)
            o_ref[q0:n_keys, hd * d:(hd + 1) * d] = (acc[:, :d] / acc[:, d:]).astype(o_ref.dtype)
            yield

    n_heads = len(scratch) // 3
    prepare(0)
    for hd in range(n_heads):
        tiles = attend(hd)
        for pos in range(nb):
            next(tiles)
            if pos == 1 and hd + 1 < n_heads:
                prepare(hd + 1)


def _moba(mqk, plain, batch, seq):
    t = mqk.shape[0]
    d = MOBA_HEAD_DIM
    hp = MOBA_HEADS_PER_STEP
    w = hp * d
    n_groups = MOBA_HEADS // hp
    v_blk0 = 6144 // w
    blk_onehot = jnp.asarray(np.arange(seq)[:, None] // MOBA_BLOCK == np.arange(LANES)[None, :], BF16)
    return pl.pallas_call(
        _moba_kernel,
        grid=(batch, n_groups),
        in_specs=[pl.BlockSpec((seq, w), lambda b, g: (b, g)),
                  pl.BlockSpec((seq, w), lambda b, g: (b, n_groups + g)),
                  pl.BlockSpec((seq, w), lambda b, g: (b, v_blk0 + g)),
                  pl.BlockSpec((seq, LANES), lambda b, g: (0, 0))],
        out_specs=pl.BlockSpec((seq, w), lambda b, g: (b, g)),
        out_shape=jax.ShapeDtypeStruct((t, MOBA_WIDTH), BF16),
        scratch_shapes=[pltpu.VMEM((seq, d + LANES), BF16)] * (3 * hp),
        compiler_params=_params(("parallel", "parallel"), 48),
        name="moba",
    )(mqk, mqk, plain, blk_onehot)


def _merge_kernel(yr_ref, ym_ref, gr_ref, gm_ref, x_ref, wr_ref, wm_ref, wo_ref, nw_ref, x1_ref, h2_ref):
    ret = jnp.dot(yr_ref[...], wr_ref[...], preferred_element_type=F32)
    mob = jnp.dot(ym_ref[...], wm_ref[...], preferred_element_type=F32)
    merged = _sigmoid(gr_ref[...].astype(F32)) * ret + _sigmoid(gm_ref[...].astype(F32)) * mob
    x1 = x_ref[...] + jnp.dot(merged.astype(BF16), wo_ref[...], preferred_element_type=F32)
    x1_ref[...] = x1
    r = lax.rsqrt(jnp.mean(x1 * x1, axis=-1, keepdims=True) + EPS)
    h2_ref[...] = (x1 * r * nw_ref[...]).astype(h2_ref.dtype)


def _merge(y_ret, y_moba, plain, x, w_ret_up, w_moba_up, w_out, norm_w, tm=512):
    t, d = x.shape
    resident = lambda shape: pl.BlockSpec(shape, lambda i: (0, 0), pipeline_mode=pl.Buffered(1))
    return pl.pallas_call(
        _merge_kernel,
        grid=(t // tm,),
        in_specs=[pl.BlockSpec((tm, RET_WIDTH), lambda i: (i, 0)),
                  pl.BlockSpec((tm, MOBA_WIDTH), lambda i: (i, 0)),
                  pl.BlockSpec((tm, d), lambda i: (i, 0)),
                  pl.BlockSpec((tm, d), lambda i: (i, 1)),
                  pl.BlockSpec((tm, d), lambda i: (i, 0)),
                  resident(w_ret_up.shape), resident(w_moba_up.shape), resident(w_out.shape),
                  pl.BlockSpec((1, d), lambda i: (0, 0))],
        out_specs=[pl.BlockSpec((tm, d), lambda i: (i, 0)), pl.BlockSpec((tm, d), lambda i: (i, 0))],
        out_shape=[jax.ShapeDtypeStruct((t, d), F32), jax.ShapeDtypeStruct((t, d), BF16)],
        compiler_params=_params(("parallel",), 62),
        name="merge_outproj",
    )(y_ret, y_moba, plain, plain, x, w_ret_up, w_moba_up, w_out, norm_w.reshape(1, d))


def _ffn_up_kernel(h_ref, wa_ref, wb_ref, cwa_ref, cwb_ref, cba_ref, cbb_ref, o_ref, ua_ref, ub_ref,
                   wa16_ref, wb16_ref, *, tiles_per_seq):
    tm = h_ref.shape[0]
    pad = 8
    i = pl.program_id(1)

    @pl.when(i == 0)
    def _():
        wa16_ref[...] = wa_ref[...].astype(wa16_ref.dtype)
        wb16_ref[...] = wb_ref[...].astype(wb16_ref.dtype)

    @pl.when(i % tiles_per_seq == 0)
    def _():
        ua_ref[0:pad, :] = jnp.zeros((pad, ua_ref.shape[1]), F32)
        ub_ref[0:pad, :] = jnp.zeros((pad, ub_ref.shape[1]), F32)

    h = h_ref[...]
    ua_ref[pad:pad + tm, :] = jnp.dot(h, wa16_ref[...], preferred_element_type=F32)
    ub_ref[pad:pad + tm, :] = jnp.dot(h, wb16_ref[...], preferred_element_type=F32)

    def conv(u_ref, cw_ref, cb_ref):
        y = cb_ref[...]
        for j in range(CONV_WIDTH):
            off = pad - (CONV_WIDTH - 1) + j
            y = y + cw_ref[j:j + 1, :] * u_ref[off:off + tm, :]
        return y

    a = conv(ua_ref, cwa_ref, cba_ref)
    b = conv(ub_ref, cwb_ref, cbb_ref)
    o_ref[...] = (_silu(a) * b).astype(o_ref.dtype)
    ua_ref[0:pad, :] = ua_ref[tm:tm + pad, :]
    ub_ref[0:pad, :] = ub_ref[tm:tm + pad, :]


def _ffn_up(h2, w_up, conv_w, conv_b, seq, tm=1024, tn=512):
    t, d = h2.shape
    d_ff = w_up.shape[1] // 2
    nj = d_ff // tn
    conv_b = conv_b.reshape(1, 2 * d_ff)
    return pl.pallas_call(
        functools.partial(_ffn_up_kernel, tiles_per_seq=seq // tm),
        grid=(nj, t // tm),
        in_specs=[pl.BlockSpec((tm, d), lambda j, i: (i, 0)),
                  pl.BlockSpec((d, tn), lambda j, i: (0, j)),
                  pl.BlockSpec((d, tn), lambda j, i: (0, nj + j)),
                  pl.BlockSpec((CONV_WIDTH, tn), lambda j, i: (0, j)),
                  pl.BlockSpec((CONV_WIDTH, tn), lambda j, i: (0, nj + j)),
                  pl.BlockSpec((1, tn), lambda j, i: (0, j)),
                  pl.BlockSpec((1, tn), lambda j, i: (0, nj + j))],
        out_specs=pl.BlockSpec((tm, tn), lambda j, i: (i, j)),
        out_shape=jax.ShapeDtypeStruct((t, d_ff), BF16),
        scratch_shapes=[pltpu.VMEM((tm + 8, tn), F32), pltpu.VMEM((tm + 8, tn), F32),
                        pltpu.VMEM((d, tn), BF16), pltpu.VMEM((d, tn), BF16)],
        compiler_params=_params(("arbitrary", "arbitrary"), 52),
        name="ffn_up_conv_gate",
    )(h2, w_up, w_up, conv_w, conv_w, conv_b, conv_b)


def _ffn_down_kernel(a_ref, w_ref, x_ref, nw_ref, o_ref, *, final_norm):
    x2 = x_ref[...] + jnp.dot(a_ref[...], w_ref[...], preferred_element_type=F32)
    if final_norm:
        r = lax.rsqrt(jnp.mean(x2 * x2, axis=-1, keepdims=True) + EPS)
        x2 = x2 * r * nw_ref[...]
    o_ref[...] = x2


def _ffn_down(act, w_down, x1, norm_w, final_norm, tm=256):
    t, d = x1.shape
    d_ff = act.shape[1]
    return pl.pallas_call(
        functools.partial(_ffn_down_kernel, final_norm=final_norm),
        grid=(t // tm,),
        in_specs=[pl.BlockSpec((tm, d_ff), lambda i: (i, 0)),
                  pl.BlockSpec((d_ff, d), lambda i: (0, 0), pipeline_mode=pl.Buffered(1)),
                  pl.BlockSpec((tm, d), lambda i: (i, 0)),
                  pl.BlockSpec((1, d), lambda i: (0, 0))],
        out_specs=pl.BlockSpec((tm, d), lambda i: (i, 0)),
        out_shape=jax.ShapeDtypeStruct((t, d), F32),
        compiler_params=_params(("parallel",), 56),
        name="ffn_down",
    )(act, w_down, x1, norm_w.reshape(1, d))


def kernel(x, attn_norm_w, w_in, ret_norm_w, w_ret_up, w_moba_up, w_out, ffn_norm_w, w_ffn_up, conv_w,
           conv_b, w_ffn_down, final_norm_w):
    batch, seq, d = x.shape
    depth = w_in.shape[0]
    t = batch * seq
    xf = x.reshape(t, d)
    tm = 1024
    tiles_per_seq = seq // tm

    ret_cos, ret_sin = _ret_rope_tables(seq)
    mob_c, mob_sa, mob_sb = _moba_rope_tables(seq)
    ret_specs = [pl.BlockSpec((tm, LANES), lambda j, i: (i % tiles_per_seq, 0))] * 2
    mob_specs = [pl.BlockSpec((None, tm, LANES), lambda j, i: (j, i % tiles_per_seq, 0))] * 3

    tn = 1024
    assert RET_WIDTH == tn and MOBA_WIDTH == tn and d == 2 * tn
    plain_block = lambda j: jnp.where(j < 4, j + 7, jnp.where(j < 6, j - 2, 6))
    for l in range(depth):
        h = _rmsnorm(xf, attn_norm_w[l], BF16)
        ret_qk = _in_proj(h, w_in[l], lambda j: j, 2 * tn, _ret_rotary_epilogue, (ret_cos, ret_sin),
                          ret_specs, tm, tn, "in_proj_ret")
        mob_qk = _in_proj(h, w_in[l], lambda j: j + 4, 2 * tn, _moba_rotary_epilogue,
                          (mob_c, mob_sa, mob_sb), mob_specs, tm, tn, "in_proj_moba")
        plain = _in_proj(h, w_in[l], plain_block, 7 * tn, _plain_epilogue, (), [], tm, tn, "in_proj_plain")

        y_ret = _retention(ret_qk, plain, ret_norm_w[l], batch, seq)
        y_moba = _moba(mob_qk, plain, batch, seq)
        x1, h2 = _merge(y_ret, y_moba, plain, xf, w_ret_up[l].astype(BF16), w_moba_up[l].astype(BF16),
                        w_out[l].astype(BF16), ffn_norm_w[l])
        act = _ffn_up(h2, w_ffn_up[l], conv_w[l], conv_b[l], seq)
        last = l == depth - 1
        xf = _ffn_down(act, w_ffn_down[l].astype(BF16), x1, final_norm_w, final_norm=last)
    return xf.reshape(batch, seq, d)
```

```python
import functools
import math

import numpy as np
import jax
import jax.numpy as jnp
from jax import lax
from jax.experimental import pallas as pl
from jax.experimental.pallas import tpu as pltpu

RET_HEADS = 4
RET_HEAD_DIM = 256
RET_WIDTH = RET_HEADS * RET_HEAD_DIM
RET_ROPE_THETA = 10000.0
RET_TILE = 256
RET_STEP_ROWS = 1024
MOBA_HEADS = 8
MOBA_HEAD_DIM = 128
MOBA_WIDTH = MOBA_HEADS * MOBA_HEAD_DIM
MOBA_BLOCK = 256
MOBA_TOPK = 3
MOBA_HEADS_PER_STEP = 2
ROPE_THETA = 500000.0
ROT_DIM = MOBA_HEAD_DIM // 4
CONV_WIDTH = 3
EPS = 1e-6

LANES = 128
MASK_PENALTY = -1e30
BF16 = jnp.bfloat16
F32 = jnp.float32

NT_DIMS = (((1,), (1,)), ((), ()))
TN_DIMS = (((0,), (0,)), ((), ()))


def _params(semantics, vmem_mib):
    return pltpu.CompilerParams(dimension_semantics=semantics, vmem_limit_bytes=vmem_mib * 1024 * 1024)


def _sigmoid(x):
    return 0.5 * jnp.tanh(0.5 * x) + 0.5


def _silu(x):
    h = 0.5 * x
    return h + h * jnp.tanh(h)


def _rmsnorm_kernel(x_ref, w_ref, o_ref):
    x = x_ref[...]
    r = lax.rsqrt(jnp.mean(x * x, axis=-1, keepdims=True) + EPS)
    o_ref[...] = (x * r * w_ref[...]).astype(o_ref.dtype)


def _rmsnorm(x, w, out_dtype, tm=512):
    t, d = x.shape
    return pl.pallas_call(
        _rmsnorm_kernel,
        grid=(t // tm,),
        in_specs=[pl.BlockSpec((tm, d), lambda i: (i, 0)), pl.BlockSpec((1, d), lambda i: (0, 0))],
        out_specs=pl.BlockSpec((tm, d), lambda i: (i, 0)),
        out_shape=jax.ShapeDtypeStruct((t, d), out_dtype),
        compiler_params=_params(("parallel",), 32),
        name="rmsnorm",
    )(x, w.reshape(1, d))


def _plain_epilogue(acc, o_ref):
    o_ref[...] = acc.astype(o_ref.dtype)


def _ret_rotary_epilogue(acc, cos_ref, sin_ref, o_ref):
    cos = cos_ref[...]
    sin = sin_ref[...]
    half = RET_HEAD_DIM // 2
    for hh in range(acc.shape[1] // RET_HEAD_DIM):
        c0 = hh * RET_HEAD_DIM
        x1 = acc[:, c0:c0 + half]
        x2 = acc[:, c0 + half:c0 + RET_HEAD_DIM]
        o_ref[:, c0:c0 + half] = (x1 * cos - x2 * sin).astype(o_ref.dtype)
        o_ref[:, c0 + half:c0 + RET_HEAD_DIM] = (x2 * cos + x1 * sin).astype(o_ref.dtype)


def _moba_rotary_epilogue(acc, c_ref, s_ref, o_ref):
    c = c_ref[...]
    s = s_ref[...]
    half = ROT_DIM // 2
    lane = lax.broadcasted_iota(jnp.int32, (acc.shape[0], MOBA_HEAD_DIM), 1)
    idx = jnp.where(lane < ROT_DIM, lane ^ half, lane)
    for hh in range(acc.shape[1] // MOBA_HEAD_DIM):
        c0 = hh * MOBA_HEAD_DIM
        x = acc[:, c0:c0 + MOBA_HEAD_DIM]
        partner = jnp.take_along_axis(x, idx, axis=1)
        o_ref[:, c0:c0 + MOBA_HEAD_DIM] = (x * c + partner * s).astype(o_ref.dtype)


def _in_proj_kernel(h_ref, w_ref, *rest, epilogue):
    wb_ref = rest[-1]

    @pl.when(pl.program_id(1) == 0)
    def _():
        wb_ref[...] = w_ref[...].astype(wb_ref.dtype)

    acc = jnp.dot(h_ref[...], wb_ref[...], preferred_element_type=F32)
    epilogue(acc, *rest[:-1])


def _in_proj(h, w, col_block, n_out, epilogue, tables, table_specs, tm, tn, name):
    t, d = h.shape
    return pl.pallas_call(
        functools.partial(_in_proj_kernel, epilogue=epilogue),
        grid=(n_out // tn, t // tm),
        in_specs=[pl.BlockSpec((tm, d), lambda j, i: (i, 0)),
                  pl.BlockSpec((d, tn), lambda j, i: (0, col_block(j)))] + table_specs,
        out_specs=pl.BlockSpec((tm, tn), lambda j, i: (i, j)),
        out_shape=jax.ShapeDtypeStruct((t, n_out), BF16),
        scratch_shapes=[pltpu.VMEM((d, tn), BF16)],
        compiler_params=_params(("arbitrary", "arbitrary"), 52),
        name=name,
    )(h, w, *tables)


def _rope_angles(seq, rot_dim, theta):
    pos = np.arange(seq, dtype=np.float64)
    inv = np.float64(theta) ** (-np.arange(0, rot_dim, 2, dtype=np.float64) / rot_dim)
    return pos[:, None] * inv[None, :]


def _ret_rope_tables(seq):
    ang = _rope_angles(seq, RET_HEAD_DIM, RET_ROPE_THETA)
    return jnp.asarray(np.cos(ang), F32), jnp.asarray(np.sin(ang), F32)


def _moba_rope_tables(seq):
    half = ROT_DIM // 2
    ang = _rope_angles(seq, ROT_DIM, ROPE_THETA)
    cos, sin = np.cos(ang), np.sin(ang)
    c = np.ones((seq, MOBA_HEAD_DIM))
    s = np.zeros((seq, MOBA_HEAD_DIM))
    c[:, :half] = cos
    c[:, half:ROT_DIM] = cos
    s[:, :half] = -sin
    s[:, half:ROT_DIM] = sin
    q_scale = MOBA_HEAD_DIM ** -0.5 * math.log2(math.e)
    stack = lambda a: jnp.asarray(np.stack([a * q_scale, a]), F32)
    return stack(c), stack(s)


def _retention_kernel(q_ref, k_ref, v_ref, g_ref, nw_ref, o_ref, state_ref, dec_ref, xi_ref, zeta_ref):
    c = pl.program_id(1)
    tile = RET_TILE
    k_scale = RET_HEAD_DIM ** -0.5
    log_gs = [float(np.log1p(-np.exp2(-5.0 - hh))) for hh in range(RET_HEADS)]

    @pl.when(c == 0)
    def _():
        state_ref[...] = jnp.zeros_like(state_ref)
        row = lax.broadcasted_iota(jnp.int32, (tile, tile), 0).astype(F32)
        col = lax.broadcasted_iota(jnp.int32, (tile, tile), 1).astype(F32)
        diff = row - col
        causal = diff >= 0
        row_d = lax.broadcasted_iota(jnp.int32, (tile, RET_HEAD_DIM), 0).astype(F32)
        for hh, log_g in enumerate(log_gs):
            dec_ref[hh] = jnp.where(causal, jnp.exp(log_g * jnp.where(causal, diff, 0.0)), 0.0) * k_scale
            xi_ref[hh] = jnp.exp(log_g * (row_d + 1.0))
            zeta_ref[hh] = jnp.exp(log_g * (tile - 1.0 - row_d)) * k_scale

    def recurrence(rows):
        outs = []
        for hh, log_g in enumerate(log_gs):
            g_chunk = float(np.exp(log_g * tile))
            sl = slice(hh * RET_HEAD_DIM, (hh + 1) * RET_HEAD_DIM)
            q = q_ref[rows, sl]
            k = k_ref[rows, sl]
            v = v_ref[rows, sl]
            state = state_ref[hh]
            scores = lax.dot_general(q, k, NT_DIMS, preferred_element_type=F32) * dec_ref[hh]
            inner = jnp.dot(scores.astype(BF16), v, preferred_element_type=F32)
            cross = jnp.dot(q, state.astype(BF16), preferred_element_type=F32) * xi_ref[hh]
            kz = (k.astype(F32) * zeta_ref[hh]).astype(BF16)
            state_ref[hh] = state * g_chunk + lax.dot_general(kz, v, TN_DIMS, preferred_element_type=F32)
            outs.append(inner + cross)
        return outs

    def epilogue(rows, outs):
        for hh, o in enumerate(outs):
            sl = slice(hh * RET_HEAD_DIM, (hh + 1) * RET_HEAD_DIM)
            o = o * lax.rsqrt(jnp.mean(o * o, axis=-1, keepdims=True) + EPS)
            o_ref[rows, sl] = (_silu(g_ref[rows, sl].astype(F32)) * (o * nw_ref[:, sl])).astype(o_ref.dtype)

    pending = None
    for sub in range(q_ref.shape[0] // tile):
        rows = slice(sub * tile, (sub + 1) * tile)
        outs = recurrence(rows)
        if pending is not None:
            epilogue(*pending)
        pending = (rows, outs)
    epilogue(*pending)


def _retention(qk, plain, norm_w, batch, seq):
    t = qk.shape[0]
    blk = RET_STEP_ROWS
    n_blk = seq // blk
    row = lambda b, c: b * n_blk + c
    hd = RET_HEAD_DIM
    return pl.pallas_call(
        _retention_kernel,
        grid=(batch, n_blk),
        in_specs=[pl.BlockSpec((blk, RET_WIDTH), lambda b, c: (row(b, c), 0)),
                  pl.BlockSpec((blk, RET_WIDTH), lambda b, c: (row(b, c), 1)),
                  pl.BlockSpec((blk, RET_WIDTH), lambda b, c: (row(b, c), 4)),
                  pl.BlockSpec((blk, RET_WIDTH), lambda b, c: (row(b, c), 5)),
                  pl.BlockSpec((1, RET_WIDTH), lambda b, c: (0, 0))],
        out_specs=pl.BlockSpec((blk, RET_WIDTH), lambda b, c: (row(b, c), 0)),
        out_shape=jax.ShapeDtypeStruct((t, RET_WIDTH), BF16),
        scratch_shapes=[pltpu.VMEM((RET_HEADS, hd, hd), F32),
                        pltpu.VMEM((RET_HEADS, RET_TILE, RET_TILE), F32),
                        pltpu.VMEM((RET_HEADS, RET_TILE, hd), F32),
                        pltpu.VMEM((RET_HEADS, RET_TILE, hd), F32)],
        compiler_params=_params(("parallel", "arbitrary"), 40),
        name="retention",
    )(qk, qk, plain, plain, norm_w.reshape(1, RET_WIDTH))


def _moba_kernel(q_ref, k_ref, v_ref, blk_onehot_ref, o_ref, *scratch):
    seq = q_ref.shape[0]
    d = MOBA_HEAD_DIM
    bs = MOBA_BLOCK
    bs_shift = bs.bit_length() - 1
    assert bs == 1 << bs_shift
    nb = seq // bs
    blk = lax.broadcasted_iota(jnp.int32, (nb, seq), 0)
    own = lax.broadcasted_iota(jnp.int32, (nb, seq), 1) >> bs_shift
    past = blk < own
    qpos = lax.broadcasted_iota(jnp.int32, (bs, bs), 0)
    kpos = lax.broadcasted_iota(jnp.int32, (bs, bs), 1)

    def prepare(hd):
        qa_ref, ka_ref, va_ref = scratch[3 * hd:3 * hd + 3]
        cols = slice(hd * d, (hd + 1) * d)
        q = q_ref[:, cols]
        k = k_ref[:, cols]
        kmean = jnp.sum(k.astype(F32).reshape(nb, bs, d), axis=1) * (1.0 / bs)
        km_hi = kmean.astype(BF16)
        km_lo = (kmean - km_hi.astype(F32)).astype(BF16)
        score = (lax.dot_general(km_hi, q, NT_DIMS, preferred_element_type=F32)
                 + lax.dot_general(km_lo, q, NT_DIMS, preferred_element_type=F32))
        score = jnp.where(past, score, -jnp.inf)
        rank = jnp.zeros((nb, seq), jnp.int32)
        for r in range(1, nb):
            other = pltpu.roll(score, r, axis=0)
            beats = (other > score) | ((other == score) & (blk >= r))
            rank = rank + beats.astype(jnp.int32)
        keep = (past & (rank < MOBA_TOPK)) | (blk == own)
        pen_t = jnp.where(keep, 0.0, MASK_PENALTY)
        pen_t = jnp.concatenate([pen_t, jnp.zeros((LANES - nb, seq), F32)], axis=0)
        qa_ref[:, :d] = q
        qa_ref[:, d:] = pen_t.T.astype(BF16)
        ka_ref[:, :d] = k
        ka_ref[:, d:] = blk_onehot_ref[...]
        va_ref[:, :d] = v_ref[:, cols]
        va_ref[:, d:] = jnp.ones((seq, LANES), BF16)

    def attend(hd):
        qa_ref, ka_ref, va_ref = scratch[3 * hd:3 * hd + 3]

        def scores(qi):
            n_keys = (qi + 1) * bs
            return lax.dot_general(qa_ref[qi * bs:n_keys, :], ka_ref[0:n_keys, :], NT_DIMS,
                                   preferred_element_type=F32)

        order = list(range(nb - 1, -1, -1))
        s_next = scores(order[0])
        for pos, qi in enumerate(order):
            q0 = qi * bs
            n_keys = q0 + bs
            s = s_next
            if pos + 1 < nb:
                s_next = scores(order[pos + 1])
            s_own = jnp.where(kpos <= qpos, s[:, q0:], -jnp.inf)
            s = s_own if qi == 0 else jnp.concatenate([s[:, :q0], s_own], axis=1)
            p = jnp.exp2((s - jnp.max(s, axis=-1, keepdims=True)).astype(BF16))
            acc = jnp.dot(p, va_ref[0:n_keys, :], preferred_element_type=F32)
            o_ref[q0:n_keys, hd * d:(hd + 1) * d] = (acc[:, :d] / acc[:, d:]).astype(o_ref.dtype)
            yield

    n_heads = len(scratch) // 3
    prepare(0)
    for hd in range(n_heads):
        tiles = attend(hd)
        for pos in range(nb):
            next(tiles)
            if pos == 1 and hd + 1 < n_heads:
                prepare(hd + 1)


def _moba(mqk, plain, batch, seq):
    t = mqk.shape[0]
    d = MOBA_HEAD_DIM
    hp = MOBA_HEADS_PER_STEP
    w = hp * d
    n_groups = MOBA_HEADS // hp
    v_blk0 = 6144 // w
    blk_onehot = jnp.asarray(np.arange(seq)[:, None] // MOBA_BLOCK == np.arange(LANES)[None, :], BF16)
    return pl.pallas_call(
        _moba_kernel,
        grid=(batch, n_groups),
        in_specs=[pl.BlockSpec((seq, w), lambda b, g: (b, g)),
                  pl.BlockSpec((seq, w), lambda b, g: (b, n_groups + g)),
                  pl.BlockSpec((seq, w), lambda b, g: (b, v_blk0 + g)),
                  pl.BlockSpec((seq, LANES), lambda b, g: (0, 0))],
        out_specs=pl.BlockSpec((seq, w), lambda b, g: (b, g)),
        out_shape=jax.ShapeDtypeStruct((t, MOBA_WIDTH), BF16),
        scratch_shapes=[pltpu.VMEM((seq, d + LANES), BF16)] * (3 * hp),
        compiler_params=_params(("parallel", "parallel"), 48),
        name="moba",
    )(mqk, mqk, plain, blk_onehot)


def _merge_kernel(yr_ref, ym_ref, gr_ref, gm_ref, x_ref, wr_ref, wm_ref, wo_ref, nw_ref, x1_ref, h2_ref):
    ret = jnp.dot(yr_ref[...], wr_ref[...], preferred_element_type=F32)
    mob = jnp.dot(ym_ref[...], wm_ref[...], preferred_element_type=F32)
    merged = _sigmoid(gr_ref[...].astype(F32)) * ret + _sigmoid(gm_ref[...].astype(F32)) * mob
    x1 = x_ref[...] + jnp.dot(merged.astype(BF16), wo_ref[...], preferred_element_type=F32)
    x1_ref[...] = x1
    r = lax.rsqrt(jnp.mean(x1 * x1, axis=-1, keepdims=True) + EPS)
    h2_ref[...] = (x1 * r * nw_ref[...]).astype(h2_ref.dtype)


def _merge(y_ret, y_moba, plain, x, w_ret_up, w_moba_up, w_out, norm_w, tm=512):
    t, d = x.shape
    resident = lambda shape: pl.BlockSpec(shape, lambda i: (0, 0), pipeline_mode=pl.Buffered(1))
    return pl.pallas_call(
        _merge_kernel,
        grid=(t // tm,),
        in_specs=[pl.BlockSpec((tm, RET_WIDTH), lambda i: (i, 0)),
                  pl.BlockSpec((tm, MOBA_WIDTH), lambda i: (i, 0)),
                  pl.BlockSpec((tm, d), lambda i: (i, 0)),
                  pl.BlockSpec((tm, d), lambda i: (i, 1)),
                  pl.BlockSpec((tm, d), lambda i: (i, 0)),
                  resident(w_ret_up.shape), resident(w_moba_up.shape), resident(w_out.shape),
                  pl.BlockSpec((1, d), lambda i: (0, 0))],
        out_specs=[pl.BlockSpec((tm, d), lambda i: (i, 0)), pl.BlockSpec((tm, d), lambda i: (i, 0))],
        out_shape=[jax.ShapeDtypeStruct((t, d), F32), jax.ShapeDtypeStruct((t, d), BF16)],
        compiler_params=_params(("parallel",), 62),
        name="merge_outproj",
    )(y_ret, y_moba, plain, plain, x, w_ret_up, w_moba_up, w_out, norm_w.reshape(1, d))


def _ffn_up_kernel(h_ref, wa_ref, wb_ref, cwa_ref, cwb_ref, cba_ref, cbb_ref, o_ref, ua_ref, ub_ref,
                   wa16_ref, wb16_ref, *, tiles_per_seq):
    tm = h_ref.shape[0]
    pad = 8
    i = pl.program_id(1)

    @pl.when(i == 0)
    def _():
        wa16_ref[...] = wa_ref[...].astype(wa16_ref.dtype)
        wb16_ref[...] = wb_ref[...].astype(wb16_ref.dtype)

    @pl.when(i % tiles_per_seq == 0)
    def _():
        ua_ref[0:pad, :] = jnp.zeros((pad, ua_ref.shape[1]), F32)
        ub_ref[0:pad, :] = jnp.zeros((pad, ub_ref.shape[1]), F32)

    h = h_ref[...]
    ua_ref[pad:pad + tm, :] = jnp.dot(h, wa16_ref[...], preferred_element_type=F32)
    ub_ref[pad:pad + tm, :] = jnp.dot(h, wb16_ref[...], preferred_element_type=F32)

    def conv(u_ref, cw_ref, cb_ref):
        y = cb_ref[...]
        for j in range(CONV_WIDTH):
            off = pad - (CONV_WIDTH - 1) + j
            y = y + cw_ref[j:j + 1, :] * u_ref[off:off + tm, :]
        return y

    a = conv(ua_ref, cwa_ref, cba_ref)
    b = conv(ub_ref, cwb_ref, cbb_ref)
    o_ref[...] = (_silu(a) * b).astype(o_ref.dtype)
    ua_ref[0:pad, :] = ua_ref[tm:tm + pad, :]
    ub_ref[0:pad, :] = ub_ref[tm:tm + pad, :]


def _ffn_up(h2, w_up, conv_w, conv_b, seq, tm=1024, tn=512):
    t, d = h2.shape
    d_ff = w_up.shape[1] // 2
    nj = d_ff // tn
    conv_b = conv_b.reshape(1, 2 * d_ff)
    return pl.pallas_call(
        functools.partial(_ffn_up_kernel, tiles_per_seq=seq // tm),
        grid=(nj, t // tm),
        in_specs=[pl.BlockSpec((tm, d), lambda j, i: (i, 0)),
                  pl.BlockSpec((d, tn), lambda j, i: (0, j)),
                  pl.BlockSpec((d, tn), lambda j, i: (0, nj + j)),
                  pl.BlockSpec((CONV_WIDTH, tn), lambda j, i: (0, j)),
                  pl.BlockSpec((CONV_WIDTH, tn), lambda j, i: (0, nj + j)),
                  pl.BlockSpec((1, tn), lambda j, i: (0, j)),
                  pl.BlockSpec((1, tn), lambda j, i: (0, nj + j))],
        out_specs=pl.BlockSpec((tm, tn), lambda j, i: (i, j)),
        out_shape=jax.ShapeDtypeStruct((t, d_ff), BF16),
        scratch_shapes=[pltpu.VMEM((tm + 8, tn), F32), pltpu.VMEM((tm + 8, tn), F32),
                        pltpu.VMEM((d, tn), BF16), pltpu.VMEM((d, tn), BF16)],
        compiler_params=_params(("arbitrary", "arbitrary"), 52),
        name="ffn_up_conv_gate",
    )(h2, w_up, w_up, conv_w, conv_w, conv_b, conv_b)


def _ffn_down_kernel(a_ref, w_ref, x_ref, nw_ref, o_ref, *, final_norm):
    x2 = x_ref[...] + jnp.dot(a_ref[...], w_ref[...], preferred_element_type=F32)
    if final_norm:
        r = lax.rsqrt(jnp.mean(x2 * x2, axis=-1, keepdims=True) + EPS)
        x2 = x2 * r * nw_ref[...]
    o_ref[...] = x2


def _ffn_down(act, w_down, x1, norm_w, final_norm, tm=512):
    t, d = x1.shape
    d_ff = act.shape[1]
    return pl.pallas_call(
        functools.partial(_ffn_down_kernel, final_norm=final_norm),
        grid=(t // tm,),
        in_specs=[pl.BlockSpec((tm, d_ff), lambda i: (i, 0)),
                  pl.BlockSpec((d_ff, d), lambda i: (0, 0), pipeline_mode=pl.Buffered(1)),
                  pl.BlockSpec((tm, d), lambda i: (i, 0)),
                  pl.BlockSpec((1, d), lambda i: (0, 0))],
        out_specs=pl.BlockSpec((tm, d), lambda i: (i, 0)),
        out_shape=jax.ShapeDtypeStruct((t, d), F32),
        compiler_params=_params(("parallel",), 62),
        name="ffn_down",
    )(act, w_down, x1, norm_w.reshape(1, d))


def kernel(x, attn_norm_w, w_in, ret_norm_w, w_ret_up, w_moba_up, w_out, ffn_norm_w, w_ffn_up, conv_w,
           conv_b, w_ffn_down, final_norm_w):
    batch, seq, d = x.shape
    depth = w_in.shape[0]
    t = batch * seq
    xf = x.reshape(t, d)
    tm = 1024
    tiles_per_seq = seq // tm

    ret_cos, ret_sin = _ret_rope_tables(seq)
    mob_tables = _moba_rope_tables(seq)
    ret_specs = [pl.BlockSpec((tm, LANES), lambda j, i: (i % tiles_per_seq, 0))] * 2
    mob_specs = [pl.BlockSpec((None, tm, LANES), lambda j, i: (j, i % tiles_per_seq, 0))] * len(mob_tables)

    tn = 1024
    assert RET_WIDTH == tn and MOBA_WIDTH == tn and d == 2 * tn
    plain_block = lambda j: jnp.where(j < 4, j + 7, jnp.where(j < 6, j - 2, 6))
    for l in range(depth):
        h = _rmsnorm(xf, attn_norm_w[l], BF16)
        ret_qk = _in_proj(h, w_in[l], lambda j: j, 2 * tn, _ret_rotary_epilogue, (ret_cos, ret_sin),
                          ret_specs, tm, tn, "in_proj_ret")
        mob_qk = _in_proj(h, w_in[l], lambda j: j + 4, 2 * tn, _moba_rotary_epilogue,
                          mob_tables, mob_specs, tm, tn, "in_proj_moba")
        plain = _in_proj(h, w_in[l], plain_block, 7 * tn, _plain_epilogue, (), [], tm, tn, "in_proj_plain")

        y_ret = _retention(ret_qk, plain, ret_norm_w[l], batch, seq)
        y_moba = _moba(mob_qk, plain, batch, seq)
        x1, h2 = _merge(y_ret, y_moba, plain, xf, w_ret_up[l].astype(BF16), w_moba_up[l].astype(BF16),
                        w_out[l].astype(BF16), ffn_norm_w[l])
        act = _ffn_up(h2, w_ffn_up[l], conv_w[l], conv_b[l], seq)
        last = l == depth - 1
        xf = _ffn_down(act, w_ffn_down[l].astype(BF16), x1, final_norm_w, final_norm=last)
    return xf.reshape(batch, seq, d)
```

```python
import functools
import math

import numpy as np
import jax
import jax.numpy as jnp
from jax import lax
from jax.experimental import pallas as pl
from jax.experimental.pallas import tpu as pltpu

RET_HEADS = 4
RET_HEAD_DIM = 256
RET_WIDTH = RET_HEADS * RET_HEAD_DIM
RET_ROPE_THETA = 10000.0
RET_TILE = 256
RET_STEP_ROWS = 2048
MOBA_HEADS = 8
MOBA_HEAD_DIM = 128
MOBA_WIDTH = MOBA_HEADS * MOBA_HEAD_DIM
MOBA_BLOCK = 256
MOBA_TOPK = 3
MOBA_HEADS_PER_STEP = 2
ROPE_THETA = 500000.0
ROT_DIM = MOBA_HEAD_DIM // 4
CONV_WIDTH = 3
EPS = 1e-6

LANES = 128
MASK_PENALTY = -1e30
BF16 = jnp.bfloat16
F32 = jnp.float32

NT_DIMS = (((1,), (1,)), ((), ()))
TN_DIMS = (((0,), (0,)), ((), ()))


def _params(semantics, vmem_mib):
    return pltpu.CompilerParams(dimension_semantics=semantics, vmem_limit_bytes=vmem_mib * 1024 * 1024)


def _sigmoid(x):
    return 0.5 * jnp.tanh(0.5 * x) + 0.5


def _silu(x):
    h = 0.5 * x
    return h + h * jnp.tanh(h)


def _rmsnorm_kernel(x_ref, w_ref, o_ref):
    x = x_ref[...]
    r = lax.rsqrt(jnp.mean(x * x, axis=-1, keepdims=True) + EPS)
    o_ref[...] = (x * r * w_ref[...]).astype(o_ref.dtype)


def _rmsnorm(x, w, out_dtype, tm=512):
    t, d = x.shape
    return pl.pallas_call(
        _rmsnorm_kernel,
        grid=(t // tm,),
        in_specs=[pl.BlockSpec((tm, d), lambda i: (i, 0)), pl.BlockSpec((1, d), lambda i: (0, 0))],
        out_specs=pl.BlockSpec((tm, d), lambda i: (i, 0)),
        out_shape=jax.ShapeDtypeStruct((t, d), out_dtype),
        compiler_params=_params(("parallel",), 32),
        name="rmsnorm",
    )(x, w.reshape(1, d))


def _plain_epilogue(acc, o_ref):
    o_ref[...] = acc.astype(o_ref.dtype)


def _ret_rotary_epilogue(acc, cos_ref, sin_ref, o_ref):
    cos = cos_ref[...]
    sin = sin_ref[...]
    half = RET_HEAD_DIM // 2
    for hh in range(acc.shape[1] // RET_HEAD_DIM):
        c0 = hh * RET_HEAD_DIM
        x1 = acc[:, c0:c0 + half]
        x2 = acc[:, c0 + half:c0 + RET_HEAD_DIM]
        o_ref[:, c0:c0 + half] = (x1 * cos - x2 * sin).astype(o_ref.dtype)
        o_ref[:, c0 + half:c0 + RET_HEAD_DIM] = (x2 * cos + x1 * sin).astype(o_ref.dtype)


def _moba_rotary_epilogue(acc, c_ref, s_ref, o_ref):
    c = c_ref[...]
    s = s_ref[...]
    half = ROT_DIM // 2
    lane = lax.broadcasted_iota(jnp.int32, (acc.shape[0], MOBA_HEAD_DIM), 1)
    idx = jnp.where(lane < ROT_DIM, lane ^ half, lane)
    for hh in range(acc.shape[1] // MOBA_HEAD_DIM):
        c0 = hh * MOBA_HEAD_DIM
        x = acc[:, c0:c0 + MOBA_HEAD_DIM]
        partner = jnp.take_along_axis(x, idx, axis=1)
        o_ref[:, c0:c0 + MOBA_HEAD_DIM] = (x * c + partner * s).astype(o_ref.dtype)


def _in_proj_kernel(h_ref, w_ref, *rest, epilogue):
    wb_ref = rest[-1]

    @pl.when(pl.program_id(1) == 0)
    def _():
        wb_ref[...] = w_ref[...].astype(wb_ref.dtype)

    acc = jnp.dot(h_ref[...], wb_ref[...], preferred_element_type=F32)
    epilogue(acc, *rest[:-1])


def _in_proj(h, w, col_block, n_out, epilogue, tables, table_specs, tm, tn, name):
    t, d = h.shape
    return pl.pallas_call(
        functools.partial(_in_proj_kernel, epilogue=epilogue),
        grid=(n_out // tn, t // tm),
        in_specs=[pl.BlockSpec((tm, d), lambda j, i: (i, 0)),
                  pl.BlockSpec((d, tn), lambda j, i: (0, col_block(j)))] + table_specs,
        out_specs=pl.BlockSpec((tm, tn), lambda j, i: (i, j)),
        out_shape=jax.ShapeDtypeStruct((t, n_out), BF16),
        scratch_shapes=[pltpu.VMEM((d, tn), BF16)],
        compiler_params=_params(("arbitrary", "arbitrary"), 52),
        name=name,
    )(h, w, *tables)


def _rope_angles(seq, rot_dim, theta):
    pos = np.arange(seq, dtype=np.float64)
    inv = np.float64(theta) ** (-np.arange(0, rot_dim, 2, dtype=np.float64) / rot_dim)
    return pos[:, None] * inv[None, :]


def _ret_rope_tables(seq):
    ang = _rope_angles(seq, RET_HEAD_DIM, RET_ROPE_THETA)
    return jnp.asarray(np.cos(ang), F32), jnp.asarray(np.sin(ang), F32)


def _moba_rope_tables(seq):
    half = ROT_DIM // 2
    ang = _rope_angles(seq, ROT_DIM, ROPE_THETA)
    cos, sin = np.cos(ang), np.sin(ang)
    c = np.ones((seq, MOBA_HEAD_DIM))
    s = np.zeros((seq, MOBA_HEAD_DIM))
    c[:, :half] = cos
    c[:, half:ROT_DIM] = cos
    s[:, :half] = -sin
    s[:, half:ROT_DIM] = sin
    q_scale = MOBA_HEAD_DIM ** -0.5 * math.log2(math.e)
    stack = lambda a: jnp.asarray(np.stack([a * q_scale, a]), F32)
    return stack(c), stack(s)


def _retention_kernel(q_ref, k_ref, v_ref, g_ref, nw_ref, o_ref, state_ref, dec_ref, xi_ref, zeta_ref):
    c = pl.program_id(1)
    tile = RET_TILE
    k_scale = RET_HEAD_DIM ** -0.5
    log_gs = [float(np.log1p(-np.exp2(-5.0 - hh))) for hh in range(RET_HEADS)]

    @pl.when(c == 0)
    def _():
        state_ref[...] = jnp.zeros_like(state_ref)
        row = lax.broadcasted_iota(jnp.int32, (tile, tile), 0).astype(F32)
        col = lax.broadcasted_iota(jnp.int32, (tile, tile), 1).astype(F32)
        diff = row - col
        causal = diff >= 0
        row_d = lax.broadcasted_iota(jnp.int32, (tile, RET_HEAD_DIM), 0).astype(F32)
        for hh, log_g in enumerate(log_gs):
            dec_ref[hh] = jnp.where(causal, jnp.exp(log_g * jnp.where(causal, diff, 0.0)), 0.0) * k_scale
            xi_ref[hh] = jnp.exp(log_g * (row_d + 1.0))
            zeta_ref[hh] = jnp.exp(log_g * (tile - 1.0 - row_d)) * k_scale

    def recurrence(rows):
        outs = []
        for hh, log_g in enumerate(log_gs):
            g_chunk = float(np.exp(log_g * tile))
            sl = slice(hh * RET_HEAD_DIM, (hh + 1) * RET_HEAD_DIM)
            q = q_ref[rows, sl]
            k = k_ref[rows, sl]
            v = v_ref[rows, sl]
            state = state_ref[hh]
            scores = lax.dot_general(q, k, NT_DIMS, preferred_element_type=F32) * dec_ref[hh]
            inner = jnp.dot(scores.astype(BF16), v, preferred_element_type=F32)
            cross = jnp.dot(q, state.astype(BF16), preferred_element_type=F32) * xi_ref[hh]
            kz = (k.astype(F32) * zeta_ref[hh]).astype(BF16)
            state_ref[hh] = state * g_chunk + lax.dot_general(kz, v, TN_DIMS, preferred_element_type=F32)
            outs.append(inner + cross)
        return outs

    def epilogue(rows, outs):
        for hh, o in enumerate(outs):
            sl = slice(hh * RET_HEAD_DIM, (hh + 1) * RET_HEAD_DIM)
            o = o * lax.rsqrt(jnp.mean(o * o, axis=-1, keepdims=True) + EPS)
            o_ref[rows, sl] = (_silu(g_ref[rows, sl].astype(F32)) * (o * nw_ref[:, sl])).astype(o_ref.dtype)

    pending = None
    for sub in range(q_ref.shape[0] // tile):
        rows = slice(sub * tile, (sub + 1) * tile)
        outs = recurrence(rows)
        if pending is not None:
            epilogue(*pending)
        pending = (rows, outs)
    epilogue(*pending)


def _plain_cols(d_model):
    return dict(g_ret=0, g_moba=d_model, rv=2 * d_model, rg=2 * d_model + RET_WIDTH,
                mv=2 * d_model + 2 * RET_WIDTH)


def _col_block(col, width):
    assert col % width == 0
    return col // width


def _retention(qk, plain, norm_w, batch, seq, cols):
    t = qk.shape[0]
    blk = RET_STEP_ROWS
    n_blk = seq // blk
    row = lambda b, c: b * n_blk + c
    hd = RET_HEAD_DIM
    rv_blk = _col_block(cols["rv"], RET_WIDTH)
    rg_blk = _col_block(cols["rg"], RET_WIDTH)
    return pl.pallas_call(
        _retention_kernel,
        grid=(batch, n_blk),
        in_specs=[pl.BlockSpec((blk, RET_WIDTH), lambda b, c: (row(b, c), 0)),
                  pl.BlockSpec((blk, RET_WIDTH), lambda b, c: (row(b, c), 1)),
                  pl.BlockSpec((blk, RET_WIDTH), lambda b, c: (row(b, c), rv_blk)),
                  pl.BlockSpec((blk, RET_WIDTH), lambda b, c: (row(b, c), rg_blk)),
                  pl.BlockSpec((1, RET_WIDTH), lambda b, c: (0, 0))],
        out_specs=pl.BlockSpec((blk, RET_WIDTH), lambda b, c: (row(b, c), 0)),
        out_shape=jax.ShapeDtypeStruct((t, RET_WIDTH), BF16),
        scratch_shapes=[pltpu.VMEM((RET_HEADS, hd, hd), F32),
                        pltpu.VMEM((RET_HEADS, RET_TILE, RET_TILE), F32),
                        pltpu.VMEM((RET_HEADS, RET_TILE, hd), F32),
                        pltpu.VMEM((RET_HEADS, RET_TILE, hd), F32)],
        compiler_params=_params(("parallel", "arbitrary"), 56),
        name="retention",
    )(qk, qk, plain, plain, norm_w.reshape(1, RET_WIDTH))


def _moba_kernel(q_ref, k_ref, v_ref, blk_onehot_ref, o_ref, *scratch):
    seq = q_ref.shape[0]
    d = MOBA_HEAD_DIM
    bs = MOBA_BLOCK
    bs_shift = bs.bit_length() - 1
    assert bs == 1 << bs_shift
    nb = seq // bs
    blk = lax.broadcasted_iota(jnp.int32, (nb, seq), 0)
    own = lax.broadcasted_iota(jnp.int32, (nb, seq), 1) >> bs_shift
    past = blk < own
    qpos = lax.broadcasted_iota(jnp.int32, (bs, bs), 0)
    kpos = lax.broadcasted_iota(jnp.int32, (bs, bs), 1)

    def prepare(hd):
        qa_ref, ka_ref, va_ref = scratch[3 * hd:3 * hd + 3]
        cols = slice(hd * d, (hd + 1) * d)
        q = q_ref[:, cols]
        k = k_ref[:, cols]
        kmean = jnp.sum(k.astype(F32).reshape(nb, bs, d), axis=1) * (1.0 / bs)
        km_hi = kmean.astype(BF16)
        km_lo = (kmean - km_hi.astype(F32)).astype(BF16)
        score = (lax.dot_general(km_hi, q, NT_DIMS, preferred_element_type=F32)
                 + lax.dot_general(km_lo, q, NT_DIMS, preferred_element_type=F32))
        score = jnp.where(past, score, -jnp.inf)
        rank = jnp.zeros((nb, seq), jnp.int32)
        for r in range(1, nb):
            other = pltpu.roll(score, r, axis=0)
            beats = (other > score) | ((other == score) & (blk >= r))
            rank = rank + beats.astype(jnp.int32)
        keep = (past & (rank < MOBA_TOPK)) | (blk == own)
        pen_t = jnp.where(keep, 0.0, MASK_PENALTY)
        pen_t = jnp.concatenate([pen_t, jnp.zeros((LANES - nb, seq), F32)], axis=0)
        qa_ref[:, :d] = q
        qa_ref[:, d:] = pen_t.T.astype(BF16)
        ka_ref[:, :d] = k
        ka_ref[:, d:] = blk_onehot_ref[...]
        va_ref[:, :d] = v_ref[:, cols]
        va_ref[:, d:] = jnp.ones((seq, LANES), BF16)

    def attend(hd):
        qa_ref, ka_ref, va_ref = scratch[3 * hd:3 * hd + 3]

        def scores(qi):
            n_keys = (qi + 1) * bs
            return lax.dot_general(qa_ref[qi * bs:n_keys, :], ka_ref[0:n_keys, :], NT_DIMS,
                                   preferred_element_type=F32)

        order = list(range(nb - 1, -1, -1))
        s_next = scores(order[0])
        for pos, qi in enumerate(order):
            q0 = qi * bs
            n_keys = q0 + bs
            s = s_next
            if pos + 1 < nb:
                s_next = scores(order[pos + 1])
            s_own = jnp.where(kpos <= qpos, s[:, q0:], -jnp.inf)
            s = s_own if qi == 0 else jnp.concatenate([s[:, :q0], s_own], axis=1)
            p = jnp.exp2((s - jnp.max(s, axis=-1, keepdims=True)).astype(BF16))
            acc = jnp.dot(p, va_ref[0:n_keys, :], preferred_element_type=F32)
            o_ref[q0:n_keys, hd * d:(hd + 1) * d] = (acc[:, :d] / acc[:, d:]).astype(o_ref.dtype)
            yield

    n_heads = len(scratch) // 3
    prepare(0)
    for hd in range(n_heads):
        tiles = attend(hd)
        for pos in range(nb):
            next(tiles)
            if pos == 1 and hd + 1 < n_heads:
                prepare(hd + 1)


def _moba(mqk, plain, batch, seq, cols):
    t = mqk.shape[0]
    d = MOBA_HEAD_DIM
    hp = MOBA_HEADS_PER_STEP
    w = hp * d
    n_groups = MOBA_HEADS // hp
    v_blk0 = _col_block(cols["mv"], w)
    blk_onehot = jnp.asarray(np.arange(seq)[:, None] // MOBA_BLOCK == np.arange(LANES)[None, :], BF16)
    return pl.pallas_call(
        _moba_kernel,
        grid=(batch, n_groups),
        in_specs=[pl.BlockSpec((seq, w), lambda b, g: (b, g)),
                  pl.BlockSpec((seq, w), lambda b, g: (b, n_groups + g)),
                  pl.BlockSpec((seq, w), lambda b, g: (b, v_blk0 + g)),
                  pl.BlockSpec((seq, LANES), lambda b, g: (0, 0))],
        out_specs=pl.BlockSpec((seq, w), lambda b, g: (b, g)),
        out_shape=jax.ShapeDtypeStruct((t, MOBA_WIDTH), BF16),
        scratch_shapes=[pltpu.VMEM((seq, d + LANES), BF16)] * (3 * hp),
        compiler_params=_params(("parallel", "parallel"), 48),
        name="moba",
    )(mqk, mqk, plain, blk_onehot)


def _merge_kernel(yr_ref, ym_ref, gr_ref, gm_ref, x_ref, wr_ref, wm_ref, wo_ref, nw_ref, x1_ref, h2_ref):
    ret = jnp.dot(yr_ref[...], wr_ref[...], preferred_element_type=F32)
    mob = jnp.dot(ym_ref[...], wm_ref[...], preferred_element_type=F32)
    merged = _sigmoid(gr_ref[...].astype(F32)) * ret + _sigmoid(gm_ref[...].astype(F32)) * mob
    x1 = x_ref[...] + jnp.dot(merged.astype(BF16), wo_ref[...], preferred_element_type=F32)
    x1_ref[...] = x1
    r = lax.rsqrt(jnp.mean(x1 * x1, axis=-1, keepdims=True) + EPS)
    h2_ref[...] = (x1 * r * nw_ref[...]).astype(h2_ref.dtype)


def _merge(y_ret, y_moba, plain, x, w_ret_up, w_moba_up, w_out, norm_w, cols, tm=512):
    t, d = x.shape
    resident = lambda shape: pl.BlockSpec(shape, lambda i: (0, 0), pipeline_mode=pl.Buffered(1))
    g_ret_blk = _col_block(cols["g_ret"], d)
    g_moba_blk = _col_block(cols["g_moba"], d)
    return pl.pallas_call(
        _merge_kernel,
        grid=(t // tm,),
        in_specs=[pl.BlockSpec((tm, RET_WIDTH), lambda i: (i, 0)),
                  pl.BlockSpec((tm, MOBA_WIDTH), lambda i: (i, 0)),
                  pl.BlockSpec((tm, d), lambda i: (i, g_ret_blk)),
                  pl.BlockSpec((tm, d), lambda i: (i, g_moba_blk)),
                  pl.BlockSpec((tm, d), lambda i: (i, 0)),
                  resident(w_ret_up.shape), resident(w_moba_up.shape), resident(w_out.shape),
                  pl.BlockSpec((1, d), lambda i: (0, 0))],
        out_specs=[pl.BlockSpec((tm, d), lambda i: (i, 0)), pl.BlockSpec((tm, d), lambda i: (i, 0))],
        out_shape=[jax.ShapeDtypeStruct((t, d), F32), jax.ShapeDtypeStruct((t, d), BF16)],
        compiler_params=_params(("parallel",), 62),
        name="merge_outproj",
    )(y_ret, y_moba, plain, plain, x, w_ret_up, w_moba_up, w_out, norm_w.reshape(1, d))


def _ffn_up_kernel(h_ref, wa_ref, wb_ref, cwa_ref, cwb_ref, cba_ref, cbb_ref, o_ref, ua_ref, ub_ref,
                   wa16_ref, wb16_ref, *, tiles_per_seq):
    tm = h_ref.shape[0]
    pad = 8
    i = pl.program_id(1)

    @pl.when(i == 0)
    def _():
        wa16_ref[...] = wa_ref[...].astype(wa16_ref.dtype)
        wb16_ref[...] = wb_ref[...].astype(wb16_ref.dtype)

    @pl.when(i % tiles_per_seq == 0)
    def _():
        ua_ref[0:pad, :] = jnp.zeros((pad, ua_ref.shape[1]), F32)
        ub_ref[0:pad, :] = jnp.zeros((pad, ub_ref.shape[1]), F32)

    h = h_ref[...]
    ua_ref[pad:pad + tm, :] = jnp.dot(h, wa16_ref[...], preferred_element_type=F32)
    ub_ref[pad:pad + tm, :] = jnp.dot(h, wb16_ref[...], preferred_element_type=F32)

    def conv(u_ref, cw_ref, cb_ref):
        y = cb_ref[...]
        for j in range(CONV_WIDTH):
            off = pad - (CONV_WIDTH - 1) + j
            y = y + cw_ref[j:j + 1, :] * u_ref[off:off + tm, :]
        return y

    a = conv(ua_ref, cwa_ref, cba_ref)
    b = conv(ub_ref, cwb_ref, cbb_ref)
    o_ref[...] = (_silu(a) * b).astype(o_ref.dtype)
    ua_ref[0:pad, :] = ua_ref[tm:tm + pad, :]
    ub_ref[0:pad, :] = ub_ref[tm:tm + pad, :]


def _ffn_up(h2, w_up, conv_w, conv_b, seq, tm=1024, tn=512):
    t, d = h2.shape
    d_ff = w_up.shape[1] // 2
    nj = d_ff // tn
    conv_b = conv_b.reshape(1, 2 * d_ff)
    return pl.pallas_call(
        functools.partial(_ffn_up_kernel, tiles_per_seq=seq // tm),
        grid=(nj, t // tm),
        in_specs=[pl.BlockSpec((tm, d), lambda j, i: (i, 0)),
                  pl.BlockSpec((d, tn), lambda j, i: (0, j)),
                  pl.BlockSpec((d, tn), lambda j, i: (0, nj + j)),
                  pl.BlockSpec((CONV_WIDTH, tn), lambda j, i: (0, j)),
                  pl.BlockSpec((CONV_WIDTH, tn), lambda j, i: (0, nj + j)),
                  pl.BlockSpec((1, tn), lambda j, i: (0, j)),
                  pl.BlockSpec((1, tn), lambda j, i: (0, nj + j))],
        out_specs=pl.BlockSpec((tm, tn), lambda j, i: (i, j)),
        out_shape=jax.ShapeDtypeStruct((t, d_ff), BF16),
        scratch_shapes=[pltpu.VMEM((tm + 8, tn), F32), pltpu.VMEM((tm + 8, tn), F32),
                        pltpu.VMEM((d, tn), BF16), pltpu.VMEM((d, tn), BF16)],
        compiler_params=_params(("arbitrary", "arbitrary"), 52),
        name="ffn_up_conv_gate",
    )(h2, w_up, w_up, conv_w, conv_w, conv_b, conv_b)


def _ffn_down_kernel(a_ref, w_ref, x_ref, nw_ref, o_ref, *, final_norm):
    x2 = x_ref[...] + jnp.dot(a_ref[...], w_ref[...], preferred_element_type=F32)
    if final_norm:
        r = lax.rsqrt(jnp.mean(x2 * x2, axis=-1, keepdims=True) + EPS)
        x2 = x2 * r * nw_ref[...]
    o_ref[...] = x2


def _ffn_down(act, w_down, x1, norm_w, final_norm, tm=512):
    t, d = x1.shape
    d_ff = act.shape[1]
    return pl.pallas_call(
        functools.partial(_ffn_down_kernel, final_norm=final_norm),
        grid=(t // tm,),
        in_specs=[pl.BlockSpec((tm, d_ff), lambda i: (i, 0)),
                  pl.BlockSpec((d_ff, d), lambda i: (0, 0), pipeline_mode=pl.Buffered(1)),
                  pl.BlockSpec((tm, d), lambda i: (i, 0)),
                  pl.BlockSpec((1, d), lambda i: (0, 0))],
        out_specs=pl.BlockSpec((tm, d), lambda i: (i, 0)),
        out_shape=jax.ShapeDtypeStruct((t, d), F32),
        compiler_params=_params(("parallel",), 62),
        name="ffn_down",
    )(act, w_down, x1, norm_w.reshape(1, d))


def kernel(x, attn_norm_w, w_in, ret_norm_w, w_ret_up, w_moba_up, w_out, ffn_norm_w, w_ffn_up, conv_w,
           conv_b, w_ffn_down, final_norm_w):
    batch, seq, d = x.shape
    depth = w_in.shape[0]
    t = batch * seq
    xf = x.reshape(t, d)
    tm = 1024
    tiles_per_seq = seq // tm

    ret_cos, ret_sin = _ret_rope_tables(seq)
    mob_tables = _moba_rope_tables(seq)
    ret_specs = [pl.BlockSpec((tm, LANES), lambda j, i: (i % tiles_per_seq, 0))] * 2
    mob_specs = [pl.BlockSpec((None, tm, LANES), lambda j, i: (j, i % tiles_per_seq, 0))] * len(mob_tables)

    tn = 1024
    assert RET_WIDTH == tn and MOBA_WIDTH == tn and d % tn == 0
    w_in_col = dict(rv=2 * RET_WIDTH, rg=3 * RET_WIDTH, mv=4 * RET_WIDTH + 2 * MOBA_WIDTH,
                    g_ret=4 * RET_WIDTH + 3 * MOBA_WIDTH, g_moba=4 * RET_WIDTH + 3 * MOBA_WIDTH + d)
    cols = _plain_cols(d)
    widths = dict(g_ret=d, g_moba=d, rv=RET_WIDTH, rg=RET_WIDTH, mv=MOBA_WIDTH)
    src_tiles = [None] * (sum(widths.values()) // tn)
    for name, width in widths.items():
        for k in range(width // tn):
            src_tiles[_col_block(cols[name], tn) + k] = _col_block(w_in_col[name], tn) + k

    def plain_block(j):
        blk = src_tiles[-1]
        for out_tile in range(len(src_tiles) - 2, -1, -1):
            blk = jnp.where(j == out_tile, src_tiles[out_tile], blk)
        return blk

    for l in range(depth):
        h = _rmsnorm(xf, attn_norm_w[l], BF16)
        ret_qk = _in_proj(h, w_in[l], lambda j: j, 2 * tn, _ret_rotary_epilogue, (ret_cos, ret_sin),
                          ret_specs, tm, tn, "in_proj_ret")
        mob_qk = _in_proj(h, w_in[l], lambda j: j + _col_block(4 * RET_WIDTH, tn), 2 * tn, _moba_rotary_epilogue,
                          mob_tables, mob_specs, tm, tn, "in_proj_moba")
        plain = _in_proj(h, w_in[l], plain_block, len(src_tiles) * tn, _plain_epilogue, (), [], tm, tn,
                         "in_proj_plain")

        y_ret = _retention(ret_qk, plain, ret_norm_w[l], batch, seq, cols)
        y_moba = _moba(mob_qk, plain, batch, seq, cols)
        x1, h2 = _merge(y_ret, y_moba, plain, xf, w_ret_up[l].astype(BF16), w_moba_up[l].astype(BF16),
                        w_out[l].astype(BF16), ffn_norm_w[l], cols)
        act = _ffn_up(h2, w_ffn_up[l], conv_w[l], conv_b[l], seq)
        last = l == depth - 1
        xf = _ffn_down(act, w_ffn_down[l].astype(BF16), x1, final_norm_w, final_norm=last)
    return xf.reshape(batch, seq, d)
```

```python
import functools
import math

import numpy as np
import jax
import jax.numpy as jnp
from jax import lax
from jax.experimental import pallas as pl
from jax.experimental.pallas import tpu as pltpu

RET_HEADS = 4
RET_HEAD_DIM = 256
RET_WIDTH = RET_HEADS * RET_HEAD_DIM
RET_ROPE_THETA = 10000.0
RET_TILE = 256
RET_STEP_ROWS = 2048
MOBA_HEADS = 8
MOBA_HEAD_DIM = 128
MOBA_WIDTH = MOBA_HEADS * MOBA_HEAD_DIM
MOBA_BLOCK = 256
MOBA_TOPK = 3
MOBA_HEADS_PER_STEP = 2
ROPE_THETA = 500000.0
ROT_DIM = MOBA_HEAD_DIM // 4
CONV_WIDTH = 3
EPS = 1e-6

LANES = 128
BF16_SUBLANES = 16
MASK_PENALTY = -1e30
BF16 = jnp.bfloat16
F32 = jnp.float32

NT_DIMS = (((1,), (1,)), ((), ()))
TN_DIMS = (((0,), (0,)), ((), ()))


def _params(semantics, vmem_mib):
    return pltpu.CompilerParams(dimension_semantics=semantics, vmem_limit_bytes=vmem_mib * 1024 * 1024)


def _sigmoid(x):
    return 0.5 * jnp.tanh(0.5 * x) + 0.5


def _silu(x):
    h = 0.5 * x
    return h + h * jnp.tanh(h)


def _rmsnorm_kernel(x_ref, w_ref, o_ref):
    x = x_ref[...]
    r = lax.rsqrt(jnp.mean(x * x, axis=-1, keepdims=True) + EPS)
    o_ref[...] = (x * r * w_ref[...]).astype(o_ref.dtype)


def _rmsnorm(x, w, out_dtype, tm=512):
    t, d = x.shape
    return pl.pallas_call(
        _rmsnorm_kernel,
        grid=(t // tm,),
        in_specs=[pl.BlockSpec((tm, d), lambda i: (i, 0)), pl.BlockSpec((1, d), lambda i: (0, 0))],
        out_specs=pl.BlockSpec((tm, d), lambda i: (i, 0)),
        out_shape=jax.ShapeDtypeStruct((t, d), out_dtype),
        compiler_params=_params(("parallel",), 32),
        name="rmsnorm",
    )(x, w.reshape(1, d))


def _plain_epilogue(acc, o_ref):
    o_ref[...] = acc.astype(o_ref.dtype)


def _ret_rotary_epilogue(acc, cos_ref, sin_ref, o_ref):
    cos = cos_ref[...]
    sin = sin_ref[...]
    half = RET_HEAD_DIM // 2
    for hh in range(acc.shape[1] // RET_HEAD_DIM):
        c0 = hh * RET_HEAD_DIM
        x1 = acc[:, c0:c0 + half]
        x2 = acc[:, c0 + half:c0 + RET_HEAD_DIM]
        o_ref[:, c0:c0 + half] = (x1 * cos - x2 * sin).astype(o_ref.dtype)
        o_ref[:, c0 + half:c0 + RET_HEAD_DIM] = (x2 * cos + x1 * sin).astype(o_ref.dtype)


def _moba_rotary_epilogue(acc, c_ref, s_ref, o_ref):
    c = c_ref[...]
    s = s_ref[...]
    half = ROT_DIM // 2
    lane = lax.broadcasted_iota(jnp.int32, (acc.shape[0], MOBA_HEAD_DIM), 1)
    idx = jnp.where(lane < ROT_DIM, lane ^ half, lane)
    for hh in range(acc.shape[1] // MOBA_HEAD_DIM):
        c0 = hh * MOBA_HEAD_DIM
        x = acc[:, c0:c0 + MOBA_HEAD_DIM]
        partner = jnp.take_along_axis(x, idx, axis=1)
        o_ref[:, c0:c0 + MOBA_HEAD_DIM] = (x * c + partner * s).astype(o_ref.dtype)


def _in_proj_kernel(h_ref, w_ref, *rest, epilogue, n_tables, n_casts):
    tables = rest[:n_tables]
    cast_src = rest[n_tables:n_tables + n_casts]
    o_ref = rest[n_tables + n_casts]
    cast_dst = rest[n_tables + n_casts + 1:n_tables + 2 * n_casts + 1]
    wb_ref = rest[-1]

    @pl.when(pl.program_id(1) == 0)
    def _():
        wb_ref[...] = w_ref[...].astype(wb_ref.dtype)

    acc = jnp.dot(h_ref[...], wb_ref[...], preferred_element_type=F32)
    epilogue(acc, *tables, o_ref)
    for src_ref, dst_ref in zip(cast_src, cast_dst):
        dst_ref[...] = src_ref[...].astype(dst_ref.dtype)


def _cast_rows_per_step(n_rows, n_steps):
    rows = BF16_SUBLANES
    while n_rows % rows or n_rows // rows > n_steps:
        rows += BF16_SUBLANES
    return rows


def _in_proj(h, w, col_block, n_out, epilogue, tables, table_specs, tm, tn, name, casts=()):
    t, d = h.shape
    n_i = t // tm
    n_steps = (n_out // tn) * n_i
    cast_specs, cast_shapes = [], []
    for c in casts:
        rows = _cast_rows_per_step(c.shape[0], n_steps)
        last = c.shape[0] // rows - 1
        spec = pl.BlockSpec((rows, c.shape[1]), lambda j, i, last=last: (jnp.minimum(j * n_i + i, last), 0))
        cast_specs.append(spec)
        cast_shapes.append(jax.ShapeDtypeStruct(c.shape, BF16))
    outs = pl.pallas_call(
        functools.partial(_in_proj_kernel, epilogue=epilogue, n_tables=len(tables), n_casts=len(casts)),
        grid=(n_out // tn, n_i),
        in_specs=[pl.BlockSpec((tm, d), lambda j, i: (i, 0)),
                  pl.BlockSpec((d, tn), lambda j, i: (0, col_block(j)))] + table_specs + cast_specs,
        out_specs=[pl.BlockSpec((tm, tn), lambda j, i: (i, j))] + cast_specs,
        out_shape=[jax.ShapeDtypeStruct((t, n_out), BF16)] + cast_shapes,
        scratch_shapes=[pltpu.VMEM((d, tn), BF16)],
        compiler_params=_params(("arbitrary", "arbitrary"), 52),
        name=name,
    )(h, w, *tables, *casts)
    return outs[0], outs[1:]


def _rope_angles(seq, rot_dim, theta):
    pos = np.arange(seq, dtype=np.float64)
    inv = np.float64(theta) ** (-np.arange(0, rot_dim, 2, dtype=np.float64) / rot_dim)
    return pos[:, None] * inv[None, :]


def _ret_rope_tables(seq):
    ang = _rope_angles(seq, RET_HEAD_DIM, RET_ROPE_THETA)
    return jnp.asarray(np.cos(ang), F32), jnp.asarray(np.sin(ang), F32)


def _moba_rope_tables(seq):
    half = ROT_DIM // 2
    ang = _rope_angles(seq, ROT_DIM, ROPE_THETA)
    cos, sin = np.cos(ang), np.sin(ang)
    c = np.ones((seq, MOBA_HEAD_DIM))
    s = np.zeros((seq, MOBA_HEAD_DIM))
    c[:, :half] = cos
    c[:, half:ROT_DIM] = cos
    s[:, :half] = -sin
    s[:, half:ROT_DIM] = sin
    q_scale = MOBA_HEAD_DIM ** -0.5 * math.log2(math.e)
    stack = lambda a: jnp.asarray(np.stack([a * q_scale, a]), F32)
    return stack(c), stack(s)


def _retention_kernel(q_ref, k_ref, v_ref, g_ref, nw_ref, o_ref, state_ref, dec_ref, xi_ref, zeta_ref):
    c = pl.program_id(1)
    tile = RET_TILE
    k_scale = RET_HEAD_DIM ** -0.5
    log_gs = [float(np.log1p(-np.exp2(-5.0 - hh))) for hh in range(RET_HEADS)]

    @pl.when(c == 0)
    def _():
        state_ref[...] = jnp.zeros_like(state_ref)
        row = lax.broadcasted_iota(jnp.int32, (tile, tile), 0).astype(F32)
        col = lax.broadcasted_iota(jnp.int32, (tile, tile), 1).astype(F32)
        diff = row - col
        causal = diff >= 0
        row_d = lax.broadcasted_iota(jnp.int32, (tile, RET_HEAD_DIM), 0).astype(F32)
        for hh, log_g in enumerate(log_gs):
            dec_ref[hh] = jnp.where(causal, jnp.exp(log_g * jnp.where(causal, diff, 0.0)), 0.0) * k_scale
            xi_ref[hh] = jnp.exp(log_g * (row_d + 1.0))
            zeta_ref[hh] = jnp.exp(log_g * (tile - 1.0 - row_d)) * k_scale

    def recurrence(rows):
        outs = []
        for hh, log_g in enumerate(log_gs):
            g_chunk = float(np.exp(log_g * tile))
            sl = slice(hh * RET_HEAD_DIM, (hh + 1) * RET_HEAD_DIM)
            q = q_ref[rows, sl]
            k = k_ref[rows, sl]
            v = v_ref[rows, sl]
            state = state_ref[hh]
            scores = lax.dot_general(q, k, NT_DIMS, preferred_element_type=F32) * dec_ref[hh]
            inner = jnp.dot(scores.astype(BF16), v, preferred_element_type=F32)
            cross = jnp.dot(q, state.astype(BF16), preferred_element_type=F32) * xi_ref[hh]
            kz = (k.astype(F32) * zeta_ref[hh]).astype(BF16)
            state_ref[hh] = state * g_chunk + lax.dot_general(kz, v, TN_DIMS, preferred_element_type=F32)
            outs.append(inner + cross)
        return outs

    def epilogue(rows, outs):
        for hh, o in enumerate(outs):
            sl = slice(hh * RET_HEAD_DIM, (hh + 1) * RET_HEAD_DIM)
            o = o * lax.rsqrt(jnp.mean(o * o, axis=-1, keepdims=True) + EPS)
            o_ref[rows, sl] = (_silu(g_ref[rows, sl].astype(F32)) * (o * nw_ref[:, sl])).astype(o_ref.dtype)

    pending = None
    for sub in range(q_ref.shape[0] // tile):
        rows = slice(sub * tile, (sub + 1) * tile)
        outs = recurrence(rows)
        if pending is not None:
            epilogue(*pending)
        pending = (rows, outs)
    epilogue(*pending)


def _plain_cols(d_model):
    return dict(g_ret=0, g_moba=d_model, rv=2 * d_model, rg=2 * d_model + RET_WIDTH,
                mv=2 * d_model + 2 * RET_WIDTH)


def _col_block(col, width):
    assert col % width == 0
    return col // width


def _retention(qk, plain, norm_w, batch, seq, cols):
    t = qk.shape[0]
    blk = RET_STEP_ROWS
    n_blk = seq // blk
    row = lambda b, c: b * n_blk + c
    hd = RET_HEAD_DIM
    rv_blk = _col_block(cols["rv"], RET_WIDTH)
    rg_blk = _col_block(cols["rg"], RET_WIDTH)
    return pl.pallas_call(
        _retention_kernel,
        grid=(batch, n_blk),
        in_specs=[pl.BlockSpec((blk, RET_WIDTH), lambda b, c: (row(b, c), 0)),
                  pl.BlockSpec((blk, RET_WIDTH), lambda b, c: (row(b, c), 1)),
                  pl.BlockSpec((blk, RET_WIDTH), lambda b, c: (row(b, c), rv_blk)),
                  pl.BlockSpec((blk, RET_WIDTH), lambda b, c: (row(b, c), rg_blk)),
                  pl.BlockSpec((1, RET_WIDTH), lambda b, c: (0, 0))],
        out_specs=pl.BlockSpec((blk, RET_WIDTH), lambda b, c: (row(b, c), 0)),
        out_shape=jax.ShapeDtypeStruct((t, RET_WIDTH), BF16),
        scratch_shapes=[pltpu.VMEM((RET_HEADS, hd, hd), F32),
                        pltpu.VMEM((RET_HEADS, RET_TILE, RET_TILE), F32),
                        pltpu.VMEM((RET_HEADS, RET_TILE, hd), F32),
                        pltpu.VMEM((RET_HEADS, RET_TILE, hd), F32)],
        compiler_params=_params(("parallel", "arbitrary"), 56),
        name="retention",
    )(qk, qk, plain, plain, norm_w.reshape(1, RET_WIDTH))


def _moba_kernel(q_ref, k_ref, v_ref, blk_onehot_ref, o_ref, *scratch):
    seq = q_ref.shape[0]
    d = MOBA_HEAD_DIM
    bs = MOBA_BLOCK
    bs_shift = bs.bit_length() - 1
    assert bs == 1 << bs_shift
    nb = seq // bs
    blk = lax.broadcasted_iota(jnp.int32, (nb, seq), 0)
    own = lax.broadcasted_iota(jnp.int32, (nb, seq), 1) >> bs_shift
    past = blk < own
    qpos = lax.broadcasted_iota(jnp.int32, (bs, bs), 0)
    kpos = lax.broadcasted_iota(jnp.int32, (bs, bs), 1)

    def prepare(hd):
        qa_ref, ka_ref, va_ref = scratch[3 * hd:3 * hd + 3]
        cols = slice(hd * d, (hd + 1) * d)
        q = q_ref[:, cols]
        k = k_ref[:, cols]
        kmean = jnp.sum(k.astype(F32).reshape(nb, bs, d), axis=1) * (1.0 / bs)
        km_hi = kmean.astype(BF16)
        km_lo = (kmean - km_hi.astype(F32)).astype(BF16)
        score = (lax.dot_general(km_hi, q, NT_DIMS, preferred_element_type=F32)
                 + lax.dot_general(km_lo, q, NT_DIMS, preferred_element_type=F32))
        score = jnp.where(past, score, -jnp.inf)
        rank = jnp.zeros((nb, seq), jnp.int32)
        for r in range(1, nb):
            other = pltpu.roll(score, r, axis=0)
            beats = (other > score) | ((other == score) & (blk >= r))
            rank = rank + beats.astype(jnp.int32)
        keep = (past & (rank < MOBA_TOPK)) | (blk == own)
        pen_t = jnp.where(keep, 0.0, MASK_PENALTY)
        pen_t = jnp.concatenate([pen_t, jnp.zeros((LANES - nb, seq), F32)], axis=0)
        qa_ref[:, :d] = q
        qa_ref[:, d:] = pen_t.T.astype(BF16)
        ka_ref[:, :d] = k
        ka_ref[:, d:] = blk_onehot_ref[...]
        va_ref[:, :d] = v_ref[:, cols]
        va_ref[:, d:] = jnp.ones((seq, LANES), BF16)

    def attend(hd):
        qa_ref, ka_ref, va_ref = scratch[3 * hd:3 * hd + 3]

        def scores(qi):
            n_keys = (qi + 1) * bs
            return lax.dot_general(qa_ref[qi * bs:n_keys, :], ka_ref[0:n_keys, :], NT_DIMS,
                                   preferred_element_type=F32)

        order = list(range(nb - 1, -1, -1))
        s_next = scores(order[0])
        for pos, qi in enumerate(order):
            q0 = qi * bs
            n_keys = q0 + bs
            s = s_next
            if pos + 1 < nb:
                s_next = scores(order[pos + 1])
            s_own = jnp.where(kpos <= qpos, s[:, q0:], -jnp.inf)
            s = s_own if qi == 0 else jnp.concatenate([s[:, :q0], s_own], axis=1)
            p = jnp.exp2((s - jnp.max(s, axis=-1, keepdims=True)).astype(BF16))
            acc = jnp.dot(p, va_ref[0:n_keys, :], preferred_element_type=F32)
            o_ref[q0:n_keys, hd * d:(hd + 1) * d] = (acc[:, :d] / acc[:, d:]).astype(o_ref.dtype)
            yield

    n_heads = len(scratch) // 3
    prepare(0)
    for hd in range(n_heads):
        tiles = attend(hd)
        for pos in range(nb):
            next(tiles)
            if pos == 1 and hd + 1 < n_heads:
                prepare(hd + 1)


def _moba(mqk, plain, batch, seq, cols):
    t = mqk.shape[0]
    d = MOBA_HEAD_DIM
    hp = MOBA_HEADS_PER_STEP
    w = hp * d
    n_groups = MOBA_HEADS // hp
    v_blk0 = _col_block(cols["mv"], w)
    blk_onehot = jnp.asarray(np.arange(seq)[:, None] // MOBA_BLOCK == np.arange(LANES)[None, :], BF16)
    return pl.pallas_call(
        _moba_kernel,
        grid=(batch, n_groups),
        in_specs=[pl.BlockSpec((seq, w), lambda b, g: (b, g)),
                  pl.BlockSpec((seq, w), lambda b, g: (b, n_groups + g)),
                  pl.BlockSpec((seq, w), lambda b, g: (b, v_blk0 + g)),
                  pl.BlockSpec((seq, LANES), lambda b, g: (0, 0))],
        out_specs=pl.BlockSpec((seq, w), lambda b, g: (b, g)),
        out_shape=jax.ShapeDtypeStruct((t, MOBA_WIDTH), BF16),
        scratch_shapes=[pltpu.VMEM((seq, d + LANES), BF16)] * (3 * hp),
        compiler_params=_params(("parallel", "parallel"), 48),
        name="moba",
    )(mqk, mqk, plain, blk_onehot)


def _merge_kernel(yr_ref, ym_ref, gr_ref, gm_ref, x_ref, wr_ref, wm_ref, wo_ref, nw_ref, x1_ref, h2_ref):
    ret = jnp.dot(yr_ref[...], wr_ref[...], preferred_element_type=F32)
    mob = jnp.dot(ym_ref[...], wm_ref[...], preferred_element_type=F32)
    merged = _sigmoid(gr_ref[...].astype(F32)) * ret + _sigmoid(gm_ref[...].astype(F32)) * mob
    x1 = x_ref[...] + jnp.dot(merged.astype(BF16), wo_ref[...], preferred_element_type=F32)
    x1_ref[...] = x1
    r = lax.rsqrt(jnp.mean(x1 * x1, axis=-1, keepdims=True) + EPS)
    h2_ref[...] = (x1 * r * nw_ref[...]).astype(h2_ref.dtype)


def _merge(y_ret, y_moba, plain, x, w_ret_up, w_moba_up, w_out, norm_w, cols, tm=512):
    t, d = x.shape
    resident = lambda shape: pl.BlockSpec(shape, lambda i: (0, 0), pipeline_mode=pl.Buffered(1))
    g_ret_blk = _col_block(cols["g_ret"], d)
    g_moba_blk = _col_block(cols["g_moba"], d)
    return pl.pallas_call(
        _merge_kernel,
        grid=(t // tm,),
        in_specs=[pl.BlockSpec((tm, RET_WIDTH), lambda i: (i, 0)),
                  pl.BlockSpec((tm, MOBA_WIDTH), lambda i: (i, 0)),
                  pl.BlockSpec((tm, d), lambda i: (i, g_ret_blk)),
                  pl.BlockSpec((tm, d), lambda i: (i, g_moba_blk)),
                  pl.BlockSpec((tm, d), lambda i: (i, 0)),
                  resident(w_ret_up.shape), resident(w_moba_up.shape), resident(w_out.shape),
                  pl.BlockSpec((1, d), lambda i: (0, 0))],
        out_specs=[pl.BlockSpec((tm, d), lambda i: (i, 0)), pl.BlockSpec((tm, d), lambda i: (i, 0))],
        out_shape=[jax.ShapeDtypeStruct((t, d), F32), jax.ShapeDtypeStruct((t, d), BF16)],
        compiler_params=_params(("parallel",), 62),
        name="merge_outproj",
    )(y_ret, y_moba, plain, plain, x, w_ret_up, w_moba_up, w_out, norm_w.reshape(1, d))


def _ffn_up_kernel(h_ref, wa_ref, wb_ref, cwa_ref, cwb_ref, cba_ref, cbb_ref, o_ref, ua_ref, ub_ref,
                   wa16_ref, wb16_ref, *, tiles_per_seq):
    tm = h_ref.shape[0]
    pad = 8
    i = pl.program_id(1)

    @pl.when(i == 0)
    def _():
        wa16_ref[...] = wa_ref[...].astype(wa16_ref.dtype)
        wb16_ref[...] = wb_ref[...].astype(wb16_ref.dtype)

    @pl.when(i % tiles_per_seq == 0)
    def _():
        ua_ref[0:pad, :] = jnp.zeros((pad, ua_ref.shape[1]), F32)
        ub_ref[0:pad, :] = jnp.zeros((pad, ub_ref.shape[1]), F32)

    h = h_ref[...]
    ua_ref[pad:pad + tm, :] = jnp.dot(h, wa16_ref[...], preferred_element_type=F32)
    ub_ref[pad:pad + tm, :] = jnp.dot(h, wb16_ref[...], preferred_element_type=F32)

    def conv(u_ref, cw_ref, cb_ref):
        y = cb_ref[...]
        for j in range(CONV_WIDTH):
            off = pad - (CONV_WIDTH - 1) + j
            y = y + cw_ref[j:j + 1, :] * u_ref[off:off + tm, :]
        return y

    a = conv(ua_ref, cwa_ref, cba_ref)
    b = conv(ub_ref, cwb_ref, cbb_ref)
    o_ref[...] = (_silu(a) * b).astype(o_ref.dtype)
    ua_ref[0:pad, :] = ua_ref[tm:tm + pad, :]
    ub_ref[0:pad, :] = ub_ref[tm:tm + pad, :]


def _ffn_up(h2, w_up, conv_w, conv_b, seq, tm=1024, tn=512):
    t, d = h2.shape
    d_ff = w_up.shape[1] // 2
    nj = d_ff // tn
    conv_b = conv_b.reshape(1, 2 * d_ff)
    return pl.pallas_call(
        functools.partial(_ffn_up_kernel, tiles_per_seq=seq // tm),
        grid=(nj, t // tm),
        in_specs=[pl.BlockSpec((tm, d), lambda j, i: (i, 0)),
                  pl.BlockSpec((d, tn), lambda j, i: (0, j)),
                  pl.BlockSpec((d, tn), lambda j, i: (0, nj + j)),
                  pl.BlockSpec((CONV_WIDTH, tn), lambda j, i: (0, j)),
                  pl.BlockSpec((CONV_WIDTH, tn), lambda j, i: (0, nj + j)),
                  pl.BlockSpec((1, tn), lambda j, i: (0, j)),
                  pl.BlockSpec((1, tn), lambda j, i: (0, nj + j))],
        out_specs=pl.BlockSpec((tm, tn), lambda j, i: (i, j)),
        out_shape=jax.ShapeDtypeStruct((t, d_ff), BF16),
        scratch_shapes=[pltpu.VMEM((tm + 8, tn), F32), pltpu.VMEM((tm + 8, tn), F32),
                        pltpu.VMEM((d, tn), BF16), pltpu.VMEM((d, tn), BF16)],
        compiler_params=_params(("arbitrary", "arbitrary"), 52),
        name="ffn_up_conv_gate",
    )(h2, w_up, w_up, conv_w, conv_w, conv_b, conv_b)


def _ffn_down_kernel(a_ref, w_ref, x_ref, nw_ref, o_ref, *, final_norm):
    x2 = x_ref[...] + jnp.dot(a_ref[...], w_ref[...], preferred_element_type=F32)
    if final_norm:
        r = lax.rsqrt(jnp.mean(x2 * x2, axis=-1, keepdims=True) + EPS)
        x2 = x2 * r * nw_ref[...]
    o_ref[...] = x2


def _ffn_down(act, w_down, x1, norm_w, final_norm, tm=512):
    t, d = x1.shape
    d_ff = act.shape[1]
    return pl.pallas_call(
        functools.partial(_ffn_down_kernel, final_norm=final_norm),
        grid=(t // tm,),
        in_specs=[pl.BlockSpec((tm, d_ff), lambda i: (i, 0)),
                  pl.BlockSpec((d_ff, d), lambda i: (0, 0), pipeline_mode=pl.Buffered(1)),
                  pl.BlockSpec((tm, d), lambda i: (i, 0)),
                  pl.BlockSpec((1, d), lambda i: (0, 0))],
        out_specs=pl.BlockSpec((tm, d), lambda i: (i, 0)),
        out_shape=jax.ShapeDtypeStruct((t, d), F32),
        compiler_params=_params(("parallel",), 62),
        name="ffn_down",
    )(act, w_down, x1, norm_w.reshape(1, d))


def kernel(x, attn_norm_w, w_in, ret_norm_w, w_ret_up, w_moba_up, w_out, ffn_norm_w, w_ffn_up, conv_w,
           conv_b, w_ffn_down, final_norm_w):
    batch, seq, d = x.shape
    depth = w_in.shape[0]
    t = batch * seq
    xf = x.reshape(t, d)
    tm = 1024
    tiles_per_seq = seq // tm

    ret_cos, ret_sin = _ret_rope_tables(seq)
    mob_tables = _moba_rope_tables(seq)
    ret_specs = [pl.BlockSpec((tm, LANES), lambda j, i: (i % tiles_per_seq, 0))] * 2
    mob_specs = [pl.BlockSpec((None, tm, LANES), lambda j, i: (j, i % tiles_per_seq, 0))] * len(mob_tables)

    tn = 1024
    assert RET_WIDTH == tn and MOBA_WIDTH == tn and d % tn == 0
    w_in_col = dict(rv=2 * RET_WIDTH, rg=3 * RET_WIDTH, mv=4 * RET_WIDTH + 2 * MOBA_WIDTH,
                    g_ret=4 * RET_WIDTH + 3 * MOBA_WIDTH, g_moba=4 * RET_WIDTH + 3 * MOBA_WIDTH + d)
    cols = _plain_cols(d)
    widths = dict(g_ret=d, g_moba=d, rv=RET_WIDTH, rg=RET_WIDTH, mv=MOBA_WIDTH)
    src_tiles = [None] * (sum(widths.values()) // tn)
    for name, width in widths.items():
        for k in range(width // tn):
            src_tiles[_col_block(cols[name], tn) + k] = _col_block(w_in_col[name], tn) + k

    def plain_block(j):
        blk = src_tiles[-1]
        for out_tile in range(len(src_tiles) - 2, -1, -1):
            blk = jnp.where(j == out_tile, src_tiles[out_tile], blk)
        return blk

    for l in range(depth):
        h = _rmsnorm(xf, attn_norm_w[l], BF16)
        ret_qk, (w_out16,) = _in_proj(h, w_in[l], lambda j: j, 2 * tn, _ret_rotary_epilogue,
                                      (ret_cos, ret_sin), ret_specs, tm, tn, "in_proj_ret",
                                      casts=(w_out[l],))
        mob_qk, (w_ret_up16, w_moba_up16) = _in_proj(
            h, w_in[l], lambda j: j + _col_block(4 * RET_WIDTH, tn), 2 * tn, _moba_rotary_epilogue,
            mob_tables, mob_specs, tm, tn, "in_proj_moba", casts=(w_ret_up[l], w_moba_up[l]))
        plain, (w_down16,) = _in_proj(h, w_in[l], plain_block, len(src_tiles) * tn, _plain_epilogue, (), [],
                                      tm, tn, "in_proj_plain", casts=(w_ffn_down[l],))

        y_ret = _retention(ret_qk, plain, ret_norm_w[l], batch, seq, cols)
        y_moba = _moba(mob_qk, plain, batch, seq, cols)
        x1, h2 = _merge(y_ret, y_moba, plain, xf, w_ret_up16, w_moba_up16, w_out16, ffn_norm_w[l], cols)
        act = _ffn_up(h2, w_ffn_up[l], conv_w[l], conv_b[l], seq)
        last = l == depth - 1
        xf = _ffn_down(act, w_down16, x1, final_norm_w, final_norm=last)
    return xf.reshape(batch, seq, d)
```

```python
import functools
import math

import numpy as np
import jax
import jax.numpy as jnp
from jax import lax
from jax.experimental import pallas as pl
from jax.experimental.pallas import tpu as pltpu

RET_HEADS = 4
RET_HEAD_DIM = 256
RET_WIDTH = RET_HEADS * RET_HEAD_DIM
RET_ROPE_THETA = 10000.0
RET_TILE = 256
RET_STEP_ROWS = 2048
MOBA_HEADS = 8
MOBA_HEAD_DIM = 128
MOBA_WIDTH = MOBA_HEADS * MOBA_HEAD_DIM
MOBA_BLOCK = 256
MOBA_TOPK = 3
MOBA_HEADS_PER_STEP = 2
ROPE_THETA = 500000.0
ROT_DIM = MOBA_HEAD_DIM // 4
CONV_WIDTH = 3
EPS = 1e-6

LANES = 128
BF16_SUBLANES = 16
MASK_PENALTY = -1e30
BF16 = jnp.bfloat16
F32 = jnp.float32

NT_DIMS = (((1,), (1,)), ((), ()))
TN_DIMS = (((0,), (0,)), ((), ()))


def _params(semantics, vmem_mib):
    return pltpu.CompilerParams(dimension_semantics=semantics, vmem_limit_bytes=vmem_mib * 1024 * 1024)


def _sigmoid(x):
    return 0.5 * jnp.tanh(0.5 * x) + 0.5


def _silu(x):
    h = 0.5 * x
    return h + h * jnp.tanh(h)


def _plain_epilogue(acc, o_ref):
    o_ref[...] = acc.astype(o_ref.dtype)


def _ret_rotary_epilogue(acc, cos_ref, sin_ref, o_ref):
    cos = cos_ref[...]
    sin = sin_ref[...]
    half = RET_HEAD_DIM // 2
    for hh in range(acc.shape[1] // RET_HEAD_DIM):
        c0 = hh * RET_HEAD_DIM
        x1 = acc[:, c0:c0 + half]
        x2 = acc[:, c0 + half:c0 + RET_HEAD_DIM]
        o_ref[:, c0:c0 + half] = (x1 * cos - x2 * sin).astype(o_ref.dtype)
        o_ref[:, c0 + half:c0 + RET_HEAD_DIM] = (x2 * cos + x1 * sin).astype(o_ref.dtype)


def _moba_rotary_epilogue(acc, c_ref, s_ref, o_ref):
    c = c_ref[...]
    s = s_ref[...]
    half = ROT_DIM // 2
    lane = lax.broadcasted_iota(jnp.int32, (acc.shape[0], MOBA_HEAD_DIM), 1)
    idx = jnp.where(lane < ROT_DIM, lane ^ half, lane)
    for hh in range(acc.shape[1] // MOBA_HEAD_DIM):
        c0 = hh * MOBA_HEAD_DIM
        x = acc[:, c0:c0 + MOBA_HEAD_DIM]
        partner = jnp.take_along_axis(x, idx, axis=1)
        o_ref[:, c0:c0 + MOBA_HEAD_DIM] = (x * c + partner * s).astype(o_ref.dtype)


def _in_proj_kernel(h_ref, w_ref, *rest, epilogue, n_tables, n_casts, fused_norm):
    rest = list(rest)
    norm_w_ref = rest.pop(0) if fused_norm else None
    tables = rest[:n_tables]
    cast_src = rest[n_tables:n_tables + n_casts]
    o_ref = rest[n_tables + n_casts]
    rest = rest[n_tables + n_casts + 1:]
    h_out_ref = rest.pop(0) if fused_norm else None
    cast_dst = rest[:n_casts]
    wb_ref = rest[-1]

    @pl.when(pl.program_id(1) == 0)
    def _():
        wb_ref[...] = w_ref[...].astype(wb_ref.dtype)

    if fused_norm:
        x = h_ref[...]
        r = lax.rsqrt(jnp.mean(x * x, axis=-1, keepdims=True) + EPS)
        h = (x * r * norm_w_ref[...]).astype(h_out_ref.dtype)
        h_out_ref[...] = h
    else:
        h = h_ref[...]
    acc = jnp.dot(h, wb_ref[...], preferred_element_type=F32)
    epilogue(acc, *tables, o_ref)
    for src_ref, dst_ref in zip(cast_src, cast_dst):
        dst_ref[...] = src_ref[...].astype(dst_ref.dtype)


def _cast_rows_per_step(n_rows, n_steps):
    rows = BF16_SUBLANES
    while n_rows % rows or n_rows // rows > n_steps:
        rows += BF16_SUBLANES
    return rows


def _in_proj(h, w, col_block, n_out, epilogue, tables, table_specs, tm, tn, name, casts=(), norm_w=None):
    t, d = h.shape
    n_i = t // tm
    n_steps = (n_out // tn) * n_i
    fused_norm = norm_w is not None
    assert not fused_norm or n_out == tn
    norm_args = [norm_w.reshape(1, d)] if fused_norm else []
    norm_specs = [pl.BlockSpec((1, d), lambda j, i: (0, 0))] if fused_norm else []
    h_out_specs = [pl.BlockSpec((tm, d), lambda j, i: (i, 0))] if fused_norm else []
    h_out_shapes = [jax.ShapeDtypeStruct((t, d), BF16)] if fused_norm else []
    cast_specs, cast_shapes = [], []
    for c in casts:
        rows = _cast_rows_per_step(c.shape[0], n_steps)
        last = c.shape[0] // rows - 1
        spec = pl.BlockSpec((rows, c.shape[1]), lambda j, i, last=last: (jnp.minimum(j * n_i + i, last), 0))
        cast_specs.append(spec)
        cast_shapes.append(jax.ShapeDtypeStruct(c.shape, BF16))
    outs = pl.pallas_call(
        functools.partial(_in_proj_kernel, epilogue=epilogue, n_tables=len(tables), n_casts=len(casts),
                          fused_norm=fused_norm),
        grid=(n_out // tn, n_i),
        in_specs=([pl.BlockSpec((tm, d), lambda j, i: (i, 0)),
                   pl.BlockSpec((d, tn), lambda j, i: (0, col_block(j)))]
                  + norm_specs + table_specs + cast_specs),
        out_specs=[pl.BlockSpec((tm, tn), lambda j, i: (i, j))] + h_out_specs + cast_specs,
        out_shape=[jax.ShapeDtypeStruct((t, n_out), BF16)] + h_out_shapes + cast_shapes,
        scratch_shapes=[pltpu.VMEM((d, tn), BF16)],
        compiler_params=_params(("arbitrary", "arbitrary"), 56),
        name=name,
    )(h, w, *norm_args, *tables, *casts)
    n_main = 1 + len(h_out_shapes)
    return outs[:n_main] if fused_norm else outs[0], outs[n_main:]


def _rope_angles(seq, rot_dim, theta):
    pos = np.arange(seq, dtype=np.float64)
    inv = np.float64(theta) ** (-np.arange(0, rot_dim, 2, dtype=np.float64) / rot_dim)
    return pos[:, None] * inv[None, :]


def _ret_rope_tables(seq):
    ang = _rope_angles(seq, RET_HEAD_DIM, RET_ROPE_THETA)
    return jnp.asarray(np.cos(ang), F32), jnp.asarray(np.sin(ang), F32)


def _moba_rope_tables(seq):
    half = ROT_DIM // 2
    ang = _rope_angles(seq, ROT_DIM, ROPE_THETA)
    cos, sin = np.cos(ang), np.sin(ang)
    c = np.ones((seq, MOBA_HEAD_DIM))
    s = np.zeros((seq, MOBA_HEAD_DIM))
    c[:, :half] = cos
    c[:, half:ROT_DIM] = cos
    s[:, :half] = -sin
    s[:, half:ROT_DIM] = sin
    q_scale = MOBA_HEAD_DIM ** -0.5 * math.log2(math.e)
    stack = lambda a: jnp.asarray(np.stack([a * q_scale, a]), F32)
    return stack(c), stack(s)


def _retention_kernel(q_ref, k_ref, v_ref, g_ref, nw_ref, o_ref, state_ref, dec_ref, xi_ref, zeta_ref):
    c = pl.program_id(1)
    tile = RET_TILE
    k_scale = RET_HEAD_DIM ** -0.5
    log_gs = [float(np.log1p(-np.exp2(-5.0 - hh))) for hh in range(RET_HEADS)]

    @pl.when(c == 0)
    def _():
        state_ref[...] = jnp.zeros_like(state_ref)
        row = lax.broadcasted_iota(jnp.int32, (tile, tile), 0).astype(F32)
        col = lax.broadcasted_iota(jnp.int32, (tile, tile), 1).astype(F32)
        diff = row - col
        causal = diff >= 0
        row_d = lax.broadcasted_iota(jnp.int32, (tile, RET_HEAD_DIM), 0).astype(F32)
        for hh, log_g in enumerate(log_gs):
            dec_ref[hh] = jnp.where(causal, jnp.exp(log_g * jnp.where(causal, diff, 0.0)), 0.0) * k_scale
            xi_ref[hh] = jnp.exp(log_g * (row_d + 1.0))
            zeta_ref[hh] = jnp.exp(log_g * (tile - 1.0 - row_d)) * k_scale

    def recurrence(rows):
        outs = []
        for hh, log_g in enumerate(log_gs):
            g_chunk = float(np.exp(log_g * tile))
            sl = slice(hh * RET_HEAD_DIM, (hh + 1) * RET_HEAD_DIM)
            q = q_ref[rows, sl]
            k = k_ref[rows, sl]
            v = v_ref[rows, sl]
            state = state_ref[hh]
            scores = lax.dot_general(q, k, NT_DIMS, preferred_element_type=F32) * dec_ref[hh]
            inner = jnp.dot(scores.astype(BF16), v, preferred_element_type=F32)
            cross = jnp.dot(q, state.astype(BF16), preferred_element_type=F32) * xi_ref[hh]
            kz = (k.astype(F32) * zeta_ref[hh]).astype(BF16)
            state_ref[hh] = state * g_chunk + lax.dot_general(kz, v, TN_DIMS, preferred_element_type=F32)
            outs.append(inner + cross)
        return outs

    def epilogue(rows, outs):
        for hh, o in enumerate(outs):
            sl = slice(hh * RET_HEAD_DIM, (hh + 1) * RET_HEAD_DIM)
            o = o * lax.rsqrt(jnp.mean(o * o, axis=-1, keepdims=True) + EPS)
            o_ref[rows, sl] = (_silu(g_ref[rows, sl].astype(F32)) * (o * nw_ref[:, sl])).astype(o_ref.dtype)

    pending = None
    for sub in range(q_ref.shape[0] // tile):
        rows = slice(sub * tile, (sub + 1) * tile)
        outs = recurrence(rows)
        if pending is not None:
            epilogue(*pending)
        pending = (rows, outs)
    epilogue(*pending)


def _plain_cols(d_model):
    return dict(g_ret=0, g_moba=d_model, rv=2 * d_model, rg=2 * d_model + RET_WIDTH,
                mv=2 * d_model + 2 * RET_WIDTH)


def _col_block(col, width):
    assert col % width == 0
    return col // width


def _retention(rq, rk, plain, norm_w, batch, seq, cols):
    t = rq.shape[0]
    blk = RET_STEP_ROWS
    n_blk = seq // blk
    row = lambda b, c: b * n_blk + c
    hd = RET_HEAD_DIM
    rv_blk = _col_block(cols["rv"], RET_WIDTH)
    rg_blk = _col_block(cols["rg"], RET_WIDTH)
    return pl.pallas_call(
        _retention_kernel,
        grid=(batch, n_blk),
        in_specs=[pl.BlockSpec((blk, RET_WIDTH), lambda b, c: (row(b, c), 0)),
                  pl.BlockSpec((blk, RET_WIDTH), lambda b, c: (row(b, c), 0)),
                  pl.BlockSpec((blk, RET_WIDTH), lambda b, c: (row(b, c), rv_blk)),
                  pl.BlockSpec((blk, RET_WIDTH), lambda b, c: (row(b, c), rg_blk)),
                  pl.BlockSpec((1, RET_WIDTH), lambda b, c: (0, 0))],
        out_specs=pl.BlockSpec((blk, RET_WIDTH), lambda b, c: (row(b, c), 0)),
        out_shape=jax.ShapeDtypeStruct((t, RET_WIDTH), BF16),
        scratch_shapes=[pltpu.VMEM((RET_HEADS, hd, hd), F32),
                        pltpu.VMEM((RET_HEADS, RET_TILE, RET_TILE), F32),
                        pltpu.VMEM((RET_HEADS, RET_TILE, hd), F32),
                        pltpu.VMEM((RET_HEADS, RET_TILE, hd), F32)],
        compiler_params=_params(("parallel", "arbitrary"), 56),
        name="retention",
    )(rq, rk, plain, plain, norm_w.reshape(1, RET_WIDTH))


def _moba_kernel(q_ref, k_ref, v_ref, blk_onehot_ref, o_ref, *scratch):
    seq = q_ref.shape[0]
    d = MOBA_HEAD_DIM
    bs = MOBA_BLOCK
    bs_shift = bs.bit_length() - 1
    assert bs == 1 << bs_shift
    nb = seq // bs
    blk = lax.broadcasted_iota(jnp.int32, (nb, seq), 0)
    own = lax.broadcasted_iota(jnp.int32, (nb, seq), 1) >> bs_shift
    past = blk < own
    qpos = lax.broadcasted_iota(jnp.int32, (bs, bs), 0)
    kpos = lax.broadcasted_iota(jnp.int32, (bs, bs), 1)

    def prepare(hd):
        qa_ref, ka_ref, va_ref = scratch[3 * hd:3 * hd + 3]
        cols = slice(hd * d, (hd + 1) * d)
        q = q_ref[:, cols]
        k = k_ref[:, cols]
        kmean = jnp.sum(k.astype(F32).reshape(nb, bs, d), axis=1) * (1.0 / bs)
        km_hi = kmean.astype(BF16)
        km_lo = (kmean - km_hi.astype(F32)).astype(BF16)
        score = (lax.dot_general(km_hi, q, NT_DIMS, preferred_element_type=F32)
                 + lax.dot_general(km_lo, q, NT_DIMS, preferred_element_type=F32))
        score = jnp.where(past, score, -jnp.inf)
        rank = jnp.zeros((nb, seq), jnp.int32)
        for r in range(1, nb):
            other = pltpu.roll(score, r, axis=0)
            beats = (other > score) | ((other == score) & (blk >= r))
            rank = rank + beats.astype(jnp.int32)
        keep = (past & (rank < MOBA_TOPK)) | (blk == own)
        pen_t = jnp.where(keep, 0.0, MASK_PENALTY)
        pen_t = jnp.concatenate([pen_t, jnp.zeros((LANES - nb, seq), F32)], axis=0)
        qa_ref[:, :d] = q
        qa_ref[:, d:] = pen_t.T.astype(BF16)
        ka_ref[:, :d] = k
        ka_ref[:, d:] = blk_onehot_ref[...]
        va_ref[:, :d] = v_ref[:, cols]
        va_ref[:, d:] = jnp.ones((seq, LANES), BF16)

    def attend(hd):
        qa_ref, ka_ref, va_ref = scratch[3 * hd:3 * hd + 3]

        def scores(qi):
            n_keys = (qi + 1) * bs
            return lax.dot_general(qa_ref[qi * bs:n_keys, :], ka_ref[0:n_keys, :], NT_DIMS,
                                   preferred_element_type=F32)

        order = list(range(nb - 1, -1, -1))
        s_next = scores(order[0])
        for pos, qi in enumerate(order):
            q0 = qi * bs
            n_keys = q0 + bs
            s = s_next
            if pos + 1 < nb:
                s_next = scores(order[pos + 1])
            s_own = jnp.where(kpos <= qpos, s[:, q0:], -jnp.inf)
            s = s_own if qi == 0 else jnp.concatenate([s[:, :q0], s_own], axis=1)
            p = jnp.exp2((s - jnp.max(s, axis=-1, keepdims=True)).astype(BF16))
            acc = jnp.dot(p, va_ref[0:n_keys, :], preferred_element_type=F32)
            o_ref[q0:n_keys, hd * d:(hd + 1) * d] = (acc[:, :d] / acc[:, d:]).astype(o_ref.dtype)
            yield

    n_heads = len(scratch) // 3
    prepare(0)
    for hd in range(n_heads):
        tiles = attend(hd)
        for pos in range(nb):
            next(tiles)
            if pos == 1 and hd + 1 < n_heads:
                prepare(hd + 1)


def _moba(mqk, plain, batch, seq, cols):
    t = mqk.shape[0]
    d = MOBA_HEAD_DIM
    hp = MOBA_HEADS_PER_STEP
    w = hp * d
    n_groups = MOBA_HEADS // hp
    v_blk0 = _col_block(cols["mv"], w)
    blk_onehot = jnp.asarray(np.arange(seq)[:, None] // MOBA_BLOCK == np.arange(LANES)[None, :], BF16)
    return pl.pallas_call(
        _moba_kernel,
        grid=(batch, n_groups),
        in_specs=[pl.BlockSpec((seq, w), lambda b, g: (b, g)),
                  pl.BlockSpec((seq, w), lambda b, g: (b, n_groups + g)),
                  pl.BlockSpec((seq, w), lambda b, g: (b, v_blk0 + g)),
                  pl.BlockSpec((seq, LANES), lambda b, g: (0, 0))],
        out_specs=pl.BlockSpec((seq, w), lambda b, g: (b, g)),
        out_shape=jax.ShapeDtypeStruct((t, MOBA_WIDTH), BF16),
        scratch_shapes=[pltpu.VMEM((seq, d + LANES), BF16)] * (3 * hp),
        compiler_params=_params(("parallel", "parallel"), 48),
        name="moba",
    )(mqk, mqk, plain, blk_onehot)


def _merge_kernel(yr_ref, ym_ref, gr_ref, gm_ref, x_ref, wr_ref, wm_ref, wo_ref, nw_ref, x1_ref, h2_ref):
    ret = jnp.dot(yr_ref[...], wr_ref[...], preferred_element_type=F32)
    mob = jnp.dot(ym_ref[...], wm_ref[...], preferred_element_type=F32)
    merged = _sigmoid(gr_ref[...].astype(F32)) * ret + _sigmoid(gm_ref[...].astype(F32)) * mob
    x1 = x_ref[...] + jnp.dot(merged.astype(BF16), wo_ref[...], preferred_element_type=F32)
    x1_ref[...] = x1
    r = lax.rsqrt(jnp.mean(x1 * x1, axis=-1, keepdims=True) + EPS)
    h2_ref[...] = (x1 * r * nw_ref[...]).astype(h2_ref.dtype)


def _merge(y_ret, y_moba, plain, x, w_ret_up, w_moba_up, w_out, norm_w, cols, tm=512):
    t, d = x.shape
    resident = lambda shape: pl.BlockSpec(shape, lambda i: (0, 0), pipeline_mode=pl.Buffered(1))
    g_ret_blk = _col_block(cols["g_ret"], d)
    g_moba_blk = _col_block(cols["g_moba"], d)
    return pl.pallas_call(
        _merge_kernel,
        grid=(t // tm,),
        in_specs=[pl.BlockSpec((tm, RET_WIDTH), lambda i: (i, 0)),
                  pl.BlockSpec((tm, MOBA_WIDTH), lambda i: (i, 0)),
                  pl.BlockSpec((tm, d), lambda i: (i, g_ret_blk)),
                  pl.BlockSpec((tm, d), lambda i: (i, g_moba_blk)),
                  pl.BlockSpec((tm, d), lambda i: (i, 0)),
                  resident(w_ret_up.shape), resident(w_moba_up.shape), resident(w_out.shape),
                  pl.BlockSpec((1, d), lambda i: (0, 0))],
        out_specs=[pl.BlockSpec((tm, d), lambda i: (i, 0)), pl.BlockSpec((tm, d), lambda i: (i, 0))],
        out_shape=[jax.ShapeDtypeStruct((t, d), F32), jax.ShapeDtypeStruct((t, d), BF16)],
        compiler_params=_params(("parallel",), 62),
        name="merge_outproj",
    )(y_ret, y_moba, plain, plain, x, w_ret_up, w_moba_up, w_out, norm_w.reshape(1, d))


def _ffn_up_kernel(h_ref, wa_ref, wb_ref, cwa_ref, cwb_ref, cba_ref, cbb_ref, o_ref, ua_ref, ub_ref,
                   wa16_ref, wb16_ref, *, tiles_per_seq):
    tm = h_ref.shape[0]
    pad = 8
    i = pl.program_id(1)

    @pl.when(i == 0)
    def _():
        wa16_ref[...] = wa_ref[...].astype(wa16_ref.dtype)
        wb16_ref[...] = wb_ref[...].astype(wb16_ref.dtype)

    @pl.when(i % tiles_per_seq == 0)
    def _():
        ua_ref[0:pad, :] = jnp.zeros((pad, ua_ref.shape[1]), F32)
        ub_ref[0:pad, :] = jnp.zeros((pad, ub_ref.shape[1]), F32)

    h = h_ref[...]
    ua_ref[pad:pad + tm, :] = jnp.dot(h, wa16_ref[...], preferred_element_type=F32)
    ub_ref[pad:pad + tm, :] = jnp.dot(h, wb16_ref[...], preferred_element_type=F32)

    def conv(u_ref, cw_ref, cb_ref):
        y = cb_ref[...]
        for j in range(CONV_WIDTH):
            off = pad - (CONV_WIDTH - 1) + j
            y = y + cw_ref[j:j + 1, :] * u_ref[off:off + tm, :]
        return y

    a = conv(ua_ref, cwa_ref, cba_ref)
    b = conv(ub_ref, cwb_ref, cbb_ref)
    o_ref[...] = (_silu(a) * b).astype(o_ref.dtype)
    ua_ref[0:pad, :] = ua_ref[tm:tm + pad, :]
    ub_ref[0:pad, :] = ub_ref[tm:tm + pad, :]


def _ffn_up(h2, w_up, conv_w, conv_b, seq, tm=1024, tn=512):
    t, d = h2.shape
    d_ff = w_up.shape[1] // 2
    nj = d_ff // tn
    conv_b = conv_b.reshape(1, 2 * d_ff)
    return pl.pallas_call(
        functools.partial(_ffn_up_kernel, tiles_per_seq=seq // tm),
        grid=(nj, t // tm),
        in_specs=[pl.BlockSpec((tm, d), lambda j, i: (i, 0)),
                  pl.BlockSpec((d, tn), lambda j, i: (0, j)),
                  pl.BlockSpec((d, tn), lambda j, i: (0, nj + j)),
                  pl.BlockSpec((CONV_WIDTH, tn), lambda j, i: (0, j)),
                  pl.BlockSpec((CONV_WIDTH, tn), lambda j, i: (0, nj + j)),
                  pl.BlockSpec((1, tn), lambda j, i: (0, j)),
                  pl.BlockSpec((1, tn), lambda j, i: (0, nj + j))],
        out_specs=pl.BlockSpec((tm, tn), lambda j, i: (i, j)),
        out_shape=jax.ShapeDtypeStruct((t, d_ff), BF16),
        scratch_shapes=[pltpu.VMEM((tm + 8, tn), F32), pltpu.VMEM((tm + 8, tn), F32),
                        pltpu.VMEM((d, tn), BF16), pltpu.VMEM((d, tn), BF16)],
        compiler_params=_params(("arbitrary", "arbitrary"), 52),
        name="ffn_up_conv_gate",
    )(h2, w_up, w_up, conv_w, conv_w, conv_b, conv_b)


def _ffn_down_kernel(a_ref, w_ref, x_ref, nw_ref, o_ref, *, final_norm):
    x2 = x_ref[...] + jnp.dot(a_ref[...], w_ref[...], preferred_element_type=F32)
    if final_norm:
        r = lax.rsqrt(jnp.mean(x2 * x2, axis=-1, keepdims=True) + EPS)
        x2 = x2 * r * nw_ref[...]
    o_ref[...] = x2


def _ffn_down(act, w_down, x1, norm_w, final_norm, tm=512):
    t, d = x1.shape
    d_ff = act.shape[1]
    return pl.pallas_call(
        functools.partial(_ffn_down_kernel, final_norm=final_norm),
        grid=(t // tm,),
        in_specs=[pl.BlockSpec((tm, d_ff), lambda i: (i, 0)),
                  pl.BlockSpec((d_ff, d), lambda i: (0, 0), pipeline_mode=pl.Buffered(1)),
                  pl.BlockSpec((tm, d), lambda i: (i, 0)),
                  pl.BlockSpec((1, d), lambda i: (0, 0))],
        out_specs=pl.BlockSpec((tm, d), lambda i: (i, 0)),
        out_shape=jax.ShapeDtypeStruct((t, d), F32),
        compiler_params=_params(("parallel",), 62),
        name="ffn_down",
    )(act, w_down, x1, norm_w.reshape(1, d))


def kernel(x, attn_norm_w, w_in, ret_norm_w, w_ret_up, w_moba_up, w_out, ffn_norm_w, w_ffn_up, conv_w,
           conv_b, w_ffn_down, final_norm_w):
    batch, seq, d = x.shape
    depth = w_in.shape[0]
    t = batch * seq
    xf = x.reshape(t, d)
    tm = 1024
    tm_norm = 1024
    tiles_per_seq = seq // tm

    ret_tables = _ret_rope_tables(seq)
    mob_tables = _moba_rope_tables(seq)

    def ret_specs(rows):
        return [pl.BlockSpec((rows, LANES), lambda j, i: (i % (seq // rows), 0))] * len(ret_tables)

    mob_specs = [pl.BlockSpec((None, tm, LANES), lambda j, i: (j, i % tiles_per_seq, 0))] * len(mob_tables)

    tn = 1024
    assert RET_WIDTH == tn and MOBA_WIDTH == tn and d % tn == 0
    w_in_col = dict(rv=2 * RET_WIDTH, rg=3 * RET_WIDTH, mv=4 * RET_WIDTH + 2 * MOBA_WIDTH,
                    g_ret=4 * RET_WIDTH + 3 * MOBA_WIDTH, g_moba=4 * RET_WIDTH + 3 * MOBA_WIDTH + d)
    cols = _plain_cols(d)
    widths = dict(g_ret=d, g_moba=d, rv=RET_WIDTH, rg=RET_WIDTH, mv=MOBA_WIDTH)
    src_tiles = [None] * (sum(widths.values()) // tn)
    for name, width in widths.items():
        for k in range(width // tn):
            src_tiles[_col_block(cols[name], tn) + k] = _col_block(w_in_col[name], tn) + k

    def plain_block(j):
        blk = src_tiles[-1]
        for out_tile in range(len(src_tiles) - 2, -1, -1):
            blk = jnp.where(j == out_tile, src_tiles[out_tile], blk)
        return blk

    for l in range(depth):
        (rq, h), (w_out16,) = _in_proj(xf, w_in[l], lambda j: j, tn, _ret_rotary_epilogue, ret_tables,
                                       ret_specs(tm_norm), tm_norm, tn, "in_proj_rq", casts=(w_out[l],),
                                       norm_w=attn_norm_w[l])
        rk, _ = _in_proj(h, w_in[l], lambda j: j + _col_block(RET_WIDTH, tn), tn, _ret_rotary_epilogue,
                         ret_tables, ret_specs(tm), tm, tn, "in_proj_rk")
        mob_qk, (w_ret_up16, w_moba_up16) = _in_proj(
            h, w_in[l], lambda j: j + _col_block(4 * RET_WIDTH, tn), 2 * tn, _moba_rotary_epilogue,
            mob_tables, mob_specs, tm, tn, "in_proj_moba", casts=(w_ret_up[l], w_moba_up[l]))
        plain, (w_down16,) = _in_proj(h, w_in[l], plain_block, len(src_tiles) * tn, _plain_epilogue, (), [],
                                      tm, tn, "in_proj_plain", casts=(w_ffn_down[l],))

        y_ret = _retention(rq, rk, plain, ret_norm_w[l], batch, seq, cols)
        y_moba = _moba(mob_qk, plain, batch, seq, cols)
        x1, h2 = _merge(y_ret, y_moba, plain, xf, w_ret_up16, w_moba_up16, w_out16, ffn_norm_w[l], cols)
        act = _ffn_up(h2, w_ffn_up[l], conv_w[l], conv_b[l], seq)
        last = l == depth - 1
        xf = _ffn_down(act, w_down16, x1, final_norm_w, final_norm=last)
    return xf.reshape(batch, seq, d)
```

```python
import functools
import math

import numpy as np
import jax
import jax.numpy as jnp
from jax import lax
from jax.experimental import pallas as pl
from jax.experimental.pallas import tpu as pltpu

RET_HEADS = 4
RET_HEAD_DIM = 256
RET_WIDTH = RET_HEADS * RET_HEAD_DIM
RET_ROPE_THETA = 10000.0
RET_TILE = 256
RET_STEP_ROWS = 2048
MOBA_HEADS = 8
MOBA_HEAD_DIM = 128
MOBA_WIDTH = MOBA_HEADS * MOBA_HEAD_DIM
MOBA_BLOCK = 256
MOBA_TOPK = 3
MOBA_HEADS_PER_STEP = 2
ROPE_THETA = 500000.0
ROT_DIM = MOBA_HEAD_DIM // 4
CONV_WIDTH = 3
EPS = 1e-6

LANES = 128
BF16_SUBLANES = 16
MASK_PENALTY = -1e30
BF16 = jnp.bfloat16
F32 = jnp.float32

NT_DIMS = (((1,), (1,)), ((), ()))
TN_DIMS = (((0,), (0,)), ((), ()))


def _params(semantics, vmem_mib):
    return pltpu.CompilerParams(dimension_semantics=semantics, vmem_limit_bytes=vmem_mib * 1024 * 1024)


def _sigmoid(x):
    return 0.5 * jnp.tanh(0.5 * x) + 0.5


def _silu(x):
    h = 0.5 * x
    return h + h * jnp.tanh(h)


def _plain_epilogue(acc, o_ref):
    o_ref[...] = acc.astype(o_ref.dtype)


def _ret_rotary_epilogue(acc, cos_ref, sin_ref, o_ref):
    cos = cos_ref[...]
    sin = sin_ref[...]
    half = RET_HEAD_DIM // 2
    for hh in range(acc.shape[1] // RET_HEAD_DIM):
        c0 = hh * RET_HEAD_DIM
        x1 = acc[:, c0:c0 + half]
        x2 = acc[:, c0 + half:c0 + RET_HEAD_DIM]
        o_ref[:, c0:c0 + half] = (x1 * cos - x2 * sin).astype(o_ref.dtype)
        o_ref[:, c0 + half:c0 + RET_HEAD_DIM] = (x2 * cos + x1 * sin).astype(o_ref.dtype)


def _moba_rotary_epilogue(acc, c_ref, s_ref, o_ref):
    c = c_ref[...]
    s = s_ref[...]
    half = ROT_DIM // 2
    lane = lax.broadcasted_iota(jnp.int32, (acc.shape[0], MOBA_HEAD_DIM), 1)
    idx = jnp.where(lane < ROT_DIM, lane ^ half, lane)
    for hh in range(acc.shape[1] // MOBA_HEAD_DIM):
        c0 = hh * MOBA_HEAD_DIM
        x = acc[:, c0:c0 + MOBA_HEAD_DIM]
        partner = jnp.take_along_axis(x, idx, axis=1)
        o_ref[:, c0:c0 + MOBA_HEAD_DIM] = (x * c + partner * s).astype(o_ref.dtype)


def _in_proj_kernel(h_ref, w_ref, *rest, epilogue, n_tables, n_casts, fused_norm):
    rest = list(rest)
    norm_w_ref = rest.pop(0) if fused_norm else None
    tables = rest[:n_tables]
    cast_src = rest[n_tables:n_tables + n_casts]
    o_ref = rest[n_tables + n_casts]
    rest = rest[n_tables + n_casts + 1:]
    h_out_ref = rest.pop(0) if fused_norm else None
    cast_dst = rest[:n_casts]
    wb_ref = rest[-1]

    @pl.when(pl.program_id(1) == 0)
    def _():
        wb_ref[...] = w_ref[...].astype(wb_ref.dtype)

    if fused_norm:
        x = h_ref[...]
        r = lax.rsqrt(jnp.mean(x * x, axis=-1, keepdims=True) + EPS)
        h = (x * r * norm_w_ref[...]).astype(h_out_ref.dtype)
        h_out_ref[...] = h
    else:
        h = h_ref[...]
    acc = jnp.dot(h, wb_ref[...], preferred_element_type=F32)
    epilogue(acc, *tables, o_ref)
    for src_ref, dst_ref in zip(cast_src, cast_dst):
        dst_ref[...] = src_ref[...].astype(dst_ref.dtype)


def _cast_rows_per_step(n_rows, n_steps):
    rows = BF16_SUBLANES
    while n_rows % rows or n_rows // rows > n_steps:
        rows += BF16_SUBLANES
    return rows


def _in_proj(h, w, col_block, n_out, epilogue, tables, table_specs, tm, tn, name, casts=(), norm_w=None):
    t, d = h.shape
    n_i = t // tm
    n_steps = (n_out // tn) * n_i
    fused_norm = norm_w is not None
    assert not fused_norm or n_out == tn
    norm_args = [norm_w.reshape(1, d)] if fused_norm else []
    norm_specs = [pl.BlockSpec((1, d), lambda j, i: (0, 0))] if fused_norm else []
    h_out_specs = [pl.BlockSpec((tm, d), lambda j, i: (i, 0))] if fused_norm else []
    h_out_shapes = [jax.ShapeDtypeStruct((t, d), BF16)] if fused_norm else []
    cast_specs, cast_shapes = [], []
    for c in casts:
        rows = _cast_rows_per_step(c.shape[0], n_steps)
        last = c.shape[0] // rows - 1
        spec = pl.BlockSpec((rows, c.shape[1]), lambda j, i, last=last: (jnp.minimum(j * n_i + i, last), 0))
        cast_specs.append(spec)
        cast_shapes.append(jax.ShapeDtypeStruct(c.shape, BF16))
    outs = pl.pallas_call(
        functools.partial(_in_proj_kernel, epilogue=epilogue, n_tables=len(tables), n_casts=len(casts),
                          fused_norm=fused_norm),
        grid=(n_out // tn, n_i),
        in_specs=([pl.BlockSpec((tm, d), lambda j, i: (i, 0)),
                   pl.BlockSpec((d, tn), lambda j, i: (0, col_block(j)))]
                  + norm_specs + table_specs + cast_specs),
        out_specs=[pl.BlockSpec((tm, tn), lambda j, i: (i, j))] + h_out_specs + cast_specs,
        out_shape=[jax.ShapeDtypeStruct((t, n_out), BF16)] + h_out_shapes + cast_shapes,
        scratch_shapes=[pltpu.VMEM((d, tn), BF16)],
        compiler_params=_params(("arbitrary", "arbitrary"), 56),
        name=name,
    )(h, w, *norm_args, *tables, *casts)
    n_main = 1 + len(h_out_shapes)
    return outs[:n_main] if fused_norm else outs[0], outs[n_main:]


def _rope_angles(seq, rot_dim, theta):
    pos = np.arange(seq, dtype=np.float64)
    inv = np.float64(theta) ** (-np.arange(0, rot_dim, 2, dtype=np.float64) / rot_dim)
    return pos[:, None] * inv[None, :]


def _ret_rope_tables(seq):
    ang = _rope_angles(seq, RET_HEAD_DIM, RET_ROPE_THETA)
    return jnp.asarray(np.cos(ang), F32), jnp.asarray(np.sin(ang), F32)


def _moba_rope_tables(seq):
    half = ROT_DIM // 2
    ang = _rope_angles(seq, ROT_DIM, ROPE_THETA)
    cos, sin = np.cos(ang), np.sin(ang)
    c = np.ones((seq, MOBA_HEAD_DIM))
    s = np.zeros((seq, MOBA_HEAD_DIM))
    c[:, :half] = cos
    c[:, half:ROT_DIM] = cos
    s[:, :half] = -sin
    s[:, half:ROT_DIM] = sin
    q_scale = MOBA_HEAD_DIM ** -0.5 * math.log2(math.e)
    stack = lambda a: jnp.asarray(np.stack([a * q_scale, a]), F32)
    return stack(c), stack(s)


def _retention_kernel(q_ref, k_ref, v_ref, g_ref, nw_ref, o_ref, state_ref, dec_ref, xi_ref, zeta_ref):
    c = pl.program_id(1)
    tile = RET_TILE
    k_scale = RET_HEAD_DIM ** -0.5
    log_gs = [float(np.log1p(-np.exp2(-5.0 - hh))) for hh in range(RET_HEADS)]

    @pl.when(c == 0)
    def _():
        state_ref[...] = jnp.zeros_like(state_ref)
        row = lax.broadcasted_iota(jnp.int32, (tile, tile), 0).astype(F32)
        col = lax.broadcasted_iota(jnp.int32, (tile, tile), 1).astype(F32)
        diff = row - col
        causal = diff >= 0
        row_d = lax.broadcasted_iota(jnp.int32, (tile, RET_HEAD_DIM), 0).astype(F32)
        for hh, log_g in enumerate(log_gs):
            dec_ref[hh] = jnp.where(causal, jnp.exp(log_g * jnp.where(causal, diff, 0.0)), 0.0) * k_scale
            xi_ref[hh] = jnp.exp(log_g * (row_d + 1.0))
            zeta_ref[hh] = jnp.exp(log_g * (tile - 1.0 - row_d)) * k_scale

    def recurrence(rows):
        outs = []
        for hh, log_g in enumerate(log_gs):
            g_chunk = float(np.exp(log_g * tile))
            sl = slice(hh * RET_HEAD_DIM, (hh + 1) * RET_HEAD_DIM)
            q = q_ref[rows, sl]
            k = k_ref[rows, sl]
            v = v_ref[rows, sl]
            state = state_ref[hh]
            scores = lax.dot_general(q, k, NT_DIMS, preferred_element_type=F32) * dec_ref[hh]
            inner = jnp.dot(scores.astype(BF16), v, preferred_element_type=F32)
            cross = jnp.dot(q, state.astype(BF16), preferred_element_type=F32) * xi_ref[hh]
            kz = (k.astype(F32) * zeta_ref[hh]).astype(BF16)
            state_ref[hh] = state * g_chunk + lax.dot_general(kz, v, TN_DIMS, preferred_element_type=F32)
            outs.append(inner + cross)
        return outs

    def epilogue(rows, outs):
        for hh, o in enumerate(outs):
            sl = slice(hh * RET_HEAD_DIM, (hh + 1) * RET_HEAD_DIM)
            o = o * lax.rsqrt(jnp.mean(o * o, axis=-1, keepdims=True) + EPS)
            o_ref[rows, sl] = (_silu(g_ref[rows, sl].astype(F32)) * (o * nw_ref[:, sl])).astype(o_ref.dtype)

    pending = None
    for sub in range(q_ref.shape[0] // tile):
        rows = slice(sub * tile, (sub + 1) * tile)
        outs = recurrence(rows)
        if pending is not None:
            epilogue(*pending)
        pending = (rows, outs)
    epilogue(*pending)


def _plain_cols(d_model):
    return dict(g_ret=0, g_moba=d_model, rv=2 * d_model, rg=2 * d_model + RET_WIDTH,
                mv=2 * d_model + 2 * RET_WIDTH)


def _col_block(col, width):
    assert col % width == 0
    return col // width


def _retention(rq, rk, plain, norm_w, batch, seq, cols):
    t = rq.shape[0]
    blk = RET_STEP_ROWS
    n_blk = seq // blk
    row = lambda b, c: b * n_blk + c
    hd = RET_HEAD_DIM
    rv_blk = _col_block(cols["rv"], RET_WIDTH)
    rg_blk = _col_block(cols["rg"], RET_WIDTH)
    return pl.pallas_call(
        _retention_kernel,
        grid=(batch, n_blk),
        in_specs=[pl.BlockSpec((blk, RET_WIDTH), lambda b, c: (row(b, c), 0)),
                  pl.BlockSpec((blk, RET_WIDTH), lambda b, c: (row(b, c), 0)),
                  pl.BlockSpec((blk, RET_WIDTH), lambda b, c: (row(b, c), rv_blk)),
                  pl.BlockSpec((blk, RET_WIDTH), lambda b, c: (row(b, c), rg_blk)),
                  pl.BlockSpec((1, RET_WIDTH), lambda b, c: (0, 0))],
        out_specs=pl.BlockSpec((blk, RET_WIDTH), lambda b, c: (row(b, c), 0)),
        out_shape=jax.ShapeDtypeStruct((t, RET_WIDTH), BF16),
        scratch_shapes=[pltpu.VMEM((RET_HEADS, hd, hd), F32),
                        pltpu.VMEM((RET_HEADS, RET_TILE, RET_TILE), F32),
                        pltpu.VMEM((RET_HEADS, RET_TILE, hd), F32),
                        pltpu.VMEM((RET_HEADS, RET_TILE, hd), F32)],
        compiler_params=_params(("parallel", "arbitrary"), 56),
        name="retention",
    )(rq, rk, plain, plain, norm_w.reshape(1, RET_WIDTH))


def _moba_kernel(q_ref, k_ref, v_ref, blk_onehot_ref, o_ref, *scratch):
    seq = q_ref.shape[0]
    d = MOBA_HEAD_DIM
    bs = MOBA_BLOCK
    bs_shift = bs.bit_length() - 1
    assert bs == 1 << bs_shift
    nb = seq // bs
    blk = lax.broadcasted_iota(jnp.int32, (nb, seq), 0)
    own = lax.broadcasted_iota(jnp.int32, (nb, seq), 1) >> bs_shift
    past = blk < own
    qpos = lax.broadcasted_iota(jnp.int32, (bs, bs), 0)
    kpos = lax.broadcasted_iota(jnp.int32, (bs, bs), 1)

    def prepare(hd):
        qa_ref, ka_ref, va_ref = scratch[3 * hd:3 * hd + 3]
        cols = slice(hd * d, (hd + 1) * d)
        q = q_ref[:, cols]
        k = k_ref[:, cols]
        kmean = jnp.sum(k.astype(F32).reshape(nb, bs, d), axis=1) * (1.0 / bs)
        km_hi = kmean.astype(BF16)
        km_lo = (kmean - km_hi.astype(F32)).astype(BF16)
        both = lax.dot_general(jnp.concatenate([km_hi, km_lo], axis=0), q, NT_DIMS,
                               preferred_element_type=F32)
        score = both[:nb] + both[nb:]
        score = jnp.where(past, score, -jnp.inf)
        rank = jnp.zeros((nb, seq), jnp.int32)
        for r in range(1, nb):
            other = pltpu.roll(score, r, axis=0)
            beats = (other > score) | ((other == score) & (blk >= r))
            rank = rank + beats.astype(jnp.int32)
        keep = (past & (rank < MOBA_TOPK)) | (blk == own)
        pen_t = jnp.where(keep, 0.0, MASK_PENALTY)
        pen_t = jnp.concatenate([pen_t, jnp.zeros((LANES - nb, seq), F32)], axis=0)
        qa_ref[:, :d] = q
        qa_ref[:, d:] = pen_t.T.astype(BF16)
        ka_ref[:, :d] = k
        ka_ref[:, d:] = blk_onehot_ref[...]
        va_ref[:, :d] = v_ref[:, cols]
        va_ref[:, d:] = jnp.ones((seq, LANES), BF16)

    def attend(hd):
        qa_ref, ka_ref, va_ref = scratch[3 * hd:3 * hd + 3]

        def scores(qi):
            n_keys = (qi + 1) * bs
            return lax.dot_general(qa_ref[qi * bs:n_keys, :], ka_ref[0:n_keys, :], NT_DIMS,
                                   preferred_element_type=F32)

        order = list(range(nb - 1, -1, -1))
        s_next = scores(order[0])
        for pos, qi in enumerate(order):
            q0 = qi * bs
            n_keys = q0 + bs
            s = s_next
            if pos + 1 < nb:
                s_next = scores(order[pos + 1])
            yield
            s_own = jnp.where(kpos <= qpos, s[:, q0:], -jnp.inf)
            s = s_own if qi == 0 else jnp.concatenate([s[:, :q0], s_own], axis=1)
            p = jnp.exp2((s - jnp.max(s, axis=-1, keepdims=True)).astype(BF16))
            acc = jnp.dot(p, va_ref[0:n_keys, :], preferred_element_type=F32)
            o_ref[q0:n_keys, hd * d:(hd + 1) * d] = (acc[:, :d] / acc[:, d:]).astype(o_ref.dtype)
            yield

    n_heads = len(scratch) // 3
    for hd in range(n_heads):
        prepare(hd)
    heads = [attend(hd) for hd in range(n_heads)]
    for _ in range(2 * nb):
        for tiles in heads:
            next(tiles)


def _moba(mqk, plain, batch, seq, cols):
    t = mqk.shape[0]
    d = MOBA_HEAD_DIM
    hp = MOBA_HEADS_PER_STEP
    w = hp * d
    n_groups = MOBA_HEADS // hp
    v_blk0 = _col_block(cols["mv"], w)
    blk_onehot = jnp.asarray(np.arange(seq)[:, None] // MOBA_BLOCK == np.arange(LANES)[None, :], BF16)
    return pl.pallas_call(
        _moba_kernel,
        grid=(batch, n_groups),
        in_specs=[pl.BlockSpec((seq, w), lambda b, g: (b, g)),
                  pl.BlockSpec((seq, w), lambda b, g: (b, n_groups + g)),
                  pl.BlockSpec((seq, w), lambda b, g: (b, v_blk0 + g)),
                  pl.BlockSpec((seq, LANES), lambda b, g: (0, 0))],
        out_specs=pl.BlockSpec((seq, w), lambda b, g: (b, g)),
        out_shape=jax.ShapeDtypeStruct((t, MOBA_WIDTH), BF16),
        scratch_shapes=[pltpu.VMEM((seq, d + LANES), BF16)] * (3 * hp),
        compiler_params=_params(("parallel", "parallel"), 48),
        name="moba",
    )(mqk, mqk, plain, blk_onehot)


def _merge_kernel(yr_ref, ym_ref, gr_ref, gm_ref, x_ref, wr_ref, wm_ref, wo_ref, nw_ref, x1_ref, h2_ref):
    ret = jnp.dot(yr_ref[...], wr_ref[...], preferred_element_type=F32)
    mob = jnp.dot(ym_ref[...], wm_ref[...], preferred_element_type=F32)
    merged = _sigmoid(gr_ref[...].astype(F32)) * ret + _sigmoid(gm_ref[...].astype(F32)) * mob
    x1 = x_ref[...] + jnp.dot(merged.astype(BF16), wo_ref[...], preferred_element_type=F32)
    x1_ref[...] = x1
    r = lax.rsqrt(jnp.mean(x1 * x1, axis=-1, keepdims=True) + EPS)
    h2_ref[...] = (x1 * r * nw_ref[...]).astype(h2_ref.dtype)


def _merge(y_ret, y_moba, plain, x, w_ret_up, w_moba_up, w_out, norm_w, cols, tm=512):
    t, d = x.shape
    resident = lambda shape: pl.BlockSpec(shape, lambda i: (0, 0), pipeline_mode=pl.Buffered(1))
    g_ret_blk = _col_block(cols["g_ret"], d)
    g_moba_blk = _col_block(cols["g_moba"], d)
    return pl.pallas_call(
        _merge_kernel,
        grid=(t // tm,),
        in_specs=[pl.BlockSpec((tm, RET_WIDTH), lambda i: (i, 0)),
                  pl.BlockSpec((tm, MOBA_WIDTH), lambda i: (i, 0)),
                  pl.BlockSpec((tm, d), lambda i: (i, g_ret_blk)),
                  pl.BlockSpec((tm, d), lambda i: (i, g_moba_blk)),
                  pl.BlockSpec((tm, d), lambda i: (i, 0)),
                  resident(w_ret_up.shape), resident(w_moba_up.shape), resident(w_out.shape),
                  pl.BlockSpec((1, d), lambda i: (0, 0))],
        out_specs=[pl.BlockSpec((tm, d), lambda i: (i, 0)), pl.BlockSpec((tm, d), lambda i: (i, 0))],
        out_shape=[jax.ShapeDtypeStruct((t, d), F32), jax.ShapeDtypeStruct((t, d), BF16)],
        compiler_params=_params(("parallel",), 62),
        name="merge_outproj",
    )(y_ret, y_moba, plain, plain, x, w_ret_up, w_moba_up, w_out, norm_w.reshape(1, d))


def _ffn_up_kernel(h_ref, wa_ref, wb_ref, cwa_ref, cwb_ref, cba_ref, cbb_ref, o_ref, ua_ref, ub_ref,
                   wa16_ref, wb16_ref, *, tiles_per_seq):
    tm = h_ref.shape[0]
    pad = 8
    i = pl.program_id(1)

    @pl.when(i == 0)
    def _():
        wa16_ref[...] = wa_ref[...].astype(wa16_ref.dtype)
        wb16_ref[...] = wb_ref[...].astype(wb16_ref.dtype)

    @pl.when(i % tiles_per_seq == 0)
    def _():
        ua_ref[0:pad, :] = jnp.zeros((pad, ua_ref.shape[1]), F32)
        ub_ref[0:pad, :] = jnp.zeros((pad, ub_ref.shape[1]), F32)

    h = h_ref[...]
    ua_ref[pad:pad + tm, :] = jnp.dot(h, wa16_ref[...], preferred_element_type=F32)
    ub_ref[pad:pad + tm, :] = jnp.dot(h, wb16_ref[...], preferred_element_type=F32)

    def conv(u_ref, cw_ref, cb_ref):
        y = cb_ref[...]
        for j in range(CONV_WIDTH):
            off = pad - (CONV_WIDTH - 1) + j
            y = y + cw_ref[j:j + 1, :] * u_ref[off:off + tm, :]
        return y

    a = conv(ua_ref, cwa_ref, cba_ref)
    b = conv(ub_ref, cwb_ref, cbb_ref)
    o_ref[...] = (_silu(a) * b).astype(o_ref.dtype)
    ua_ref[0:pad, :] = ua_ref[tm:tm + pad, :]
    ub_ref[0:pad, :] = ub_ref[tm:tm + pad, :]


def _ffn_up(h2, w_up, conv_w, conv_b, seq, tm=1024, tn=512):
    t, d = h2.shape
    d_ff = w_up.shape[1] // 2
    nj = d_ff // tn
    conv_b = conv_b.reshape(1, 2 * d_ff)
    return pl.pallas_call(
        functools.partial(_ffn_up_kernel, tiles_per_seq=seq // tm),
        grid=(nj, t // tm),
        in_specs=[pl.BlockSpec((tm, d), lambda j, i: (i, 0)),
                  pl.BlockSpec((d, tn), lambda j, i: (0, j)),
                  pl.BlockSpec((d, tn), lambda j, i: (0, nj + j)),
                  pl.BlockSpec((CONV_WIDTH, tn), lambda j, i: (0, j)),
                  pl.BlockSpec((CONV_WIDTH, tn), lambda j, i: (0, nj + j)),
                  pl.BlockSpec((1, tn), lambda j, i: (0, j)),
                  pl.BlockSpec((1, tn), lambda j, i: (0, nj + j))],
        out_specs=pl.BlockSpec((tm, tn), lambda j, i: (i, j)),
        out_shape=jax.ShapeDtypeStruct((t, d_ff), BF16),
        scratch_shapes=[pltpu.VMEM((tm + 8, tn), F32), pltpu.VMEM((tm + 8, tn), F32),
                        pltpu.VMEM((d, tn), BF16), pltpu.VMEM((d, tn), BF16)],
        compiler_params=_params(("arbitrary", "arbitrary"), 52),
        name="ffn_up_conv_gate",
    )(h2, w_up, w_up, conv_w, conv_w, conv_b, conv_b)


def _ffn_down_kernel(a_ref, w_ref, x_ref, nw_ref, o_ref, *, final_norm):
    x2 = x_ref[...] + jnp.dot(a_ref[...], w_ref[...], preferred_element_type=F32)
    if final_norm:
        r = lax.rsqrt(jnp.mean(x2 * x2, axis=-1, keepdims=True) + EPS)
        x2 = x2 * r * nw_ref[...]
    o_ref[...] = x2


def _ffn_down(act, w_down, x1, norm_w, final_norm, tm=512):
    t, d = x1.shape
    d_ff = act.shape[1]
    return pl.pallas_call(
        functools.partial(_ffn_down_kernel, final_norm=final_norm),
        grid=(t // tm,),
        in_specs=[pl.BlockSpec((tm, d_ff), lambda i: (i, 0)),
                  pl.BlockSpec((d_ff, d), lambda i: (0, 0), pipeline_mode=pl.Buffered(1)),
                  pl.BlockSpec((tm, d), lambda i: (i, 0)),
                  pl.BlockSpec((1, d), lambda i: (0, 0))],
        out_specs=pl.BlockSpec((tm, d), lambda i: (i, 0)),
        out_shape=jax.ShapeDtypeStruct((t, d), F32),
        compiler_params=_params(("parallel",), 62),
        name="ffn_down",
    )(act, w_down, x1, norm_w.reshape(1, d))


def kernel(x, attn_norm_w, w_in, ret_norm_w, w_ret_up, w_moba_up, w_out, ffn_norm_w, w_ffn_up, conv_w,
           conv_b, w_ffn_down, final_norm_w):
    batch, seq, d = x.shape
    depth = w_in.shape[0]
    t = batch * seq
    xf = x.reshape(t, d)
    tm = 1024
    tm_norm = 1024
    tiles_per_seq = seq // tm

    ret_tables = _ret_rope_tables(seq)
    mob_tables = _moba_rope_tables(seq)

    def ret_specs(rows):
        return [pl.BlockSpec((rows, LANES), lambda j, i: (i % (seq // rows), 0))] * len(ret_tables)

    mob_specs = [pl.BlockSpec((None, tm, LANES), lambda j, i: (j, i % tiles_per_seq, 0))] * len(mob_tables)

    tn = 1024
    assert RET_WIDTH == tn and MOBA_WIDTH == tn and d % tn == 0
    w_in_col = dict(rv=2 * RET_WIDTH, rg=3 * RET_WIDTH, mv=4 * RET_WIDTH + 2 * MOBA_WIDTH,
                    g_ret=4 * RET_WIDTH + 3 * MOBA_WIDTH, g_moba=4 * RET_WIDTH + 3 * MOBA_WIDTH + d)
    cols = _plain_cols(d)
    widths = dict(g_ret=d, g_moba=d, rv=RET_WIDTH, rg=RET_WIDTH, mv=MOBA_WIDTH)
    src_tiles = [None] * (sum(widths.values()) // tn)
    for name, width in widths.items():
        for k in range(width // tn):
            src_tiles[_col_block(cols[name], tn) + k] = _col_block(w_in_col[name], tn) + k

    def plain_block(j):
        blk = src_tiles[-1]
        for out_tile in range(len(src_tiles) - 2, -1, -1):
            blk = jnp.where(j == out_tile, src_tiles[out_tile], blk)
        return blk

    for l in range(depth):
        (rq, h), _ = _in_proj(xf, w_in[l], lambda j: j, tn, _ret_rotary_epilogue, ret_tables,
                              ret_specs(tm_norm), tm_norm, tn, "in_proj_rq", norm_w=attn_norm_w[l])
        rk, (w_out16,) = _in_proj(h, w_in[l], lambda j: j + _col_block(RET_WIDTH, tn), tn,
                                  _ret_rotary_epilogue, ret_tables, ret_specs(tm), tm, tn, "in_proj_rk",
                                  casts=(w_out[l],))
        mob_qk, (w_ret_up16, w_moba_up16) = _in_proj(
            h, w_in[l], lambda j: j + _col_block(4 * RET_WIDTH, tn), 2 * tn, _moba_rotary_epilogue,
            mob_tables, mob_specs, tm, tn, "in_proj_moba", casts=(w_ret_up[l], w_moba_up[l]))
        plain, (w_down16,) = _in_proj(h, w_in[l], plain_block, len(src_tiles) * tn, _plain_epilogue, (), [],
                                      tm, tn, "in_proj_plain", casts=(w_ffn_down[l],))

        y_ret = _retention(rq, rk, plain, ret_norm_w[l], batch, seq, cols)
        y_moba = _moba(mob_qk, plain, batch, seq, cols)
        x1, h2 = _merge(y_ret, y_moba, plain, xf, w_ret_up16, w_moba_up16, w_out16, ffn_norm_w[l], cols)
        act = _ffn_up(h2, w_ffn_up[l], conv_w[l], conv_b[l], seq)
        last = l == depth - 1
        xf = _ffn_down(act, w_down16, x1, final_norm_w, final_norm=last)
    return xf.reshape(batch, seq, d)
```

```python
import functools
import math

import numpy as np
import jax
import jax.numpy as jnp
from jax import lax
from jax.experimental import pallas as pl
from jax.experimental.pallas import tpu as pltpu

RET_HEADS = 4
RET_HEAD_DIM = 256
RET_WIDTH = RET_HEADS * RET_HEAD_DIM
RET_ROPE_THETA = 10000.0
RET_TILE = 256
RET_STEP_ROWS = 2048
MOBA_HEADS = 8
MOBA_HEAD_DIM = 128
MOBA_WIDTH = MOBA_HEADS * MOBA_HEAD_DIM
MOBA_BLOCK = 256
MOBA_TOPK = 3
MOBA_HEADS_PER_STEP = 2
ROPE_THETA = 500000.0
ROT_DIM = MOBA_HEAD_DIM // 4
CONV_WIDTH = 3
EPS = 1e-6

LANES = 128
BF16_SUBLANES = 16
MASK_PENALTY = -1e30
BF16 = jnp.bfloat16
F32 = jnp.float32

NT_DIMS = (((1,), (1,)), ((), ()))
TN_DIMS = (((0,), (0,)), ((), ()))


def _params(semantics, vmem_mib):
    return pltpu.CompilerParams(dimension_semantics=semantics, vmem_limit_bytes=vmem_mib * 1024 * 1024)


def _sigmoid(x):
    return 0.5 * jnp.tanh(0.5 * x) + 0.5


def _silu(x):
    h = 0.5 * x
    return h + h * jnp.tanh(h)


def _plain_epilogue(acc, o_ref):
    o_ref[...] = acc.astype(o_ref.dtype)


def _ret_rotary_epilogue(acc, cos_ref, sin_ref, o_ref):
    cos = cos_ref[...]
    sin = sin_ref[...]
    half = RET_HEAD_DIM // 2
    for hh in range(acc.shape[1] // RET_HEAD_DIM):
        c0 = hh * RET_HEAD_DIM
        x1 = acc[:, c0:c0 + half]
        x2 = acc[:, c0 + half:c0 + RET_HEAD_DIM]
        o_ref[:, c0:c0 + half] = (x1 * cos - x2 * sin).astype(o_ref.dtype)
        o_ref[:, c0 + half:c0 + RET_HEAD_DIM] = (x2 * cos + x1 * sin).astype(o_ref.dtype)


def _moba_rotary_epilogue(acc, c_ref, s_ref, o_ref):
    c = c_ref[...]
    s = s_ref[...]
    half = ROT_DIM // 2
    lane = lax.broadcasted_iota(jnp.int32, (acc.shape[0], MOBA_HEAD_DIM), 1)
    idx = jnp.where(lane < ROT_DIM, lane ^ half, lane)
    for hh in range(acc.shape[1] // MOBA_HEAD_DIM):
        c0 = hh * MOBA_HEAD_DIM
        x = acc[:, c0:c0 + MOBA_HEAD_DIM]
        partner = jnp.take_along_axis(x, idx, axis=1)
        o_ref[:, c0:c0 + MOBA_HEAD_DIM] = (x * c + partner * s).astype(o_ref.dtype)


def _in_proj_kernel(h_ref, w_ref, *rest, epilogues, kind_of, n_tables, n_casts, fused_norm):
    rest = list(rest)
    norm_w_ref = rest.pop(0) if fused_norm else None
    tables = rest[:n_tables]
    cast_src = rest[n_tables:n_tables + n_casts]
    o_ref = rest[n_tables + n_casts]
    rest = rest[n_tables + n_casts + 1:]
    h_out_ref = rest.pop(0) if fused_norm else None
    cast_dst = rest[:n_casts]
    wb_ref = rest[-1]

    @pl.when(pl.program_id(1) == 0)
    def _():
        wb_ref[...] = w_ref[...].astype(wb_ref.dtype)

    def rows():
        if not fused_norm:
            return h_ref[...]
        x = h_ref[...]
        r = lax.rsqrt(jnp.mean(x * x, axis=-1, keepdims=True) + EPS)
        h = (x * r * norm_w_ref[...]).astype(h_out_ref.dtype)
        h_out_ref[...] = h
        return h

    def project(fn, lo, hi):
        fn(jnp.dot(rows(), wb_ref[...], preferred_element_type=F32), *tables[lo:hi], o_ref)

    if len(epilogues) == 1:
        project(*epilogues[0])
    else:
        kind = kind_of(pl.program_id(0))
        for k, ep in enumerate(epilogues):
            pl.when(kind == k)(functools.partial(project, *ep))
    for src_ref, dst_ref in zip(cast_src, cast_dst):
        dst_ref[...] = src_ref[...].astype(dst_ref.dtype)


def _cast_rows_per_step(n_rows, n_steps):
    rows = BF16_SUBLANES
    while n_rows % rows or n_rows // rows > n_steps:
        rows += BF16_SUBLANES
    return rows


def _in_proj(h, w, col_block, n_out, epilogues, tables, table_specs, tm, tn, name, casts=(), norm_w=None,
             kind_of=None):
    t, d = h.shape
    bounds = np.cumsum([0] + [n for _, n in epilogues])
    assert bounds[-1] == len(tables) and (kind_of is not None or len(epilogues) == 1)
    epilogues = tuple((fn, int(lo), int(hi)) for (fn, _), lo, hi in zip(epilogues, bounds[:-1], bounds[1:]))
    n_i = t // tm
    n_steps = (n_out // tn) * n_i
    fused_norm = norm_w is not None
    assert not fused_norm or n_out == tn
    norm_args = [norm_w.reshape(1, d)] if fused_norm else []
    norm_specs = [pl.BlockSpec((1, d), lambda j, i: (0, 0))] if fused_norm else []
    h_out_specs = [pl.BlockSpec((tm, d), lambda j, i: (i, 0))] if fused_norm else []
    h_out_shapes = [jax.ShapeDtypeStruct((t, d), BF16)] if fused_norm else []
    cast_specs, cast_shapes = [], []
    for c in casts:
        rows = _cast_rows_per_step(c.shape[0], n_steps)
        last = c.shape[0] // rows - 1
        spec = pl.BlockSpec((rows, c.shape[1]), lambda j, i, last=last: (jnp.minimum(j * n_i + i, last), 0))
        cast_specs.append(spec)
        cast_shapes.append(jax.ShapeDtypeStruct(c.shape, BF16))
    outs = pl.pallas_call(
        functools.partial(_in_proj_kernel, epilogues=epilogues, kind_of=kind_of, n_tables=len(tables),
                          n_casts=len(casts), fused_norm=fused_norm),
        grid=(n_out // tn, n_i),
        in_specs=([pl.BlockSpec((tm, d), lambda j, i: (i, 0)),
                   pl.BlockSpec((d, tn), lambda j, i: (0, col_block(j)))]
                  + norm_specs + table_specs + cast_specs),
        out_specs=[pl.BlockSpec((tm, tn), lambda j, i: (i, j))] + h_out_specs + cast_specs,
        out_shape=[jax.ShapeDtypeStruct((t, n_out), BF16)] + h_out_shapes + cast_shapes,
        scratch_shapes=[pltpu.VMEM((d, tn), BF16)],
        compiler_params=_params(("arbitrary", "arbitrary"), 56),
        name=name,
    )(h, w, *norm_args, *tables, *casts)
    n_main = 1 + len(h_out_shapes)
    return outs[:n_main] if fused_norm else outs[0], outs[n_main:]


def _rope_angles(seq, rot_dim, theta):
    pos = np.arange(seq, dtype=np.float64)
    inv = np.float64(theta) ** (-np.arange(0, rot_dim, 2, dtype=np.float64) / rot_dim)
    return pos[:, None] * inv[None, :]


def _ret_rope_tables(seq):
    ang = _rope_angles(seq, RET_HEAD_DIM, RET_ROPE_THETA)
    return jnp.asarray(np.cos(ang), F32), jnp.asarray(np.sin(ang), F32)


def _moba_rope_tables(seq):
    half = ROT_DIM // 2
    ang = _rope_angles(seq, ROT_DIM, ROPE_THETA)
    cos, sin = np.cos(ang), np.sin(ang)
    c = np.ones((seq, MOBA_HEAD_DIM))
    s = np.zeros((seq, MOBA_HEAD_DIM))
    c[:, :half] = cos
    c[:, half:ROT_DIM] = cos
    s[:, :half] = -sin
    s[:, half:ROT_DIM] = sin
    q_scale = MOBA_HEAD_DIM ** -0.5 * math.log2(math.e)
    stack = lambda a: jnp.asarray(np.stack([a * q_scale, a]), F32)
    return stack(c), stack(s)


def _retention_kernel(q_ref, k_ref, v_ref, g_ref, nw_ref, o_ref, state_ref, dec_ref, xi_ref, zeta_ref):
    c = pl.program_id(1)
    tile = RET_TILE
    k_scale = RET_HEAD_DIM ** -0.5
    log_gs = [float(np.log1p(-np.exp2(-5.0 - hh))) for hh in range(RET_HEADS)]

    @pl.when(c == 0)
    def _():
        state_ref[...] = jnp.zeros_like(state_ref)
        row = lax.broadcasted_iota(jnp.int32, (tile, tile), 0).astype(F32)
        col = lax.broadcasted_iota(jnp.int32, (tile, tile), 1).astype(F32)
        diff = row - col
        causal = diff >= 0
        row_d = lax.broadcasted_iota(jnp.int32, (tile, RET_HEAD_DIM), 0).astype(F32)
        for hh, log_g in enumerate(log_gs):
            dec_ref[hh] = jnp.where(causal, jnp.exp(log_g * jnp.where(causal, diff, 0.0)), 0.0) * k_scale
            xi_ref[hh] = jnp.exp(log_g * (row_d + 1.0))
            zeta_ref[hh] = jnp.exp(log_g * (tile - 1.0 - row_d)) * k_scale

    def recurrence(rows):
        outs = []
        for hh, log_g in enumerate(log_gs):
            g_chunk = float(np.exp(log_g * tile))
            sl = slice(hh * RET_HEAD_DIM, (hh + 1) * RET_HEAD_DIM)
            q = q_ref[rows, sl]
            k = k_ref[rows, sl]
            v = v_ref[rows, sl]
            state = state_ref[hh]
            scores = lax.dot_general(q, k, NT_DIMS, preferred_element_type=F32) * dec_ref[hh]
            inner = jnp.dot(scores.astype(BF16), v, preferred_element_type=F32)
            cross = jnp.dot(q, state.astype(BF16), preferred_element_type=F32) * xi_ref[hh]
            kz = (k.astype(F32) * zeta_ref[hh]).astype(BF16)
            state_ref[hh] = state * g_chunk + lax.dot_general(kz, v, TN_DIMS, preferred_element_type=F32)
            outs.append(inner + cross)
        return outs

    def epilogue(rows, outs):
        for hh, o in enumerate(outs):
            sl = slice(hh * RET_HEAD_DIM, (hh + 1) * RET_HEAD_DIM)
            o = o * lax.rsqrt(jnp.mean(o * o, axis=-1, keepdims=True) + EPS)
            o_ref[rows, sl] = (_silu(g_ref[rows, sl].astype(F32)) * (o * nw_ref[:, sl])).astype(o_ref.dtype)

    pending = None
    for sub in range(q_ref.shape[0] // tile):
        rows = slice(sub * tile, (sub + 1) * tile)
        outs = recurrence(rows)
        if pending is not None:
            epilogue(*pending)
        pending = (rows, outs)
    epilogue(*pending)


PROJ_ORDER = ("g_ret", "g_moba", "rk", "mq", "mk", "rv", "rg", "mv")


def _proj_widths(d_model):
    return dict(g_ret=d_model, g_moba=d_model, rk=RET_WIDTH, mq=MOBA_WIDTH, mk=MOBA_WIDTH, rv=RET_WIDTH,
                rg=RET_WIDTH, mv=MOBA_WIDTH)


def _proj_cols(d_model):
    widths = _proj_widths(d_model)
    offsets = np.cumsum([0] + [widths[name] for name in PROJ_ORDER])
    return {name: int(off) for name, off in zip(PROJ_ORDER, offsets)}


def _col_block(col, width):
    assert col % width == 0
    return col // width


def _retention(rq, proj, norm_w, batch, seq, cols):
    t = rq.shape[0]
    blk = RET_STEP_ROWS
    n_blk = seq // blk
    row = lambda b, c: b * n_blk + c
    hd = RET_HEAD_DIM
    rk_blk = _col_block(cols["rk"], RET_WIDTH)
    rv_blk = _col_block(cols["rv"], RET_WIDTH)
    rg_blk = _col_block(cols["rg"], RET_WIDTH)
    return pl.pallas_call(
        _retention_kernel,
        grid=(batch, n_blk),
        in_specs=[pl.BlockSpec((blk, RET_WIDTH), lambda b, c: (row(b, c), 0)),
                  pl.BlockSpec((blk, RET_WIDTH), lambda b, c: (row(b, c), rk_blk)),
                  pl.BlockSpec((blk, RET_WIDTH), lambda b, c: (row(b, c), rv_blk)),
                  pl.BlockSpec((blk, RET_WIDTH), lambda b, c: (row(b, c), rg_blk)),
                  pl.BlockSpec((1, RET_WIDTH), lambda b, c: (0, 0))],
        out_specs=pl.BlockSpec((blk, RET_WIDTH), lambda b, c: (row(b, c), 0)),
        out_shape=jax.ShapeDtypeStruct((t, RET_WIDTH), BF16),
        scratch_shapes=[pltpu.VMEM((RET_HEADS, hd, hd), F32),
                        pltpu.VMEM((RET_HEADS, RET_TILE, RET_TILE), F32),
                        pltpu.VMEM((RET_HEADS, RET_TILE, hd), F32),
                        pltpu.VMEM((RET_HEADS, RET_TILE, hd), F32)],
        compiler_params=_params(("parallel", "arbitrary"), 56),
        name="retention",
    )(rq, proj, proj, proj, norm_w.reshape(1, RET_WIDTH))


def _moba_kernel(q_ref, k_ref, v_ref, blk_onehot_ref, o_ref, *scratch):
    seq = q_ref.shape[0]
    d = MOBA_HEAD_DIM
    bs = MOBA_BLOCK
    bs_shift = bs.bit_length() - 1
    assert bs == 1 << bs_shift
    nb = seq // bs
    blk = lax.broadcasted_iota(jnp.int32, (nb, seq), 0)
    own = lax.broadcasted_iota(jnp.int32, (nb, seq), 1) >> bs_shift
    past = blk < own
    qpos = lax.broadcasted_iota(jnp.int32, (bs, bs), 0)
    kpos = lax.broadcasted_iota(jnp.int32, (bs, bs), 1)

    def prepare(hd):
        qa_ref, ka_ref, va_ref = scratch[3 * hd:3 * hd + 3]
        cols = slice(hd * d, (hd + 1) * d)
        q = q_ref[:, cols]
        k = k_ref[:, cols]
        kmean = jnp.sum(k.astype(F32).reshape(nb, bs, d), axis=1) * (1.0 / bs)
        km_hi = kmean.astype(BF16)
        km_lo = (kmean - km_hi.astype(F32)).astype(BF16)
        both = lax.dot_general(jnp.concatenate([km_hi, km_lo], axis=0), q, NT_DIMS,
                               preferred_element_type=F32)
        score = both[:nb] + both[nb:]
        score = jnp.where(past, score, -jnp.inf)
        rank = jnp.zeros((nb, seq), jnp.int32)
        for r in range(1, nb):
            other = pltpu.roll(score, r, axis=0)
            beats = (other > score) | ((other == score) & (blk >= r))
            rank = rank + beats.astype(jnp.int32)
        keep = (past & (rank < MOBA_TOPK)) | (blk == own)
        pen_t = jnp.where(keep, 0.0, MASK_PENALTY)
        pen_t = jnp.concatenate([pen_t, jnp.zeros((LANES - nb, seq), F32)], axis=0)
        qa_ref[:, :d] = q
        qa_ref[:, d:] = pen_t.T.astype(BF16)
        ka_ref[:, :d] = k
        ka_ref[:, d:] = blk_onehot_ref[...]
        va_ref[:, :d] = v_ref[:, cols]
        va_ref[:, d:] = jnp.ones((seq, LANES), BF16)

    def attend(hd):
        qa_ref, ka_ref, va_ref = scratch[3 * hd:3 * hd + 3]

        def scores(qi):
            n_keys = (qi + 1) * bs
            return lax.dot_general(qa_ref[qi * bs:n_keys, :], ka_ref[0:n_keys, :], NT_DIMS,
                                   preferred_element_type=F32)

        order = list(range(nb - 1, -1, -1))
        s_next = scores(order[0])
        for pos, qi in enumerate(order):
            q0 = qi * bs
            n_keys = q0 + bs
            s = s_next
            if pos + 1 < nb:
                s_next = scores(order[pos + 1])
            yield
            s_own = jnp.where(kpos <= qpos, s[:, q0:], -jnp.inf)
            s = s_own if qi == 0 else jnp.concatenate([s[:, :q0], s_own], axis=1)
            p = jnp.exp2((s - jnp.max(s, axis=-1, keepdims=True)).astype(BF16))
            acc = jnp.dot(p, va_ref[0:n_keys, :], preferred_element_type=F32)
            o_ref[q0:n_keys, hd * d:(hd + 1) * d] = (acc[:, :d] / acc[:, d:]).astype(o_ref.dtype)
            yield

    n_heads = len(scratch) // 3
    for hd in range(n_heads):
        prepare(hd)
    heads = [attend(hd) for hd in range(n_heads)]
    for _ in range(2 * nb):
        for tiles in heads:
            next(tiles)


def _moba(proj, batch, seq, cols):
    t = proj.shape[0]
    d = MOBA_HEAD_DIM
    hp = MOBA_HEADS_PER_STEP
    w = hp * d
    n_groups = MOBA_HEADS // hp
    q_blk0 = _col_block(cols["mq"], w)
    k_blk0 = _col_block(cols["mk"], w)
    v_blk0 = _col_block(cols["mv"], w)
    blk_onehot = jnp.asarray(np.arange(seq)[:, None] // MOBA_BLOCK == np.arange(LANES)[None, :], BF16)
    return pl.pallas_call(
        _moba_kernel,
        grid=(batch, n_groups),
        in_specs=[pl.BlockSpec((seq, w), lambda b, g: (b, q_blk0 + g)),
                  pl.BlockSpec((seq, w), lambda b, g: (b, k_blk0 + g)),
                  pl.BlockSpec((seq, w), lambda b, g: (b, v_blk0 + g)),
                  pl.BlockSpec((seq, LANES), lambda b, g: (0, 0))],
        out_specs=pl.BlockSpec((seq, w), lambda b, g: (b, g)),
        out_shape=jax.ShapeDtypeStruct((t, MOBA_WIDTH), BF16),
        scratch_shapes=[pltpu.VMEM((seq, d + LANES), BF16)] * (3 * hp),
        compiler_params=_params(("parallel", "parallel"), 48),
        name="moba",
    )(proj, proj, proj, blk_onehot)


def _merge_kernel(yr_ref, ym_ref, gr_ref, gm_ref, x_ref, wr_ref, wm_ref, wo_ref, nw_ref, x1_ref, h2_ref):
    ret = jnp.dot(yr_ref[...], wr_ref[...], preferred_element_type=F32)
    mob = jnp.dot(ym_ref[...], wm_ref[...], preferred_element_type=F32)
    merged = _sigmoid(gr_ref[...].astype(F32)) * ret + _sigmoid(gm_ref[...].astype(F32)) * mob
    x1 = x_ref[...] + jnp.dot(merged.astype(BF16), wo_ref[...], preferred_element_type=F32)
    x1_ref[...] = x1
    r = lax.rsqrt(jnp.mean(x1 * x1, axis=-1, keepdims=True) + EPS)
    h2_ref[...] = (x1 * r * nw_ref[...]).astype(h2_ref.dtype)


def _merge(y_ret, y_moba, plain, x, w_ret_up, w_moba_up, w_out, norm_w, cols, tm=512):
    t, d = x.shape
    resident = lambda shape: pl.BlockSpec(shape, lambda i: (0, 0), pipeline_mode=pl.Buffered(1))
    g_ret_blk = _col_block(cols["g_ret"], d)
    g_moba_blk = _col_block(cols["g_moba"], d)
    return pl.pallas_call(
        _merge_kernel,
        grid=(t // tm,),
        in_specs=[pl.BlockSpec((tm, RET_WIDTH), lambda i: (i, 0)),
                  pl.BlockSpec((tm, MOBA_WIDTH), lambda i: (i, 0)),
                  pl.BlockSpec((tm, d), lambda i: (i, g_ret_blk)),
                  pl.BlockSpec((tm, d), lambda i: (i, g_moba_blk)),
                  pl.BlockSpec((tm, d), lambda i: (i, 0)),
                  resident(w_ret_up.shape), resident(w_moba_up.shape), resident(w_out.shape),
                  pl.BlockSpec((1, d), lambda i: (0, 0))],
        out_specs=[pl.BlockSpec((tm, d), lambda i: (i, 0)), pl.BlockSpec((tm, d), lambda i: (i, 0))],
        out_shape=[jax.ShapeDtypeStruct((t, d), F32), jax.ShapeDtypeStruct((t, d), BF16)],
        compiler_params=_params(("parallel",), 62),
        name="merge_outproj",
    )(y_ret, y_moba, plain, plain, x, w_ret_up, w_moba_up, w_out, norm_w.reshape(1, d))


def _ffn_up_kernel(h_ref, wa_ref, wb_ref, cwa_ref, cwb_ref, cba_ref, cbb_ref, o_ref, ua_ref, ub_ref,
                   wa16_ref, wb16_ref, *, tiles_per_seq):
    tm = h_ref.shape[0]
    pad = 8
    i = pl.program_id(1)

    @pl.when(i == 0)
    def _():
        wa16_ref[...] = wa_ref[...].astype(wa16_ref.dtype)
        wb16_ref[...] = wb_ref[...].astype(wb16_ref.dtype)

    @pl.when(i % tiles_per_seq == 0)
    def _():
        ua_ref[0:pad, :] = jnp.zeros((pad, ua_ref.shape[1]), F32)
        ub_ref[0:pad, :] = jnp.zeros((pad, ub_ref.shape[1]), F32)

    h = h_ref[...]
    ua_ref[pad:pad + tm, :] = jnp.dot(h, wa16_ref[...], preferred_element_type=F32)
    ub_ref[pad:pad + tm, :] = jnp.dot(h, wb16_ref[...], preferred_element_type=F32)

    def conv(u_ref, cw_ref, cb_ref):
        y = cb_ref[...]
        for j in range(CONV_WIDTH):
            off = pad - (CONV_WIDTH - 1) + j
            y = y + cw_ref[j:j + 1, :] * u_ref[off:off + tm, :]
        return y

    a = conv(ua_ref, cwa_ref, cba_ref)
    b = conv(ub_ref, cwb_ref, cbb_ref)
    o_ref[...] = (_silu(a) * b).astype(o_ref.dtype)
    ua_ref[0:pad, :] = ua_ref[tm:tm + pad, :]
    ub_ref[0:pad, :] = ub_ref[tm:tm + pad, :]


def _ffn_up(h2, w_up, conv_w, conv_b, seq, tm=1024, tn=512):
    t, d = h2.shape
    d_ff = w_up.shape[1] // 2
    nj = d_ff // tn
    conv_b = conv_b.reshape(1, 2 * d_ff)
    return pl.pallas_call(
        functools.partial(_ffn_up_kernel, tiles_per_seq=seq // tm),
        grid=(nj, t // tm),
        in_specs=[pl.BlockSpec((tm, d), lambda j, i: (i, 0)),
                  pl.BlockSpec((d, tn), lambda j, i: (0, j)),
                  pl.BlockSpec((d, tn), lambda j, i: (0, nj + j)),
                  pl.BlockSpec((CONV_WIDTH, tn), lambda j, i: (0, j)),
                  pl.BlockSpec((CONV_WIDTH, tn), lambda j, i: (0, nj + j)),
                  pl.BlockSpec((1, tn), lambda j, i: (0, j)),
                  pl.BlockSpec((1, tn), lambda j, i: (0, nj + j))],
        out_specs=pl.BlockSpec((tm, tn), lambda j, i: (i, j)),
        out_shape=jax.ShapeDtypeStruct((t, d_ff), BF16),
        scratch_shapes=[pltpu.VMEM((tm + 8, tn), F32), pltpu.VMEM((tm + 8, tn), F32),
                        pltpu.VMEM((d, tn), BF16), pltpu.VMEM((d, tn), BF16)],
        compiler_params=_params(("arbitrary", "arbitrary"), 52),
        name="ffn_up_conv_gate",
    )(h2, w_up, w_up, conv_w, conv_w, conv_b, conv_b)


def _ffn_down_kernel(a_ref, w_ref, x_ref, nw_ref, o_ref, *, final_norm):
    x2 = x_ref[...] + jnp.dot(a_ref[...], w_ref[...], preferred_element_type=F32)
    if final_norm:
        r = lax.rsqrt(jnp.mean(x2 * x2, axis=-1, keepdims=True) + EPS)
        x2 = x2 * r * nw_ref[...]
    o_ref[...] = x2


def _ffn_down(act, w_down, x1, norm_w, final_norm, tm=512):
    t, d = x1.shape
    d_ff = act.shape[1]
    return pl.pallas_call(
        functools.partial(_ffn_down_kernel, final_norm=final_norm),
        grid=(t // tm,),
        in_specs=[pl.BlockSpec((tm, d_ff), lambda i: (i, 0)),
                  pl.BlockSpec((d_ff, d), lambda i: (0, 0), pipeline_mode=pl.Buffered(1)),
                  pl.BlockSpec((tm, d), lambda i: (i, 0)),
                  pl.BlockSpec((1, d), lambda i: (0, 0))],
        out_specs=pl.BlockSpec((tm, d), lambda i: (i, 0)),
        out_shape=jax.ShapeDtypeStruct((t, d), F32),
        compiler_params=_params(("parallel",), 62),
        name="ffn_down",
    )(act, w_down, x1, norm_w.reshape(1, d))


def kernel(x, attn_norm_w, w_in, ret_norm_w, w_ret_up, w_moba_up, w_out, ffn_norm_w, w_ffn_up, conv_w,
           conv_b, w_ffn_down, final_norm_w):
    batch, seq, d = x.shape
    depth = w_in.shape[0]
    t = batch * seq
    xf = x.reshape(t, d)
    tm = 1024
    tm_norm = 1024
    tiles_per_seq = seq // tm

    ret_tables = _ret_rope_tables(seq)
    mob_tables = _moba_rope_tables(seq)

    def ret_specs(rows):
        return [pl.BlockSpec((rows, LANES), lambda j, i: (i % (seq // rows), 0))] * len(ret_tables)

    tn = 1024
    assert RET_WIDTH == tn and MOBA_WIDTH == tn and d % tn == 0
    w_in_col = dict(rk=RET_WIDTH, rv=2 * RET_WIDTH, rg=3 * RET_WIDTH, mq=4 * RET_WIDTH,
                    mk=4 * RET_WIDTH + MOBA_WIDTH, mv=4 * RET_WIDTH + 2 * MOBA_WIDTH,
                    g_ret=4 * RET_WIDTH + 3 * MOBA_WIDTH, g_moba=4 * RET_WIDTH + 3 * MOBA_WIDTH + d)
    cols = _proj_cols(d)
    widths = _proj_widths(d)
    kinds = dict(rk=1, mq=2, mk=2)
    src_tiles, tile_kinds, rope_index = [], [], []
    for name in PROJ_ORDER:
        for k in range(widths[name] // tn):
            src_tiles.append(_col_block(w_in_col[name], tn) + k)
            tile_kinds.append(kinds.get(name, 0))
            rope_index.append(1 if name == "mk" else 0)

    def lookup(values):
        def pick(j):
            out = values[-1]
            for tile in range(len(values) - 2, -1, -1):
                out = jnp.where(j == tile, values[tile], out)
            return out
        return pick

    mob_specs = [pl.BlockSpec((None, tm, LANES), lambda j, i: (lookup(rope_index)(j), i % tiles_per_seq, 0))
                 ] * len(mob_tables)

    for l in range(depth):
        (rq, h), (w_out16,) = _in_proj(xf, w_in[l], lambda j: j, tn, [(_ret_rotary_epilogue, len(ret_tables))],
                                       ret_tables, ret_specs(tm_norm), tm_norm, tn, "in_proj_rq",
                                       casts=(w_out[l],), norm_w=attn_norm_w[l])
        proj, (w_ret_up16, w_moba_up16, w_down16) = _in_proj(
            h, w_in[l], lookup(src_tiles), len(src_tiles) * tn,
            [(_plain_epilogue, 0), (_ret_rotary_epilogue, len(ret_tables)),
             (_moba_rotary_epilogue, len(mob_tables))],
            (*ret_tables, *mob_tables), ret_specs(tm) + mob_specs, tm, tn, "in_proj_h",
            casts=(w_ret_up[l], w_moba_up[l], w_ffn_down[l]), kind_of=lookup(tile_kinds))

        y_ret = _retention(rq, proj, ret_norm_w[l], batch, seq, cols)
        y_moba = _moba(proj, batch, seq, cols)
        x1, h2 = _merge(y_ret, y_moba, proj, xf, w_ret_up16, w_moba_up16, w_out16, ffn_norm_w[l], cols)
        act = _ffn_up(h2, w_ffn_up[l], conv_w[l], conv_b[l], seq)
        last = l == depth - 1
        xf = _ffn_down(act, w_down16, x1, final_norm_w, final_norm=last)
    return xf.reshape(batch, seq, d)
```

```python
import functools
import math

import numpy as np
import jax
import jax.numpy as jnp
from jax import lax
from jax.experimental import pallas as pl
from jax.experimental.pallas import tpu as pltpu

RET_HEADS = 4
RET_HEAD_DIM = 256
RET_WIDTH = RET_HEADS * RET_HEAD_DIM
RET_ROPE_THETA = 10000.0
RET_TILE = 256
RET_STEP_ROWS = 2048
MOBA_HEADS = 8
MOBA_HEAD_DIM = 128
MOBA_WIDTH = MOBA_HEADS * MOBA_HEAD_DIM
MOBA_BLOCK = 256
MOBA_TOPK = 3
MOBA_HEADS_PER_STEP = 2
ROPE_THETA = 500000.0
ROT_DIM = MOBA_HEAD_DIM // 4
CONV_WIDTH = 3
EPS = 1e-6

LANES = 128
BF16_SUBLANES = 16
MASK_PENALTY = -1e30
BF16 = jnp.bfloat16
F32 = jnp.float32

NT_DIMS = (((1,), (1,)), ((), ()))
TN_DIMS = (((0,), (0,)), ((), ()))


def _params(semantics, vmem_mib):
    return pltpu.CompilerParams(dimension_semantics=semantics, vmem_limit_bytes=vmem_mib * 1024 * 1024)


def _sigmoid(x):
    return 0.5 * jnp.tanh(0.5 * x) + 0.5


def _silu(x):
    h = 0.5 * x
    return h + h * jnp.tanh(h)


def _plain_epilogue(acc, o_ref):
    o_ref[...] = acc.astype(o_ref.dtype)


def _ret_rotary_epilogue(acc, cos_ref, sin_ref, o_ref):
    cos = cos_ref[...]
    sin = sin_ref[...]
    half = RET_HEAD_DIM // 2
    for hh in range(acc.shape[1] // RET_HEAD_DIM):
        c0 = hh * RET_HEAD_DIM
        x1 = acc[:, c0:c0 + half]
        x2 = acc[:, c0 + half:c0 + RET_HEAD_DIM]
        o_ref[:, c0:c0 + half] = (x1 * cos - x2 * sin).astype(o_ref.dtype)
        o_ref[:, c0 + half:c0 + RET_HEAD_DIM] = (x2 * cos + x1 * sin).astype(o_ref.dtype)


def _moba_rotary_epilogue(acc, c_ref, s_ref, o_ref):
    c = c_ref[...]
    s = s_ref[...]
    half = ROT_DIM // 2
    lane = lax.broadcasted_iota(jnp.int32, (acc.shape[0], MOBA_HEAD_DIM), 1)
    idx = jnp.where(lane < ROT_DIM, lane ^ half, lane)
    for hh in range(acc.shape[1] // MOBA_HEAD_DIM):
        c0 = hh * MOBA_HEAD_DIM
        x = acc[:, c0:c0 + MOBA_HEAD_DIM]
        partner = jnp.take_along_axis(x, idx, axis=1)
        o_ref[:, c0:c0 + MOBA_HEAD_DIM] = (x * c + partner * s).astype(o_ref.dtype)


def _in_proj_kernel(h_ref, w_ref, *rest, epilogue, n_tables, n_casts, fused_norm):
    rest = list(rest)
    norm_w_ref = rest.pop(0) if fused_norm else None
    tables = rest[:n_tables]
    cast_src = rest[n_tables:n_tables + n_casts]
    o_ref = rest[n_tables + n_casts]
    rest = rest[n_tables + n_casts + 1:]
    h_out_ref = rest.pop(0) if fused_norm else None
    cast_dst = rest[:n_casts]
    wb_ref = rest[-1]

    @pl.when(pl.program_id(1) == 0)
    def _():
        wb_ref[...] = w_ref[...].astype(wb_ref.dtype)

    if fused_norm:
        x = h_ref[...]
        r = lax.rsqrt(jnp.mean(x * x, axis=-1, keepdims=True) + EPS)
        h = (x * r * norm_w_ref[...]).astype(h_out_ref.dtype)
        h_out_ref[...] = h
    else:
        h = h_ref[...]
    acc = jnp.dot(h, wb_ref[...], preferred_element_type=F32)
    epilogue(acc, *tables, o_ref)
    for src_ref, dst_ref in zip(cast_src, cast_dst):
        dst_ref[...] = src_ref[...].astype(dst_ref.dtype)


def _cast_rows_per_step(n_rows, n_steps):
    rows = BF16_SUBLANES
    while n_rows % rows or n_rows // rows > n_steps:
        rows += BF16_SUBLANES
    return rows


def _in_proj(h, w, col_block, n_out, epilogue, tables, table_specs, tm, tn, name, casts=(), norm_w=None):
    t, d = h.shape
    n_i = t // tm
    n_steps = (n_out // tn) * n_i
    fused_norm = norm_w is not None
    assert not fused_norm or n_out == tn
    norm_args = [norm_w.reshape(1, d)] if fused_norm else []
    norm_specs = [pl.BlockSpec((1, d), lambda j, i: (0, 0))] if fused_norm else []
    h_out_specs = [pl.BlockSpec((tm, d), lambda j, i: (i, 0))] if fused_norm else []
    h_out_shapes = [jax.ShapeDtypeStruct((t, d), BF16)] if fused_norm else []
    cast_specs, cast_shapes = [], []
    for c in casts:
        rows = _cast_rows_per_step(c.shape[0], n_steps)
        last = c.shape[0] // rows - 1
        spec = pl.BlockSpec((rows, c.shape[1]), lambda j, i, last=last: (jnp.minimum(j * n_i + i, last), 0))
        cast_specs.append(spec)
        cast_shapes.append(jax.ShapeDtypeStruct(c.shape, BF16))
    outs = pl.pallas_call(
        functools.partial(_in_proj_kernel, epilogue=epilogue, n_tables=len(tables), n_casts=len(casts),
                          fused_norm=fused_norm),
        grid=(n_out // tn, n_i),
        in_specs=([pl.BlockSpec((tm, d), lambda j, i: (i, 0)),
                   pl.BlockSpec((d, tn), lambda j, i: (0, col_block(j)))]
                  + norm_specs + table_specs + cast_specs),
        out_specs=[pl.BlockSpec((tm, tn), lambda j, i: (i, j))] + h_out_specs + cast_specs,
        out_shape=[jax.ShapeDtypeStruct((t, n_out), BF16)] + h_out_shapes + cast_shapes,
        scratch_shapes=[pltpu.VMEM((d, tn), BF16)],
        compiler_params=_params(("arbitrary", "arbitrary"), 56),
        name=name,
    )(h, w, *norm_args, *tables, *casts)
    n_main = 1 + len(h_out_shapes)
    return outs[:n_main] if fused_norm else outs[0], outs[n_main:]


def _rope_angles(seq, rot_dim, theta):
    pos = np.arange(seq, dtype=np.float64)
    inv = np.float64(theta) ** (-np.arange(0, rot_dim, 2, dtype=np.float64) / rot_dim)
    return pos[:, None] * inv[None, :]


def _ret_rope_tables(seq):
    ang = _rope_angles(seq, RET_HEAD_DIM, RET_ROPE_THETA)
    return jnp.asarray(np.cos(ang), F32), jnp.asarray(np.sin(ang), F32)


def _moba_rope_tables(seq):
    half = ROT_DIM // 2
    ang = _rope_angles(seq, ROT_DIM, ROPE_THETA)
    cos, sin = np.cos(ang), np.sin(ang)
    c = np.ones((seq, MOBA_HEAD_DIM))
    s = np.zeros((seq, MOBA_HEAD_DIM))
    c[:, :half] = cos
    c[:, half:ROT_DIM] = cos
    s[:, :half] = -sin
    s[:, half:ROT_DIM] = sin
    q_scale = MOBA_HEAD_DIM ** -0.5 * math.log2(math.e)
    stack = lambda a: jnp.asarray(np.stack([a * q_scale, a]), F32)
    return stack(c), stack(s)


def _retention_kernel(q_ref, k_ref, v_ref, g_ref, nw_ref, o_ref, state_ref, dec_ref, xi_ref, zeta_ref):
    c = pl.program_id(1)
    tile = RET_TILE
    k_scale = RET_HEAD_DIM ** -0.5
    log_gs = [float(np.log1p(-np.exp2(-5.0 - hh))) for hh in range(RET_HEADS)]

    @pl.when(c == 0)
    def _():
        state_ref[...] = jnp.zeros_like(state_ref)
        row = lax.broadcasted_iota(jnp.int32, (tile, tile), 0).astype(F32)
        col = lax.broadcasted_iota(jnp.int32, (tile, tile), 1).astype(F32)
        diff = row - col
        causal = diff >= 0
        row_d = lax.broadcasted_iota(jnp.int32, (tile, RET_HEAD_DIM), 0).astype(F32)
        for hh, log_g in enumerate(log_gs):
            dec_ref[hh] = jnp.where(causal, jnp.exp(log_g * jnp.where(causal, diff, 0.0)), 0.0) * k_scale
            xi_ref[hh] = jnp.exp(log_g * (row_d + 1.0))
            zeta_ref[hh] = jnp.exp(log_g * (tile - 1.0 - row_d)) * k_scale

    def recurrence(rows):
        outs = []
        for hh, log_g in enumerate(log_gs):
            g_chunk = float(np.exp(log_g * tile))
            sl = slice(hh * RET_HEAD_DIM, (hh + 1) * RET_HEAD_DIM)
            q = q_ref[rows, sl]
            k = k_ref[rows, sl]
            v = v_ref[rows, sl]
            state = state_ref[hh]
            scores = lax.dot_general(q, k, NT_DIMS, preferred_element_type=F32) * dec_ref[hh]
            inner = jnp.dot(scores.astype(BF16), v, preferred_element_type=F32)
            cross = jnp.dot(q, state.astype(BF16), preferred_element_type=F32) * xi_ref[hh]
            kz = (k.astype(F32) * zeta_ref[hh]).astype(BF16)
            state_ref[hh] = state * g_chunk + lax.dot_general(kz, v, TN_DIMS, preferred_element_type=F32)
            outs.append(inner + cross)
        return outs

    def epilogue(rows, outs):
        for hh, o in enumerate(outs):
            sl = slice(hh * RET_HEAD_DIM, (hh + 1) * RET_HEAD_DIM)
            o = o * lax.rsqrt(jnp.mean(o * o, axis=-1, keepdims=True) + EPS)
            o_ref[rows, sl] = (_silu(g_ref[rows, sl].astype(F32)) * (o * nw_ref[:, sl])).astype(o_ref.dtype)

    pending = None
    for sub in range(q_ref.shape[0] // tile):
        rows = slice(sub * tile, (sub + 1) * tile)
        outs = recurrence(rows)
        if pending is not None:
            epilogue(*pending)
        pending = (rows, outs)
    epilogue(*pending)


def _plain_cols(d_model):
    return dict(g_ret=0, g_moba=d_model, rv=2 * d_model, rg=2 * d_model + RET_WIDTH,
                mv=2 * d_model + 2 * RET_WIDTH)


def _col_block(col, width):
    assert col % width == 0
    return col // width


def _retention(rq, rk, plain, norm_w, batch, seq, cols):
    t = rq.shape[0]
    blk = RET_STEP_ROWS
    n_blk = seq // blk
    row = lambda b, c: b * n_blk + c
    hd = RET_HEAD_DIM
    rv_blk = _col_block(cols["rv"], RET_WIDTH)
    rg_blk = _col_block(cols["rg"], RET_WIDTH)
    return pl.pallas_call(
        _retention_kernel,
        grid=(batch, n_blk),
        in_specs=[pl.BlockSpec((blk, RET_WIDTH), lambda b, c: (row(b, c), 0)),
                  pl.BlockSpec((blk, RET_WIDTH), lambda b, c: (row(b, c), 0)),
                  pl.BlockSpec((blk, RET_WIDTH), lambda b, c: (row(b, c), rv_blk)),
                  pl.BlockSpec((blk, RET_WIDTH), lambda b, c: (row(b, c), rg_blk)),
                  pl.BlockSpec((1, RET_WIDTH), lambda b, c: (0, 0))],
        out_specs=pl.BlockSpec((blk, RET_WIDTH), lambda b, c: (row(b, c), 0)),
        out_shape=jax.ShapeDtypeStruct((t, RET_WIDTH), BF16),
        scratch_shapes=[pltpu.VMEM((RET_HEADS, hd, hd), F32),
                        pltpu.VMEM((RET_HEADS, RET_TILE, RET_TILE), F32),
                        pltpu.VMEM((RET_HEADS, RET_TILE, hd), F32),
                        pltpu.VMEM((RET_HEADS, RET_TILE, hd), F32)],
        compiler_params=_params(("parallel", "arbitrary"), 56),
        name="retention",
    )(rq, rk, plain, plain, norm_w.reshape(1, RET_WIDTH))


def _moba_kernel(q_ref, k_ref, v_ref, blk_onehot_ref, o_ref, *scratch):
    seq = q_ref.shape[0]
    d = MOBA_HEAD_DIM
    bs = MOBA_BLOCK
    bs_shift = bs.bit_length() - 1
    assert bs == 1 << bs_shift
    nb = seq // bs
    blk = lax.broadcasted_iota(jnp.int32, (nb, seq), 0)
    own = lax.broadcasted_iota(jnp.int32, (nb, seq), 1) >> bs_shift
    past = blk < own
    qpos = lax.broadcasted_iota(jnp.int32, (bs, bs), 0)
    kpos = lax.broadcasted_iota(jnp.int32, (bs, bs), 1)

    def prepare(hd):
        qa_ref, ka_ref, va_ref = scratch[3 * hd:3 * hd + 3]
        cols = slice(hd * d, (hd + 1) * d)
        q = q_ref[:, cols]
        k = k_ref[:, cols]
        kmean = jnp.sum(k.astype(F32).reshape(nb, bs, d), axis=1) * (1.0 / bs)
        km_hi = kmean.astype(BF16)
        km_lo = (kmean - km_hi.astype(F32)).astype(BF16)
        both = lax.dot_general(jnp.concatenate([km_hi, km_lo], axis=0), q, NT_DIMS,
                               preferred_element_type=F32)
        score = both[:nb] + both[nb:]
        score = jnp.where(past, score, -jnp.inf)
        rank = jnp.zeros((nb, seq), jnp.int32)
        for r in range(1, nb):
            other = pltpu.roll(score, r, axis=0)
            beats = (other > score) | ((other == score) & (blk >= r))
            rank = rank + beats.astype(jnp.int32)
        keep = (past & (rank < MOBA_TOPK)) | (blk == own)
        pen_t = jnp.where(keep, 0.0, MASK_PENALTY)
        pen_t = jnp.concatenate([pen_t, jnp.zeros((LANES - nb, seq), F32)], axis=0)
        qa_ref[:, :d] = q
        qa_ref[:, d:] = pen_t.T.astype(BF16)
        ka_ref[:, :d] = k
        ka_ref[:, d:] = blk_onehot_ref[...]
        va_ref[:, :d] = v_ref[:, cols]
        va_ref[:, d:] = jnp.ones((seq, LANES), BF16)

    def attend(hd):
        qa_ref, ka_ref, va_ref = scratch[3 * hd:3 * hd + 3]

        def scores(qi):
            n_keys = (qi + 1) * bs
            return lax.dot_general(qa_ref[qi * bs:n_keys, :], ka_ref[0:n_keys, :], NT_DIMS,
                                   preferred_element_type=F32)

        order = list(range(nb - 1, -1, -1))
        s_next = scores(order[0])
        for pos, qi in enumerate(order):
            q0 = qi * bs
            n_keys = q0 + bs
            s = s_next
            if pos + 1 < nb:
                s_next = scores(order[pos + 1])
            yield
            s_own = jnp.where(kpos <= qpos, s[:, q0:], -jnp.inf)
            s = s_own if qi == 0 else jnp.concatenate([s[:, :q0], s_own], axis=1)
            p = jnp.exp2((s - jnp.max(s, axis=-1, keepdims=True)).astype(BF16))
            acc = jnp.dot(p, va_ref[0:n_keys, :], preferred_element_type=F32)
            o_ref[q0:n_keys, hd * d:(hd + 1) * d] = (acc[:, :d] / acc[:, d:]).astype(o_ref.dtype)
            yield

    n_heads = len(scratch) // 3
    for hd in range(n_heads):
        prepare(hd)
    heads = [attend(hd) for hd in range(n_heads)]
    for _ in range(2 * nb):
        for tiles in heads:
            next(tiles)


def _moba(mqk, plain, batch, seq, cols):
    t = mqk.shape[0]
    d = MOBA_HEAD_DIM
    hp = MOBA_HEADS_PER_STEP
    w = hp * d
    n_groups = MOBA_HEADS // hp
    v_blk0 = _col_block(cols["mv"], w)
    blk_onehot = jnp.asarray(np.arange(seq)[:, None] // MOBA_BLOCK == np.arange(LANES)[None, :], BF16)
    return pl.pallas_call(
        _moba_kernel,
        grid=(batch, n_groups),
        in_specs=[pl.BlockSpec((seq, w), lambda b, g: (b, g)),
                  pl.BlockSpec((seq, w), lambda b, g: (b, n_groups + g)),
                  pl.BlockSpec((seq, w), lambda b, g: (b, v_blk0 + g)),
                  pl.BlockSpec((seq, LANES), lambda b, g: (0, 0))],
        out_specs=pl.BlockSpec((seq, w), lambda b, g: (b, g)),
        out_shape=jax.ShapeDtypeStruct((t, MOBA_WIDTH), BF16),
        scratch_shapes=[pltpu.VMEM((seq, d + LANES), BF16)] * (3 * hp),
        compiler_params=_params(("parallel", "parallel"), 48),
        name="moba",
    )(mqk, mqk, plain, blk_onehot)


def _merge_kernel(yr_ref, ym_ref, gr_ref, gm_ref, x_ref, wr_ref, wm_ref, wo_ref, nw_ref, x1_ref, h2_ref):
    ret = jnp.dot(yr_ref[...], wr_ref[...], preferred_element_type=F32)
    mob = jnp.dot(ym_ref[...], wm_ref[...], preferred_element_type=F32)
    merged = _sigmoid(gr_ref[...].astype(F32)) * ret + _sigmoid(gm_ref[...].astype(F32)) * mob
    x1 = x_ref[...] + jnp.dot(merged.astype(BF16), wo_ref[...], preferred_element_type=F32)
    x1_ref[...] = x1
    r = lax.rsqrt(jnp.mean(x1 * x1, axis=-1, keepdims=True) + EPS)
    h2_ref[...] = (x1 * r * nw_ref[...]).astype(h2_ref.dtype)


def _merge(y_ret, y_moba, plain, x, w_ret_up, w_moba_up, w_out, norm_w, cols, tm=512):
    t, d = x.shape
    resident = lambda shape: pl.BlockSpec(shape, lambda i: (0, 0), pipeline_mode=pl.Buffered(1))
    g_ret_blk = _col_block(cols["g_ret"], d)
    g_moba_blk = _col_block(cols["g_moba"], d)
    return pl.pallas_call(
        _merge_kernel,
        grid=(t // tm,),
        in_specs=[pl.BlockSpec((tm, RET_WIDTH), lambda i: (i, 0)),
                  pl.BlockSpec((tm, MOBA_WIDTH), lambda i: (i, 0)),
                  pl.BlockSpec((tm, d), lambda i: (i, g_ret_blk)),
                  pl.BlockSpec((tm, d), lambda i: (i, g_moba_blk)),
                  pl.BlockSpec((tm, d), lambda i: (i, 0)),
                  resident(w_ret_up.shape), resident(w_moba_up.shape), resident(w_out.shape),
                  pl.BlockSpec((1, d), lambda i: (0, 0))],
        out_specs=[pl.BlockSpec((tm, d), lambda i: (i, 0)), pl.BlockSpec((tm, d), lambda i: (i, 0))],
        out_shape=[jax.ShapeDtypeStruct((t, d), F32), jax.ShapeDtypeStruct((t, d), BF16)],
        compiler_params=_params(("parallel",), 62),
        name="merge_outproj",
    )(y_ret, y_moba, plain, plain, x, w_ret_up, w_moba_up, w_out, norm_w.reshape(1, d))


def _ffn_up_kernel(h_ref, wa_ref, wb_ref, cwa_ref, cwb_ref, cba_ref, cbb_ref, o_ref, ua_ref, ub_ref,
                   wa16_ref, wb16_ref, *, tiles_per_seq):
    tm = h_ref.shape[0]
    pad = 8
    i = pl.program_id(1)

    @pl.when(i == 0)
    def _():
        wa16_ref[...] = wa_ref[...].astype(wa16_ref.dtype)
        wb16_ref[...] = wb_ref[...].astype(wb16_ref.dtype)

    @pl.when(i % tiles_per_seq == 0)
    def _():
        ua_ref[0:pad, :] = jnp.zeros((pad, ua_ref.shape[1]), F32)
        ub_ref[0:pad, :] = jnp.zeros((pad, ub_ref.shape[1]), F32)

    h = h_ref[...]
    ua_ref[pad:pad + tm, :] = jnp.dot(h, wa16_ref[...], preferred_element_type=F32)
    ub_ref[pad:pad + tm, :] = jnp.dot(h, wb16_ref[...], preferred_element_type=F32)

    def conv(u_ref, cw_ref, cb_ref, scale):
        u = u_ref[...]
        cw = cw_ref[...] * scale
        y = cw[0:1, :] * u
        for j in range(1, CONV_WIDTH):
            y = cw[j:j + 1, :] * u + pltpu.roll(y, 1, axis=0)
        return y[pad:pad + tm, :] + cb_ref[...] * scale

    half_a = conv(ua_ref, cwa_ref, cba_ref, 0.5)
    b = conv(ub_ref, cwb_ref, cbb_ref, 1.0)
    o_ref[...] = ((half_a + half_a * jnp.tanh(half_a)) * b).astype(o_ref.dtype)
    ua_ref[0:pad, :] = ua_ref[tm:tm + pad, :]
    ub_ref[0:pad, :] = ub_ref[tm:tm + pad, :]


def _ffn_up(h2, w_up, conv_w, conv_b, seq, tm=1024, tn=512):
    t, d = h2.shape
    d_ff = w_up.shape[1] // 2
    nj = d_ff // tn
    conv_b = conv_b.reshape(1, 2 * d_ff)
    return pl.pallas_call(
        functools.partial(_ffn_up_kernel, tiles_per_seq=seq // tm),
        grid=(nj, t // tm),
        in_specs=[pl.BlockSpec((tm, d), lambda j, i: (i, 0)),
                  pl.BlockSpec((d, tn), lambda j, i: (0, j)),
                  pl.BlockSpec((d, tn), lambda j, i: (0, nj + j)),
                  pl.BlockSpec((CONV_WIDTH, tn), lambda j, i: (0, j)),
                  pl.BlockSpec((CONV_WIDTH, tn), lambda j, i: (0, nj + j)),
                  pl.BlockSpec((1, tn), lambda j, i: (0, j)),
                  pl.BlockSpec((1, tn), lambda j, i: (0, nj + j))],
        out_specs=pl.BlockSpec((tm, tn), lambda j, i: (i, j)),
        out_shape=jax.ShapeDtypeStruct((t, d_ff), BF16),
        scratch_shapes=[pltpu.VMEM((tm + 8, tn), F32), pltpu.VMEM((tm + 8, tn), F32),
                        pltpu.VMEM((d, tn), BF16), pltpu.VMEM((d, tn), BF16)],
        compiler_params=_params(("arbitrary", "arbitrary"), 52),
        name="ffn_up_conv_gate",
    )(h2, w_up, w_up, conv_w, conv_w, conv_b, conv_b)


def _ffn_down_kernel(a_ref, w_ref, x_ref, nw_ref, o_ref, *, final_norm):
    x2 = x_ref[...] + jnp.dot(a_ref[...], w_ref[...], preferred_element_type=F32)
    if final_norm:
        r = lax.rsqrt(jnp.mean(x2 * x2, axis=-1, keepdims=True) + EPS)
        x2 = x2 * r * nw_ref[...]
    o_ref[...] = x2


def _ffn_down(act, w_down, x1, norm_w, final_norm, tm=512):
    t, d = x1.shape
    d_ff = act.shape[1]
    return pl.pallas_call(
        functools.partial(_ffn_down_kernel, final_norm=final_norm),
        grid=(t // tm,),
        in_specs=[pl.BlockSpec((tm, d_ff), lambda i: (i, 0)),
                  pl.BlockSpec((d_ff, d), lambda i: (0, 0), pipeline_mode=pl.Buffered(1)),
                  pl.BlockSpec((tm, d), lambda i: (i, 0)),
                  pl.BlockSpec((1, d), lambda i: (0, 0))],
        out_specs=pl.BlockSpec((tm, d), lambda i: (i, 0)),
        out_shape=jax.ShapeDtypeStruct((t, d), F32),
        compiler_params=_params(("parallel",), 62),
        name="ffn_down",
    )(act, w_down, x1, norm_w.reshape(1, d))


def kernel(x, attn_norm_w, w_in, ret_norm_w, w_ret_up, w_moba_up, w_out, ffn_norm_w, w_ffn_up, conv_w,
           conv_b, w_ffn_down, final_norm_w):
    batch, seq, d = x.shape
    depth = w_in.shape[0]
    t = batch * seq
    xf = x.reshape(t, d)
    tm = 1024
    tm_norm = 1024
    tiles_per_seq = seq // tm

    ret_tables = _ret_rope_tables(seq)
    mob_tables = _moba_rope_tables(seq)

    def ret_specs(rows):
        return [pl.BlockSpec((rows, LANES), lambda j, i: (i % (seq // rows), 0))] * len(ret_tables)

    mob_specs = [pl.BlockSpec((None, tm, LANES), lambda j, i: (j, i % tiles_per_seq, 0))] * len(mob_tables)

    tn = 1024
    assert RET_WIDTH == tn and MOBA_WIDTH == tn and d % tn == 0
    w_in_col = dict(rv=2 * RET_WIDTH, rg=3 * RET_WIDTH, mv=4 * RET_WIDTH + 2 * MOBA_WIDTH,
                    g_ret=4 * RET_WIDTH + 3 * MOBA_WIDTH, g_moba=4 * RET_WIDTH + 3 * MOBA_WIDTH + d)
    cols = _plain_cols(d)
    widths = dict(g_ret=d, g_moba=d, rv=RET_WIDTH, rg=RET_WIDTH, mv=MOBA_WIDTH)
    src_tiles = [None] * (sum(widths.values()) // tn)
    for name, width in widths.items():
        for k in range(width // tn):
            src_tiles[_col_block(cols[name], tn) + k] = _col_block(w_in_col[name], tn) + k

    def plain_block(j):
        blk = src_tiles[-1]
        for out_tile in range(len(src_tiles) - 2, -1, -1):
            blk = jnp.where(j == out_tile, src_tiles[out_tile], blk)
        return blk

    for l in range(depth):
        (rq, h), _ = _in_proj(xf, w_in[l], lambda j: j, tn, _ret_rotary_epilogue, ret_tables,
                              ret_specs(tm_norm), tm_norm, tn, "in_proj_rq", norm_w=attn_norm_w[l])
        rk, (w_out16,) = _in_proj(h, w_in[l], lambda j: j + _col_block(RET_WIDTH, tn), tn,
                                  _ret_rotary_epilogue, ret_tables, ret_specs(tm), tm, tn, "in_proj_rk",
                                  casts=(w_out[l],))
        mob_qk, (w_ret_up16, w_moba_up16) = _in_proj(
            h, w_in[l], lambda j: j + _col_block(4 * RET_WIDTH, tn), 2 * tn, _moba_rotary_epilogue,
            mob_tables, mob_specs, tm, tn, "in_proj_moba", casts=(w_ret_up[l], w_moba_up[l]))
        plain, (w_down16,) = _in_proj(h, w_in[l], plain_block, len(src_tiles) * tn, _plain_epilogue, (), [],
                                      tm, tn, "in_proj_plain", casts=(w_ffn_down[l],))

        y_ret = _retention(rq, rk, plain, ret_norm_w[l], batch, seq, cols)
        y_moba = _moba(mob_qk, plain, batch, seq, cols)
        x1, h2 = _merge(y_ret, y_moba, plain, xf, w_ret_up16, w_moba_up16, w_out16, ffn_norm_w[l], cols)
        act = _ffn_up(h2, w_ffn_up[l], conv_w[l], conv_b[l], seq)
        last = l == depth - 1
        xf = _ffn_down(act, w_down16, x1, final_norm_w, final_norm=last)
    return xf.reshape(batch, seq, d)
```

```python
import functools
import math

import numpy as np
import jax
import jax.numpy as jnp
from jax import lax
from jax.experimental import pallas as pl
from jax.experimental.pallas import tpu as pltpu

RET_HEADS = 4
RET_HEAD_DIM = 256
RET_WIDTH = RET_HEADS * RET_HEAD_DIM
RET_ROPE_THETA = 10000.0
RET_TILE = 256
RET_STEP_ROWS = 2048
MOBA_HEADS = 8
MOBA_HEAD_DIM = 128
MOBA_WIDTH = MOBA_HEADS * MOBA_HEAD_DIM
MOBA_BLOCK = 256
MOBA_TOPK = 3
MOBA_HEADS_PER_STEP = 2
ROPE_THETA = 500000.0
ROT_DIM = MOBA_HEAD_DIM // 4
CONV_WIDTH = 3
FFN_ROW_GROUP = 512
EPS = 1e-6

LANES = 128
BF16_SUBLANES = 16
MASK_PENALTY = -1e30
BF16 = jnp.bfloat16
F32 = jnp.float32

NT_DIMS = (((1,), (1,)), ((), ()))
TN_DIMS = (((0,), (0,)), ((), ()))


def _params(semantics, vmem_mib):
    return pltpu.CompilerParams(dimension_semantics=semantics, vmem_limit_bytes=vmem_mib * 1024 * 1024)


def _sigmoid(x):
    return 0.5 * jnp.tanh(0.5 * x) + 0.5


def _silu(x):
    h = 0.5 * x
    return h + h * jnp.tanh(h)


def _plain_epilogue(acc, o_ref):
    o_ref[...] = acc.astype(o_ref.dtype)


def _ret_rotary_epilogue(acc, cos_ref, sin_ref, o_ref):
    cos = cos_ref[...]
    sin = sin_ref[...]
    half = RET_HEAD_DIM // 2
    for hh in range(acc.shape[1] // RET_HEAD_DIM):
        c0 = hh * RET_HEAD_DIM
        x1 = acc[:, c0:c0 + half]
        x2 = acc[:, c0 + half:c0 + RET_HEAD_DIM]
        o_ref[:, c0:c0 + half] = (x1 * cos - x2 * sin).astype(o_ref.dtype)
        o_ref[:, c0 + half:c0 + RET_HEAD_DIM] = (x2 * cos + x1 * sin).astype(o_ref.dtype)


def _moba_rotary_epilogue(acc, c_ref, s_ref, o_ref):
    c = c_ref[...]
    s = s_ref[...]
    half = ROT_DIM // 2
    lane = lax.broadcasted_iota(jnp.int32, (acc.shape[0], MOBA_HEAD_DIM), 1)
    idx = jnp.where(lane < ROT_DIM, lane ^ half, lane)
    for hh in range(acc.shape[1] // MOBA_HEAD_DIM):
        c0 = hh * MOBA_HEAD_DIM
        x = acc[:, c0:c0 + MOBA_HEAD_DIM]
        partner = jnp.take_along_axis(x, idx, axis=1)
        o_ref[:, c0:c0 + MOBA_HEAD_DIM] = (x * c + partner * s).astype(o_ref.dtype)


def _in_proj_kernel(h_ref, w_ref, *rest, epilogue, n_tables, n_casts, fused_norm):
    rest = list(rest)
    norm_w_ref = rest.pop(0) if fused_norm else None
    tables = rest[:n_tables]
    cast_src = rest[n_tables:n_tables + n_casts]
    o_ref = rest[n_tables + n_casts]
    rest = rest[n_tables + n_casts + 1:]
    h_out_ref = rest.pop(0) if fused_norm else None
    cast_dst = rest[:n_casts]
    wb_ref = rest[-1]

    @pl.when(pl.program_id(1) == 0)
    def _():
        wb_ref[...] = w_ref[...].astype(wb_ref.dtype)

    if fused_norm:
        x = h_ref[...]
        r = lax.rsqrt(jnp.mean(x * x, axis=-1, keepdims=True) + EPS)
        h = (x * r * norm_w_ref[...]).astype(h_out_ref.dtype)
        h_out_ref[...] = h
    else:
        h = h_ref[...]
    acc = jnp.dot(h, wb_ref[...], preferred_element_type=F32)
    epilogue(acc, *tables, o_ref)
    for src_ref, dst_ref in zip(cast_src, cast_dst):
        dst_ref[...] = src_ref[...].astype(dst_ref.dtype)


def _cast_rows_per_step(n_rows, n_steps):
    rows = BF16_SUBLANES
    while n_rows % rows or n_rows // rows > n_steps:
        rows += BF16_SUBLANES
    return rows


def _in_proj(h, w, col_block, n_out, epilogue, tables, table_specs, tm, tn, name, casts=(), norm_w=None):
    t, d = h.shape
    n_i = t // tm
    n_steps = (n_out // tn) * n_i
    fused_norm = norm_w is not None
    assert not fused_norm or n_out == tn
    norm_args = [norm_w.reshape(1, d)] if fused_norm else []
    norm_specs = [pl.BlockSpec((1, d), lambda j, i: (0, 0))] if fused_norm else []
    h_out_specs = [pl.BlockSpec((tm, d), lambda j, i: (i, 0))] if fused_norm else []
    h_out_shapes = [jax.ShapeDtypeStruct((t, d), BF16)] if fused_norm else []
    cast_specs, cast_shapes = [], []
    for c in casts:
        rows = _cast_rows_per_step(c.shape[0], n_steps)
        last = c.shape[0] // rows - 1
        spec = pl.BlockSpec((rows, c.shape[1]), lambda j, i, last=last: (jnp.minimum(j * n_i + i, last), 0))
        cast_specs.append(spec)
        cast_shapes.append(jax.ShapeDtypeStruct(c.shape, BF16))
    outs = pl.pallas_call(
        functools.partial(_in_proj_kernel, epilogue=epilogue, n_tables=len(tables), n_casts=len(casts),
                          fused_norm=fused_norm),
        grid=(n_out // tn, n_i),
        in_specs=([pl.BlockSpec((tm, d), lambda j, i: (i, 0)),
                   pl.BlockSpec((d, tn), lambda j, i: (0, col_block(j)))]
                  + norm_specs + table_specs + cast_specs),
        out_specs=[pl.BlockSpec((tm, tn), lambda j, i: (i, j))] + h_out_specs + cast_specs,
        out_shape=[jax.ShapeDtypeStruct((t, n_out), BF16)] + h_out_shapes + cast_shapes,
        scratch_shapes=[pltpu.VMEM((d, tn), BF16)],
        compiler_params=_params(("arbitrary", "arbitrary"), 56),
        name=name,
    )(h, w, *norm_args, *tables, *casts)
    n_main = 1 + len(h_out_shapes)
    return outs[:n_main] if fused_norm else outs[0], outs[n_main:]


def _rope_angles(seq, rot_dim, theta):
    pos = np.arange(seq, dtype=np.float64)
    inv = np.float64(theta) ** (-np.arange(0, rot_dim, 2, dtype=np.float64) / rot_dim)
    return pos[:, None] * inv[None, :]


def _ret_rope_tables(seq):
    ang = _rope_angles(seq, RET_HEAD_DIM, RET_ROPE_THETA)
    return jnp.asarray(np.cos(ang), F32), jnp.asarray(np.sin(ang), F32)


def _moba_rope_tables(seq):
    half = ROT_DIM // 2
    ang = _rope_angles(seq, ROT_DIM, ROPE_THETA)
    cos, sin = np.cos(ang), np.sin(ang)
    c = np.ones((seq, MOBA_HEAD_DIM))
    s = np.zeros((seq, MOBA_HEAD_DIM))
    c[:, :half] = cos
    c[:, half:ROT_DIM] = cos
    s[:, :half] = -sin
    s[:, half:ROT_DIM] = sin
    q_scale = MOBA_HEAD_DIM ** -0.5 * math.log2(math.e)
    stack = lambda a: jnp.asarray(np.stack([a * q_scale, a]), F32)
    return stack(c), stack(s)


def _retention_kernel(q_ref, k_ref, v_ref, g_ref, nw_ref, o_ref, state_ref, dec_ref, xi_ref, zeta_ref):
    c = pl.program_id(1)
    tile = RET_TILE
    k_scale = RET_HEAD_DIM ** -0.5
    log_gs = [float(np.log1p(-np.exp2(-5.0 - hh))) for hh in range(RET_HEADS)]

    @pl.when(c == 0)
    def _():
        state_ref[...] = jnp.zeros_like(state_ref)
        row = lax.broadcasted_iota(jnp.int32, (tile, tile), 0).astype(F32)
        col = lax.broadcasted_iota(jnp.int32, (tile, tile), 1).astype(F32)
        diff = row - col
        causal = diff >= 0
        row_d = lax.broadcasted_iota(jnp.int32, (tile, RET_HEAD_DIM), 0).astype(F32)
        for hh, log_g in enumerate(log_gs):
            dec_ref[hh] = jnp.where(causal, jnp.exp(log_g * jnp.where(causal, diff, 0.0)), 0.0) * k_scale
            xi_ref[hh] = jnp.exp(log_g * (row_d + 1.0))
            zeta_ref[hh] = jnp.exp(log_g * (tile - 1.0 - row_d)) * k_scale

    def recurrence(rows):
        outs = []
        for hh, log_g in enumerate(log_gs):
            g_chunk = float(np.exp(log_g * tile))
            sl = slice(hh * RET_HEAD_DIM, (hh + 1) * RET_HEAD_DIM)
            q = q_ref[rows, sl]
            k = k_ref[rows, sl]
            v = v_ref[rows, sl]
            state = state_ref[hh]
            scores = lax.dot_general(q, k, NT_DIMS, preferred_element_type=F32) * dec_ref[hh]
            inner = jnp.dot(scores.astype(BF16), v, preferred_element_type=F32)
            cross = jnp.dot(q, state.astype(BF16), preferred_element_type=F32) * xi_ref[hh]
            kz = (k.astype(F32) * zeta_ref[hh]).astype(BF16)
            state_ref[hh] = state * g_chunk + lax.dot_general(kz, v, TN_DIMS, preferred_element_type=F32)
            outs.append(inner + cross)
        return outs

    def epilogue(rows, outs):
        for hh, o in enumerate(outs):
            sl = slice(hh * RET_HEAD_DIM, (hh + 1) * RET_HEAD_DIM)
            o = o * lax.rsqrt(jnp.mean(o * o, axis=-1, keepdims=True) + EPS)
            o_ref[rows, sl] = (_silu(g_ref[rows, sl].astype(F32)) * (o * nw_ref[:, sl])).astype(o_ref.dtype)

    pending = None
    for sub in range(q_ref.shape[0] // tile):
        rows = slice(sub * tile, (sub + 1) * tile)
        outs = recurrence(rows)
        if pending is not None:
            epilogue(*pending)
        pending = (rows, outs)
    epilogue(*pending)


def _plain_cols(d_model):
    return dict(g_ret=0, g_moba=d_model, rv=2 * d_model, rg=2 * d_model + RET_WIDTH,
                mv=2 * d_model + 2 * RET_WIDTH)


def _col_block(col, width):
    assert col % width == 0
    return col // width


def _retention(rq, rk, plain, norm_w, batch, seq, cols):
    t = rq.shape[0]
    blk = RET_STEP_ROWS
    n_blk = seq // blk
    row = lambda b, c: b * n_blk + c
    hd = RET_HEAD_DIM
    rv_blk = _col_block(cols["rv"], RET_WIDTH)
    rg_blk = _col_block(cols["rg"], RET_WIDTH)
    return pl.pallas_call(
        _retention_kernel,
        grid=(batch, n_blk),
        in_specs=[pl.BlockSpec((blk, RET_WIDTH), lambda b, c: (row(b, c), 0)),
                  pl.BlockSpec((blk, RET_WIDTH), lambda b, c: (row(b, c), 0)),
                  pl.BlockSpec((blk, RET_WIDTH), lambda b, c: (row(b, c), rv_blk)),
                  pl.BlockSpec((blk, RET_WIDTH), lambda b, c: (row(b, c), rg_blk)),
                  pl.BlockSpec((1, RET_WIDTH), lambda b, c: (0, 0))],
        out_specs=pl.BlockSpec((blk, RET_WIDTH), lambda b, c: (row(b, c), 0)),
        out_shape=jax.ShapeDtypeStruct((t, RET_WIDTH), BF16),
        scratch_shapes=[pltpu.VMEM((RET_HEADS, hd, hd), F32),
                        pltpu.VMEM((RET_HEADS, RET_TILE, RET_TILE), F32),
                        pltpu.VMEM((RET_HEADS, RET_TILE, hd), F32),
                        pltpu.VMEM((RET_HEADS, RET_TILE, hd), F32)],
        compiler_params=_params(("parallel", "arbitrary"), 56),
        name="retention",
    )(rq, rk, plain, plain, norm_w.reshape(1, RET_WIDTH))


def _moba_kernel(q_ref, k_ref, v_ref, blk_onehot_ref, o_ref, *scratch):
    seq = q_ref.shape[0]
    d = MOBA_HEAD_DIM
    bs = MOBA_BLOCK
    bs_shift = bs.bit_length() - 1
    assert bs == 1 << bs_shift
    nb = seq // bs
    blk = lax.broadcasted_iota(jnp.int32, (nb, seq), 0)
    own = lax.broadcasted_iota(jnp.int32, (nb, seq), 1) >> bs_shift
    past = blk < own
    qpos = lax.broadcasted_iota(jnp.int32, (bs, bs), 0)
    kpos = lax.broadcasted_iota(jnp.int32, (bs, bs), 1)

    def prepare(hd):
        qa_ref, ka_ref, va_ref = scratch[3 * hd:3 * hd + 3]
        cols = slice(hd * d, (hd + 1) * d)
        q = q_ref[:, cols]
        k = k_ref[:, cols]
        kmean = jnp.sum(k.astype(F32).reshape(nb, bs, d), axis=1) * (1.0 / bs)
        km_hi = kmean.astype(BF16)
        km_lo = (kmean - km_hi.astype(F32)).astype(BF16)
        both = lax.dot_general(jnp.concatenate([km_hi, km_lo], axis=0), q, NT_DIMS,
                               preferred_element_type=F32)
        score = both[:nb] + both[nb:]
        score = jnp.where(past, score, -jnp.inf)
        rank = jnp.zeros((nb, seq), jnp.int32)
        for r in range(1, nb):
            other = pltpu.roll(score, r, axis=0)
            beats = (other > score) | ((other == score) & (blk >= r))
            rank = rank + beats.astype(jnp.int32)
        keep = (past & (rank < MOBA_TOPK)) | (blk == own)
        pen_t = jnp.where(keep, 0.0, MASK_PENALTY)
        pen_t = jnp.concatenate([pen_t, jnp.zeros((LANES - nb, seq), F32)], axis=0)
        qa_ref[:, :d] = q
        qa_ref[:, d:] = pen_t.T.astype(BF16)
        ka_ref[:, :d] = k
        ka_ref[:, d:] = blk_onehot_ref[...]
        va_ref[:, :d] = v_ref[:, cols]
        va_ref[:, d:] = jnp.ones((seq, LANES), BF16)

    def attend(hd):
        qa_ref, ka_ref, va_ref = scratch[3 * hd:3 * hd + 3]

        def scores(qi):
            n_keys = (qi + 1) * bs
            return lax.dot_general(qa_ref[qi * bs:n_keys, :], ka_ref[0:n_keys, :], NT_DIMS,
                                   preferred_element_type=F32)

        order = list(range(nb - 1, -1, -1))
        s_next = scores(order[0])
        for pos, qi in enumerate(order):
            q0 = qi * bs
            n_keys = q0 + bs
            s = s_next
            if pos + 1 < nb:
                s_next = scores(order[pos + 1])
            yield
            s_own = jnp.where(kpos <= qpos, s[:, q0:], -jnp.inf)
            s = s_own if qi == 0 else jnp.concatenate([s[:, :q0], s_own], axis=1)
            p = jnp.exp2((s - jnp.max(s, axis=-1, keepdims=True)).astype(BF16))
            acc = jnp.dot(p, va_ref[0:n_keys, :], preferred_element_type=F32)
            o_ref[q0:n_keys, hd * d:(hd + 1) * d] = (acc[:, :d] / acc[:, d:]).astype(o_ref.dtype)
            yield

    n_heads = len(scratch) // 3
    for hd in range(n_heads):
        prepare(hd)
    heads = [attend(hd) for hd in range(n_heads)]
    for _ in range(2 * nb):
        for tiles in heads:
            next(tiles)


def _moba(mqk, plain, batch, seq, cols):
    t = mqk.shape[0]
    d = MOBA_HEAD_DIM
    hp = MOBA_HEADS_PER_STEP
    w = hp * d
    n_groups = MOBA_HEADS // hp
    v_blk0 = _col_block(cols["mv"], w)
    blk_onehot = jnp.asarray(np.arange(seq)[:, None] // MOBA_BLOCK == np.arange(LANES)[None, :], BF16)
    return pl.pallas_call(
        _moba_kernel,
        grid=(batch, n_groups),
        in_specs=[pl.BlockSpec((seq, w), lambda b, g: (b, g)),
                  pl.BlockSpec((seq, w), lambda b, g: (b, n_groups + g)),
                  pl.BlockSpec((seq, w), lambda b, g: (b, v_blk0 + g)),
                  pl.BlockSpec((seq, LANES), lambda b, g: (0, 0))],
        out_specs=pl.BlockSpec((seq, w), lambda b, g: (b, g)),
        out_shape=jax.ShapeDtypeStruct((t, MOBA_WIDTH), BF16),
        scratch_shapes=[pltpu.VMEM((seq, d + LANES), BF16)] * (3 * hp),
        compiler_params=_params(("parallel", "parallel"), 48),
        name="moba",
    )(mqk, mqk, plain, blk_onehot)


def _merge_kernel(yr_ref, ym_ref, gr_ref, gm_ref, x_ref, wr_ref, wm_ref, wo_ref, nw_ref, x1_ref, h2_ref):
    ret = jnp.dot(yr_ref[...], wr_ref[...], preferred_element_type=F32)
    mob = jnp.dot(ym_ref[...], wm_ref[...], preferred_element_type=F32)
    merged = _sigmoid(gr_ref[...].astype(F32)) * ret + _sigmoid(gm_ref[...].astype(F32)) * mob
    x1 = x_ref[...] + jnp.dot(merged.astype(BF16), wo_ref[...], preferred_element_type=F32)
    x1_ref[...] = x1
    r = lax.rsqrt(jnp.mean(x1 * x1, axis=-1, keepdims=True) + EPS)
    h2_ref[...] = (x1 * r * nw_ref[...]).astype(h2_ref.dtype)


def _merge(y_ret, y_moba, plain, x, w_ret_up, w_moba_up, w_out, norm_w, cols, tm=512):
    t, d = x.shape
    resident = lambda shape: pl.BlockSpec(shape, lambda i: (0, 0), pipeline_mode=pl.Buffered(1))
    g_ret_blk = _col_block(cols["g_ret"], d)
    g_moba_blk = _col_block(cols["g_moba"], d)
    return pl.pallas_call(
        _merge_kernel,
        grid=(t // tm,),
        in_specs=[pl.BlockSpec((tm, RET_WIDTH), lambda i: (i, 0)),
                  pl.BlockSpec((tm, MOBA_WIDTH), lambda i: (i, 0)),
                  pl.BlockSpec((tm, d), lambda i: (i, g_ret_blk)),
                  pl.BlockSpec((tm, d), lambda i: (i, g_moba_blk)),
                  pl.BlockSpec((tm, d), lambda i: (i, 0)),
                  resident(w_ret_up.shape), resident(w_moba_up.shape), resident(w_out.shape),
                  pl.BlockSpec((1, d), lambda i: (0, 0))],
        out_specs=[pl.BlockSpec((tm, d), lambda i: (i, 0)), pl.BlockSpec((tm, d), lambda i: (i, 0))],
        out_shape=[jax.ShapeDtypeStruct((t, d), F32), jax.ShapeDtypeStruct((t, d), BF16)],
        compiler_params=_params(("parallel",), 62),
        name="merge_outproj",
    )(y_ret, y_moba, plain, plain, x, w_ret_up, w_moba_up, w_out, norm_w.reshape(1, d))


def _ffn_up_kernel(h_ref, wa_ref, wb_ref, cwa_ref, cwb_ref, cba_ref, cbb_ref, o_ref, ua_ref, ub_ref, y_ref,
                   wa16_ref, wb16_ref, *, tiles_per_seq):
    tm = h_ref.shape[0]
    pad = 8
    i = pl.program_id(1)
    slabs = ua_ref.shape[0]

    @pl.when(i == 0)
    def _():
        wa16_ref[...] = wa_ref[...].astype(wa16_ref.dtype)
        wb16_ref[...] = wb_ref[...].astype(wb16_ref.dtype)

    @pl.when(i % tiles_per_seq == 0)
    def _():
        ua_ref[:, 0:pad, :] = jnp.zeros((slabs, pad, LANES), F32)
        ub_ref[:, 0:pad, :] = jnp.zeros((slabs, pad, LANES), F32)

    h = h_ref[...]
    for u_ref, w16_ref in ((ua_ref, wa16_ref), (ub_ref, wb16_ref)):
        u = jnp.dot(h, w16_ref[...], preferred_element_type=F32)
        for c in range(slabs):
            u_ref[c, pad:pad + tm, :] = u[:, c * LANES:(c + 1) * LANES]

    grp = FFN_ROW_GROUP
    n = grp // 2

    def row_sets(u_ref, c, g):
        base = pad + g * grp
        return [u_ref[c, pl.ds(base + s, n, stride=2), :] for s in range(1 - CONV_WIDTH, 2)]

    def conv(rows, cw_ref, cb_ref, lanes, scale):
        y = cb_ref[:, lanes] * scale
        for j in range(CONV_WIDTH):
            y = y + (cw_ref[j:j + 1, lanes] * scale) * rows[j]
        return y

    for g in range(tm // grp):
        for c in range(slabs):
            lanes = slice(c * LANES, (c + 1) * LANES)
            rows_a = row_sets(ua_ref, c, g)
            rows_b = row_sets(ub_ref, c, g)
            for par in range(2):
                half_a = conv(rows_a[par:par + CONV_WIDTH], cwa_ref, cba_ref, lanes, 0.5)
                b = conv(rows_b[par:par + CONV_WIDTH], cwb_ref, cbb_ref, lanes, 1.0)
                y_ref[c, pl.ds(g * grp + par, n, stride=2), :] = (half_a + half_a * jnp.tanh(half_a)) * b
            o_ref[g * grp:(g + 1) * grp, lanes] = y_ref[c, g * grp:(g + 1) * grp, :].astype(o_ref.dtype)
    ua_ref[:, 0:pad, :] = ua_ref[:, tm:tm + pad, :]
    ub_ref[:, 0:pad, :] = ub_ref[:, tm:tm + pad, :]


def _ffn_up(h2, w_up, conv_w, conv_b, seq, tm=1024, tn=512):
    t, d = h2.shape
    d_ff = w_up.shape[1] // 2
    nj = d_ff // tn
    conv_b = conv_b.reshape(1, 2 * d_ff)
    return pl.pallas_call(
        functools.partial(_ffn_up_kernel, tiles_per_seq=seq // tm),
        grid=(nj, t // tm),
        in_specs=[pl.BlockSpec((tm, d), lambda j, i: (i, 0)),
                  pl.BlockSpec((d, tn), lambda j, i: (0, j)),
                  pl.BlockSpec((d, tn), lambda j, i: (0, nj + j)),
                  pl.BlockSpec((CONV_WIDTH, tn), lambda j, i: (0, j)),
                  pl.BlockSpec((CONV_WIDTH, tn), lambda j, i: (0, nj + j)),
                  pl.BlockSpec((1, tn), lambda j, i: (0, j)),
                  pl.BlockSpec((1, tn), lambda j, i: (0, nj + j))],
        out_specs=pl.BlockSpec((tm, tn), lambda j, i: (i, j)),
        out_shape=jax.ShapeDtypeStruct((t, d_ff), BF16),
        scratch_shapes=([pltpu.VMEM((tn // LANES, tm + 8, LANES), F32)] * 2
                        + [pltpu.VMEM((tn // LANES, tm, LANES), F32)] + [pltpu.VMEM((d, tn), BF16)] * 2),
        compiler_params=_params(("arbitrary", "arbitrary"), 56),
        name="ffn_up_conv_gate",
    )(h2, w_up, w_up, conv_w, conv_w, conv_b, conv_b)


def _ffn_down_kernel(a_ref, w_ref, x_ref, nw_ref, o_ref, *, final_norm):
    x2 = x_ref[...] + jnp.dot(a_ref[...], w_ref[...], preferred_element_type=F32)
    if final_norm:
        r = lax.rsqrt(jnp.mean(x2 * x2, axis=-1, keepdims=True) + EPS)
        x2 = x2 * r * nw_ref[...]
    o_ref[...] = x2


def _ffn_down(act, w_down, x1, norm_w, final_norm, tm=512):
    t, d = x1.shape
    d_ff = act.shape[1]
    return pl.pallas_call(
        functools.partial(_ffn_down_kernel, final_norm=final_norm),
        grid=(t // tm,),
        in_specs=[pl.BlockSpec((tm, d_ff), lambda i: (i, 0)),
                  pl.BlockSpec((d_ff, d), lambda i: (0, 0), pipeline_mode=pl.Buffered(1)),
                  pl.BlockSpec((tm, d), lambda i: (i, 0)),
                  pl.BlockSpec((1, d), lambda i: (0, 0))],
        out_specs=pl.BlockSpec((tm, d), lambda i: (i, 0)),
        out_shape=jax.ShapeDtypeStruct((t, d), F32),
        compiler_params=_params(("parallel",), 62),
        name="ffn_down",
    )(act, w_down, x1, norm_w.reshape(1, d))


def kernel(x, attn_norm_w, w_in, ret_norm_w, w_ret_up, w_moba_up, w_out, ffn_norm_w, w_ffn_up, conv_w,
           conv_b, w_ffn_down, final_norm_w):
    batch, seq, d = x.shape
    depth = w_in.shape[0]
    t = batch * seq
    xf = x.reshape(t, d)
    tm = 1024
    tm_norm = 1024
    tiles_per_seq = seq // tm

    ret_tables = _ret_rope_tables(seq)
    mob_tables = _moba_rope_tables(seq)

    def ret_specs(rows):
        return [pl.BlockSpec((rows, LANES), lambda j, i: (i % (seq // rows), 0))] * len(ret_tables)

    mob_specs = [pl.BlockSpec((None, tm, LANES), lambda j, i: (j, i % tiles_per_seq, 0))] * len(mob_tables)

    tn = 1024
    assert RET_WIDTH == tn and MOBA_WIDTH == tn and d % tn == 0
    w_in_col = dict(rv=2 * RET_WIDTH, rg=3 * RET_WIDTH, mv=4 * RET_WIDTH + 2 * MOBA_WIDTH,
                    g_ret=4 * RET_WIDTH + 3 * MOBA_WIDTH, g_moba=4 * RET_WIDTH + 3 * MOBA_WIDTH + d)
    cols = _plain_cols(d)
    widths = dict(g_ret=d, g_moba=d, rv=RET_WIDTH, rg=RET_WIDTH, mv=MOBA_WIDTH)
    src_tiles = [None] * (sum(widths.values()) // tn)
    for name, width in widths.items():
        for k in range(width // tn):
            src_tiles[_col_block(cols[name], tn) + k] = _col_block(w_in_col[name], tn) + k

    def plain_block(j):
        blk = src_tiles[-1]
        for out_tile in range(len(src_tiles) - 2, -1, -1):
            blk = jnp.where(j == out_tile, src_tiles[out_tile], blk)
        return blk

    for l in range(depth):
        (rq, h), _ = _in_proj(xf, w_in[l], lambda j: j, tn, _ret_rotary_epilogue, ret_tables,
                              ret_specs(tm_norm), tm_norm, tn, "in_proj_rq", norm_w=attn_norm_w[l])
        rk, (w_out16,) = _in_proj(h, w_in[l], lambda j: j + _col_block(RET_WIDTH, tn), tn,
                                  _ret_rotary_epilogue, ret_tables, ret_specs(tm), tm, tn, "in_proj_rk",
                                  casts=(w_out[l],))
        mob_qk, (w_ret_up16, w_moba_up16) = _in_proj(
            h, w_in[l], lambda j: j + _col_block(4 * RET_WIDTH, tn), 2 * tn, _moba_rotary_epilogue,
            mob_tables, mob_specs, tm, tn, "in_proj_moba", casts=(w_ret_up[l], w_moba_up[l]))
        plain, (w_down16,) = _in_proj(h, w_in[l], plain_block, len(src_tiles) * tn, _plain_epilogue, (), [],
                                      tm, tn, "in_proj_plain", casts=(w_ffn_down[l],))

        y_ret = _retention(rq, rk, plain, ret_norm_w[l], batch, seq, cols)
        y_moba = _moba(mob_qk, plain, batch, seq, cols)
        x1, h2 = _merge(y_ret, y_moba, plain, xf, w_ret_up16, w_moba_up16, w_out16, ffn_norm_w[l], cols)
        act = _ffn_up(h2, w_ffn_up[l], conv_w[l], conv_b[l], seq)
        last = l == depth - 1
        xf = _ffn_down(act, w_down16, x1, final_norm_w, final_norm=last)
    return xf.reshape(batch, seq, d)
```

```python
import functools
import math

import numpy as np
import jax
import jax.numpy as jnp
from jax import lax
from jax.experimental import pallas as pl
from jax.experimental.pallas import tpu as pltpu

RET_HEADS = 4
RET_HEAD_DIM = 256
RET_WIDTH = RET_HEADS * RET_HEAD_DIM
RET_ROPE_THETA = 10000.0
RET_TILE = 256
RET_STEP_ROWS = 2048
MOBA_HEADS = 8
MOBA_HEAD_DIM = 128
MOBA_WIDTH = MOBA_HEADS * MOBA_HEAD_DIM
MOBA_BLOCK = 256
MOBA_TOPK = 3
MOBA_HEADS_PER_STEP = 2
ROPE_THETA = 500000.0
ROT_DIM = MOBA_HEAD_DIM // 4
CONV_WIDTH = 3
EPS = 1e-6

LANES = 128
BF16_SUBLANES = 16
MASK_PENALTY = -1e30
BF16 = jnp.bfloat16
F32 = jnp.float32

NT_DIMS = (((1,), (1,)), ((), ()))
TN_DIMS = (((0,), (0,)), ((), ()))


def _params(semantics, vmem_mib):
    return pltpu.CompilerParams(dimension_semantics=semantics, vmem_limit_bytes=vmem_mib * 1024 * 1024)


def _sigmoid(x):
    return 0.5 * jnp.tanh(0.5 * x) + 0.5


def _silu(x):
    h = 0.5 * x
    return h + h * jnp.tanh(h)


def _plain_epilogue(acc, o_ref):
    o_ref[...] = acc.astype(o_ref.dtype)


def _ret_rotary_epilogue(acc, cos_ref, sin_ref, o_ref):
    cos = cos_ref[...]
    sin = sin_ref[...]
    half = RET_HEAD_DIM // 2
    for hh in range(acc.shape[1] // RET_HEAD_DIM):
        c0 = hh * RET_HEAD_DIM
        x1 = acc[:, c0:c0 + half]
        x2 = acc[:, c0 + half:c0 + RET_HEAD_DIM]
        o_ref[:, c0:c0 + half] = (x1 * cos - x2 * sin).astype(o_ref.dtype)
        o_ref[:, c0 + half:c0 + RET_HEAD_DIM] = (x2 * cos + x1 * sin).astype(o_ref.dtype)


def _moba_rotary_epilogue(acc, c_ref, s_ref, o_ref):
    c = c_ref[...]
    s = s_ref[...]
    half = ROT_DIM // 2
    lane = lax.broadcasted_iota(jnp.int32, (acc.shape[0], MOBA_HEAD_DIM), 1)
    idx = jnp.where(lane < ROT_DIM, lane ^ half, lane)
    for hh in range(acc.shape[1] // MOBA_HEAD_DIM):
        c0 = hh * MOBA_HEAD_DIM
        x = acc[:, c0:c0 + MOBA_HEAD_DIM]
        partner = jnp.take_along_axis(x, idx, axis=1)
        o_ref[:, c0:c0 + MOBA_HEAD_DIM] = (x * c + partner * s).astype(o_ref.dtype)


def _in_proj_kernel(h_ref, w_ref, *rest, epilogue, n_tables, n_casts, fused_norm):
    rest = list(rest)
    norm_w_ref = rest.pop(0) if fused_norm else None
    tables = rest[:n_tables]
    cast_src = rest[n_tables:n_tables + n_casts]
    o_ref = rest[n_tables + n_casts]
    rest = rest[n_tables + n_casts + 1:]
    h_out_ref = rest.pop(0) if fused_norm else None
    cast_dst = rest[:n_casts]
    wb_ref = rest[-1]

    @pl.when(pl.program_id(1) == 0)
    def _():
        wb_ref[...] = w_ref[...].astype(wb_ref.dtype)

    if fused_norm:
        x = h_ref[...]
        r = lax.rsqrt(jnp.mean(x * x, axis=-1, keepdims=True) + EPS)
        h = (x * r * norm_w_ref[...]).astype(h_out_ref.dtype)
        h_out_ref[...] = h
    else:
        h = h_ref[...]
    acc = jnp.dot(h, wb_ref[...], preferred_element_type=F32)
    epilogue(acc, *tables, o_ref)
    for src_ref, dst_ref in zip(cast_src, cast_dst):
        dst_ref[...] = src_ref[...].astype(dst_ref.dtype)


def _cast_rows_per_step(n_rows, n_steps):
    rows = BF16_SUBLANES
    while n_rows % rows or n_rows // rows > n_steps:
        rows += BF16_SUBLANES
    return rows


def _in_proj(h, w, col_block, n_out, epilogue, tables, table_specs, tm, tn, name, casts=(), norm_w=None):
    t, d = h.shape
    n_i = t // tm
    n_steps = (n_out // tn) * n_i
    fused_norm = norm_w is not None
    assert not fused_norm or n_out == tn
    norm_args = [norm_w.reshape(1, d)] if fused_norm else []
    norm_specs = [pl.BlockSpec((1, d), lambda j, i: (0, 0))] if fused_norm else []
    h_out_specs = [pl.BlockSpec((tm, d), lambda j, i: (i, 0))] if fused_norm else []
    h_out_shapes = [jax.ShapeDtypeStruct((t, d), BF16)] if fused_norm else []
    cast_specs, cast_shapes = [], []
    for c in casts:
        rows = _cast_rows_per_step(c.shape[0], n_steps)
        last = c.shape[0] // rows - 1
        spec = pl.BlockSpec((rows, c.shape[1]), lambda j, i, last=last: (jnp.minimum(j * n_i + i, last), 0))
        cast_specs.append(spec)
        cast_shapes.append(jax.ShapeDtypeStruct(c.shape, BF16))
    outs = pl.pallas_call(
        functools.partial(_in_proj_kernel, epilogue=epilogue, n_tables=len(tables), n_casts=len(casts),
                          fused_norm=fused_norm),
        grid=(n_out // tn, n_i),
        in_specs=([pl.BlockSpec((tm, d), lambda j, i: (i, 0)),
                   pl.BlockSpec((d, tn), lambda j, i: (0, col_block(j)))]
                  + norm_specs + table_specs + cast_specs),
        out_specs=[pl.BlockSpec((tm, tn), lambda j, i: (i, j))] + h_out_specs + cast_specs,
        out_shape=[jax.ShapeDtypeStruct((t, n_out), BF16)] + h_out_shapes + cast_shapes,
        scratch_shapes=[pltpu.VMEM((d, tn), BF16)],
        compiler_params=_params(("arbitrary", "arbitrary"), 56),
        name=name,
    )(h, w, *norm_args, *tables, *casts)
    n_main = 1 + len(h_out_shapes)
    return outs[:n_main] if fused_norm else outs[0], outs[n_main:]


def _rope_angles(seq, rot_dim, theta):
    pos = np.arange(seq, dtype=np.float64)
    inv = np.float64(theta) ** (-np.arange(0, rot_dim, 2, dtype=np.float64) / rot_dim)
    return pos[:, None] * inv[None, :]


def _ret_rope_tables(seq):
    ang = _rope_angles(seq, RET_HEAD_DIM, RET_ROPE_THETA)
    return jnp.asarray(np.cos(ang), F32), jnp.asarray(np.sin(ang), F32)


def _moba_rope_tables(seq):
    half = ROT_DIM // 2
    ang = _rope_angles(seq, ROT_DIM, ROPE_THETA)
    cos, sin = np.cos(ang), np.sin(ang)
    c = np.ones((seq, MOBA_HEAD_DIM))
    s = np.zeros((seq, MOBA_HEAD_DIM))
    c[:, :half] = cos
    c[:, half:ROT_DIM] = cos
    s[:, :half] = -sin
    s[:, half:ROT_DIM] = sin
    q_scale = MOBA_HEAD_DIM ** -0.5 * math.log2(math.e)
    stack = lambda a: jnp.asarray(np.stack([a * q_scale, a]), F32)
    return stack(c), stack(s)


def _retention_kernel(q_ref, k_ref, v_ref, g_ref, nw_ref, o_ref, state_ref, dec_ref, xi_ref, zeta_ref):
    c = pl.program_id(1)
    tile = RET_TILE
    k_scale = RET_HEAD_DIM ** -0.5
    log_gs = [float(np.log1p(-np.exp2(-5.0 - hh))) for hh in range(RET_HEADS)]

    @pl.when(c == 0)
    def _():
        state_ref[...] = jnp.zeros_like(state_ref)
        row = lax.broadcasted_iota(jnp.int32, (tile, tile), 0).astype(F32)
        col = lax.broadcasted_iota(jnp.int32, (tile, tile), 1).astype(F32)
        diff = row - col
        causal = diff >= 0
        row_d = lax.broadcasted_iota(jnp.int32, (tile, RET_HEAD_DIM), 0).astype(F32)
        for hh, log_g in enumerate(log_gs):
            dec_ref[hh] = jnp.where(causal, jnp.exp(log_g * jnp.where(causal, diff, 0.0)), 0.0) * k_scale
            xi_ref[hh] = jnp.exp(log_g * (row_d + 1.0))
            zeta_ref[hh] = jnp.exp(log_g * (tile - 1.0 - row_d)) * k_scale

    def recurrence(rows):
        outs = []
        for hh, log_g in enumerate(log_gs):
            g_chunk = float(np.exp(log_g * tile))
            sl = slice(hh * RET_HEAD_DIM, (hh + 1) * RET_HEAD_DIM)
            q = q_ref[rows, sl]
            k = k_ref[rows, sl]
            v = v_ref[rows, sl]
            state = state_ref[hh]
            scores = lax.dot_general(q, k, NT_DIMS, preferred_element_type=F32) * dec_ref[hh]
            inner = jnp.dot(scores.astype(BF16), v, preferred_element_type=F32)
            cross = jnp.dot(q, state.astype(BF16), preferred_element_type=F32) * xi_ref[hh]
            kz = (k.astype(F32) * zeta_ref[hh]).astype(BF16)
            state_ref[hh] = state * g_chunk + lax.dot_general(kz, v, TN_DIMS, preferred_element_type=F32)
            outs.append(inner + cross)
        return outs

    def epilogue(rows, outs):
        for hh, o in enumerate(outs):
            sl = slice(hh * RET_HEAD_DIM, (hh + 1) * RET_HEAD_DIM)
            o = o * lax.rsqrt(jnp.mean(o * o, axis=-1, keepdims=True) + EPS)
            o_ref[rows, sl] = (_silu(g_ref[rows, sl].astype(F32)) * (o * nw_ref[:, sl])).astype(o_ref.dtype)

    pending = None
    for sub in range(q_ref.shape[0] // tile):
        rows = slice(sub * tile, (sub + 1) * tile)
        outs = recurrence(rows)
        if pending is not None:
            epilogue(*pending)
        pending = (rows, outs)
    epilogue(*pending)


def _plain_cols(d_model):
    return dict(g_ret=0, g_moba=d_model, rv=2 * d_model, rg=2 * d_model + RET_WIDTH,
                mv=2 * d_model + 2 * RET_WIDTH)


def _col_block(col, width):
    assert col % width == 0
    return col // width


def _retention(rq, rk, plain, norm_w, batch, seq, cols):
    t = rq.shape[0]
    blk = RET_STEP_ROWS
    n_blk = seq // blk
    row = lambda b, c: b * n_blk + c
    hd = RET_HEAD_DIM
    rv_blk = _col_block(cols["rv"], RET_WIDTH)
    rg_blk = _col_block(cols["rg"], RET_WIDTH)
    return pl.pallas_call(
        _retention_kernel,
        grid=(batch, n_blk),
        in_specs=[pl.BlockSpec((blk, RET_WIDTH), lambda b, c: (row(b, c), 0)),
                  pl.BlockSpec((blk, RET_WIDTH), lambda b, c: (row(b, c), 0)),
                  pl.BlockSpec((blk, RET_WIDTH), lambda b, c: (row(b, c), rv_blk)),
                  pl.BlockSpec((blk, RET_WIDTH), lambda b, c: (row(b, c), rg_blk)),
                  pl.BlockSpec((1, RET_WIDTH), lambda b, c: (0, 0))],
        out_specs=pl.BlockSpec((blk, RET_WIDTH), lambda b, c: (row(b, c), 0)),
        out_shape=jax.ShapeDtypeStruct((t, RET_WIDTH), BF16),
        scratch_shapes=[pltpu.VMEM((RET_HEADS, hd, hd), F32),
                        pltpu.VMEM((RET_HEADS, RET_TILE, RET_TILE), F32),
                        pltpu.VMEM((RET_HEADS, RET_TILE, hd), F32),
                        pltpu.VMEM((RET_HEADS, RET_TILE, hd), F32)],
        compiler_params=_params(("parallel", "arbitrary"), 56),
        name="retention",
    )(rq, rk, plain, plain, norm_w.reshape(1, RET_WIDTH))


def _moba_kernel(q_ref, k_ref, v_ref, blk_onehot_ref, o_ref, *scratch):
    seq = q_ref.shape[0]
    d = MOBA_HEAD_DIM
    bs = MOBA_BLOCK
    bs_shift = bs.bit_length() - 1
    assert bs == 1 << bs_shift
    nb = seq // bs
    blk = lax.broadcasted_iota(jnp.int32, (nb, seq), 0)
    own = lax.broadcasted_iota(jnp.int32, (nb, seq), 1) >> bs_shift
    past = blk < own
    qpos = lax.broadcasted_iota(jnp.int32, (bs, bs), 0)
    kpos = lax.broadcasted_iota(jnp.int32, (bs, bs), 1)

    def prepare(hd):
        qa_ref, ka_ref, va_ref = scratch[3 * hd:3 * hd + 3]
        cols = slice(hd * d, (hd + 1) * d)
        q = q_ref[:, cols]
        k = k_ref[:, cols]
        kmean = jnp.sum(k.astype(F32).reshape(nb, bs, d), axis=1) * (1.0 / bs)
        km_hi = kmean.astype(BF16)
        km_lo = (kmean - km_hi.astype(F32)).astype(BF16)
        both = lax.dot_general(jnp.concatenate([km_hi, km_lo], axis=0), q, NT_DIMS,
                               preferred_element_type=F32)
        score = both[:nb] + both[nb:]
        score = jnp.where(past, score, -jnp.inf)
        rank = jnp.zeros((nb, seq), jnp.int32)
        for r in range(1, nb):
            other = pltpu.roll(score, r, axis=0)
            beats = (other > score) | ((other == score) & (blk >= r))
            rank = rank + beats.astype(jnp.int32)
        keep = (past & (rank < MOBA_TOPK)) | (blk == own)
        pen_t = jnp.where(keep, 0.0, MASK_PENALTY)
        pen_t = jnp.concatenate([pen_t, jnp.zeros((LANES - nb, seq), F32)], axis=0)
        qa_ref[:, :d] = q
        qa_ref[:, d:] = pen_t.T.astype(BF16)
        ka_ref[:, :d] = k
        ka_ref[:, d:] = blk_onehot_ref[...]
        va_ref[:, :d] = v_ref[:, cols]
        va_ref[:, d:] = jnp.ones((seq, LANES), BF16)

    def attend(hd):
        qa_ref, ka_ref, va_ref = scratch[3 * hd:3 * hd + 3]

        def scores(qi):
            n_keys = (qi + 1) * bs
            return lax.dot_general(qa_ref[qi * bs:n_keys, :], ka_ref[0:n_keys, :], NT_DIMS,
                                   preferred_element_type=F32)

        order = list(range(nb - 1, -1, -1))
        s_next = scores(order[0])
        for pos, qi in enumerate(order):
            q0 = qi * bs
            n_keys = q0 + bs
            s = s_next
            if pos + 1 < nb:
                s_next = scores(order[pos + 1])
            yield
            s_own = jnp.where(kpos <= qpos, s[:, q0:], -jnp.inf)
            s = s_own if qi == 0 else jnp.concatenate([s[:, :q0], s_own], axis=1)
            p = jnp.exp2((s - jnp.max(s, axis=-1, keepdims=True)).astype(BF16))
            acc = jnp.dot(p, va_ref[0:n_keys, :], preferred_element_type=F32)
            o_ref[q0:n_keys, hd * d:(hd + 1) * d] = (acc[:, :d] / acc[:, d:]).astype(o_ref.dtype)
            yield

    n_heads = len(scratch) // 3
    for hd in range(n_heads):
        prepare(hd)
    heads = [attend(hd) for hd in range(n_heads)]
    for _ in range(2 * nb):
        for tiles in heads:
            next(tiles)


def _moba(mqk, plain, batch, seq, cols):
    t = mqk.shape[0]
    d = MOBA_HEAD_DIM
    hp = MOBA_HEADS_PER_STEP
    w = hp * d
    n_groups = MOBA_HEADS // hp
    v_blk0 = _col_block(cols["mv"], w)
    blk_onehot = jnp.asarray(np.arange(seq)[:, None] // MOBA_BLOCK == np.arange(LANES)[None, :], BF16)
    return pl.pallas_call(
        _moba_kernel,
        grid=(batch, n_groups),
        in_specs=[pl.BlockSpec((seq, w), lambda b, g: (b, g)),
                  pl.BlockSpec((seq, w), lambda b, g: (b, n_groups + g)),
                  pl.BlockSpec((seq, w), lambda b, g: (b, v_blk0 + g)),
                  pl.BlockSpec((seq, LANES), lambda b, g: (0, 0))],
        out_specs=pl.BlockSpec((seq, w), lambda b, g: (b, g)),
        out_shape=jax.ShapeDtypeStruct((t, MOBA_WIDTH), BF16),
        scratch_shapes=[pltpu.VMEM((seq, d + LANES), BF16)] * (3 * hp),
        compiler_params=_params(("parallel", "parallel"), 48),
        name="moba",
    )(mqk, mqk, plain, blk_onehot)


def _merge_kernel(yr_ref, ym_ref, gr_ref, gm_ref, x_ref, wr_hbm_ref, wm_hbm_ref, wo_hbm_ref, nw_ref,
                  x1_ref, h2_ref, wr_ref, wm_ref, wo_ref, sem_ref):
    copies = [pltpu.make_async_copy(src, dst, sem_ref.at[c]) for c, (src, dst) in enumerate(
        ((wr_hbm_ref, wr_ref), (wm_hbm_ref, wm_ref), (wo_hbm_ref, wo_ref)))]

    def body(wait):
        wait(0)
        ret = jnp.dot(yr_ref[...], wr_ref[...], preferred_element_type=F32)
        wait(1)
        mob = jnp.dot(ym_ref[...], wm_ref[...], preferred_element_type=F32)
        merged = _sigmoid(gr_ref[...].astype(F32)) * ret + _sigmoid(gm_ref[...].astype(F32)) * mob
        wait(2)
        x1 = x_ref[...] + jnp.dot(merged.astype(BF16), wo_ref[...], preferred_element_type=F32)
        x1_ref[...] = x1
        r = lax.rsqrt(jnp.mean(x1 * x1, axis=-1, keepdims=True) + EPS)
        h2_ref[...] = (x1 * r * nw_ref[...]).astype(h2_ref.dtype)

    @pl.when(pl.program_id(0) == 0)
    def _():
        for copy in copies:
            copy.start()
        body(lambda c: copies[c].wait())

    @pl.when(pl.program_id(0) > 0)
    def _():
        body(lambda c: None)


def _merge(y_ret, y_moba, plain, x, w_ret_up, w_moba_up, w_out, norm_w, cols, tm=512):
    t, d = x.shape
    weights = (w_ret_up, w_moba_up, w_out)
    g_ret_blk = _col_block(cols["g_ret"], d)
    g_moba_blk = _col_block(cols["g_moba"], d)
    return pl.pallas_call(
        _merge_kernel,
        grid=(t // tm,),
        in_specs=[pl.BlockSpec((tm, RET_WIDTH), lambda i: (i, 0)),
                  pl.BlockSpec((tm, MOBA_WIDTH), lambda i: (i, 0)),
                  pl.BlockSpec((tm, d), lambda i: (i, g_ret_blk)),
                  pl.BlockSpec((tm, d), lambda i: (i, g_moba_blk)),
                  pl.BlockSpec((tm, d), lambda i: (i, 0)),
                  *[pl.BlockSpec(memory_space=pl.ANY) for _ in weights],
                  pl.BlockSpec((1, d), lambda i: (0, 0))],
        out_specs=[pl.BlockSpec((tm, d), lambda i: (i, 0)), pl.BlockSpec((tm, d), lambda i: (i, 0))],
        out_shape=[jax.ShapeDtypeStruct((t, d), F32), jax.ShapeDtypeStruct((t, d), BF16)],
        scratch_shapes=([pltpu.VMEM(w.shape, w.dtype) for w in weights]
                        + [pltpu.SemaphoreType.DMA((len(weights),))]),
        compiler_params=_params(("arbitrary",), 62),
        name="merge_outproj",
    )(y_ret, y_moba, plain, plain, x, *weights, norm_w.reshape(1, d))


def _ffn_up_kernel(h_ref, wa_ref, wb_ref, cwa_ref, cwb_ref, cba_ref, cbb_ref, o_ref, ua_ref, ub_ref,
                   wa16_ref, wb16_ref, *, tiles_per_seq):
    tm = h_ref.shape[0]
    pad = 8
    i = pl.program_id(1)

    @pl.when(i == 0)
    def _():
        wa16_ref[...] = wa_ref[...].astype(wa16_ref.dtype)
        wb16_ref[...] = wb_ref[...].astype(wb16_ref.dtype)

    @pl.when(i % tiles_per_seq == 0)
    def _():
        ua_ref[0:pad, :] = jnp.zeros((pad, ua_ref.shape[1]), F32)
        ub_ref[0:pad, :] = jnp.zeros((pad, ub_ref.shape[1]), F32)

    h = h_ref[...]
    ua_ref[pad:pad + tm, :] = jnp.dot(h, wa16_ref[...], preferred_element_type=F32)
    ub_ref[pad:pad + tm, :] = jnp.dot(h, wb16_ref[...], preferred_element_type=F32)

    def conv(u_ref, cw_ref, cb_ref, scale):
        u = u_ref[...]
        cw = cw_ref[...] * scale
        y = cw[0:1, :] * u
        for j in range(1, CONV_WIDTH):
            y = cw[j:j + 1, :] * u + pltpu.roll(y, 1, axis=0)
        return y[pad:pad + tm, :] + cb_ref[...] * scale

    half_a = conv(ua_ref, cwa_ref, cba_ref, 0.5)
    b = conv(ub_ref, cwb_ref, cbb_ref, 1.0)
    o_ref[...] = ((half_a + half_a * jnp.tanh(half_a)) * b).astype(o_ref.dtype)
    ua_ref[0:pad, :] = ua_ref[tm:tm + pad, :]
    ub_ref[0:pad, :] = ub_ref[tm:tm + pad, :]


def _ffn_up(h2, w_up, conv_w, conv_b, seq, tm=1024, tn=512):
    t, d = h2.shape
    d_ff = w_up.shape[1] // 2
    nj = d_ff // tn
    conv_b = conv_b.reshape(1, 2 * d_ff)
    return pl.pallas_call(
        functools.partial(_ffn_up_kernel, tiles_per_seq=seq // tm),
        grid=(nj, t // tm),
        in_specs=[pl.BlockSpec((tm, d), lambda j, i: (i, 0)),
                  pl.BlockSpec((d, tn), lambda j, i: (0, j)),
                  pl.BlockSpec((d, tn), lambda j, i: (0, nj + j)),
                  pl.BlockSpec((CONV_WIDTH, tn), lambda j, i: (0, j)),
                  pl.BlockSpec((CONV_WIDTH, tn), lambda j, i: (0, nj + j)),
                  pl.BlockSpec((1, tn), lambda j, i: (0, j)),
                  pl.BlockSpec((1, tn), lambda j, i: (0, nj + j))],
        out_specs=pl.BlockSpec((tm, tn), lambda j, i: (i, j)),
        out_shape=jax.ShapeDtypeStruct((t, d_ff), BF16),
        scratch_shapes=[pltpu.VMEM((tm + 8, tn), F32), pltpu.VMEM((tm + 8, tn), F32),
                        pltpu.VMEM((d, tn), BF16), pltpu.VMEM((d, tn), BF16)],
        compiler_params=_params(("arbitrary", "arbitrary"), 52),
        name="ffn_up_conv_gate",
    )(h2, w_up, w_up, conv_w, conv_w, conv_b, conv_b)


def _ffn_down_kernel(a_ref, w_hbm_ref, x_ref, nw_ref, o_ref, w_ref, sem_ref, *, final_norm, k_chunks):
    i = pl.program_id(0)
    copies = [pltpu.make_async_copy(w_hbm_ref.at[lo:hi, :], w_ref.at[lo:hi, :], sem_ref.at[c])
              for c, (lo, hi) in enumerate(k_chunks)]

    def finish(x2):
        if final_norm:
            r = lax.rsqrt(jnp.mean(x2 * x2, axis=-1, keepdims=True) + EPS)
            x2 = x2 * r * nw_ref[...]
        o_ref[...] = x2

    @pl.when(i == 0)
    def _():
        for copy in copies:
            copy.start()
        x2 = x_ref[...]
        for copy, (lo, hi) in zip(copies, k_chunks):
            copy.wait()
            x2 = x2 + jnp.dot(a_ref[:, lo:hi], w_ref[lo:hi, :], preferred_element_type=F32)
        finish(x2)

    @pl.when(i > 0)
    def _():
        finish(x_ref[...] + jnp.dot(a_ref[...], w_ref[...], preferred_element_type=F32))


def _k_chunks(k, tile=256):
    if k % tile:
        return ((0, k),)
    n = k // tile
    cuts = sorted({0, max(n // 5, 1), max(n // 2, 1), n})
    return tuple((lo * tile, hi * tile) for lo, hi in zip(cuts[:-1], cuts[1:]))


def _ffn_down(act, w_down, x1, norm_w, final_norm, tm=512):
    t, d = x1.shape
    d_ff = act.shape[1]
    k_chunks = _k_chunks(d_ff)
    return pl.pallas_call(
        functools.partial(_ffn_down_kernel, final_norm=final_norm, k_chunks=k_chunks),
        grid=(t // tm,),
        in_specs=[pl.BlockSpec((tm, d_ff), lambda i: (i, 0)),
                  pl.BlockSpec(memory_space=pl.ANY),
                  pl.BlockSpec((tm, d), lambda i: (i, 0)),
                  pl.BlockSpec((1, d), lambda i: (0, 0))],
        out_specs=pl.BlockSpec((tm, d), lambda i: (i, 0)),
        out_shape=jax.ShapeDtypeStruct((t, d), F32),
        scratch_shapes=[pltpu.VMEM((d_ff, d), BF16), pltpu.SemaphoreType.DMA((len(k_chunks),))],
        compiler_params=_params(("arbitrary",), 62),
        name="ffn_down",
    )(act, w_down, x1, norm_w.reshape(1, d))


def kernel(x, attn_norm_w, w_in, ret_norm_w, w_ret_up, w_moba_up, w_out, ffn_norm_w, w_ffn_up, conv_w,
           conv_b, w_ffn_down, final_norm_w):
    batch, seq, d = x.shape
    depth = w_in.shape[0]
    t = batch * seq
    xf = x.reshape(t, d)
    tm = 1024
    tm_norm = 1024
    tiles_per_seq = seq // tm

    ret_tables = _ret_rope_tables(seq)
    mob_tables = _moba_rope_tables(seq)

    def ret_specs(rows):
        return [pl.BlockSpec((rows, LANES), lambda j, i: (i % (seq // rows), 0))] * len(ret_tables)

    mob_specs = [pl.BlockSpec((None, tm, LANES), lambda j, i: (j, i % tiles_per_seq, 0))] * len(mob_tables)

    tn = 1024
    assert RET_WIDTH == tn and MOBA_WIDTH == tn and d % tn == 0
    w_in_col = dict(rv=2 * RET_WIDTH, rg=3 * RET_WIDTH, mv=4 * RET_WIDTH + 2 * MOBA_WIDTH,
                    g_ret=4 * RET_WIDTH + 3 * MOBA_WIDTH, g_moba=4 * RET_WIDTH + 3 * MOBA_WIDTH + d)
    cols = _plain_cols(d)
    widths = dict(g_ret=d, g_moba=d, rv=RET_WIDTH, rg=RET_WIDTH, mv=MOBA_WIDTH)
    src_tiles = [None] * (sum(widths.values()) // tn)
    for name, width in widths.items():
        for k in range(width // tn):
            src_tiles[_col_block(cols[name], tn) + k] = _col_block(w_in_col[name], tn) + k

    def plain_block(j):
        blk = src_tiles[-1]
        for out_tile in range(len(src_tiles) - 2, -1, -1):
            blk = jnp.where(j == out_tile, src_tiles[out_tile], blk)
        return blk

    for l in range(depth):
        (rq, h), _ = _in_proj(xf, w_in[l], lambda j: j, tn, _ret_rotary_epilogue, ret_tables,
                              ret_specs(tm_norm), tm_norm, tn, "in_proj_rq", norm_w=attn_norm_w[l])
        rk, (w_out16,) = _in_proj(h, w_in[l], lambda j: j + _col_block(RET_WIDTH, tn), tn,
                                  _ret_rotary_epilogue, ret_tables, ret_specs(tm), tm, tn, "in_proj_rk",
                                  casts=(w_out[l],))
        mob_qk, (w_ret_up16, w_moba_up16) = _in_proj(
            h, w_in[l], lambda j: j + _col_block(4 * RET_WIDTH, tn), 2 * tn, _moba_rotary_epilogue,
            mob_tables, mob_specs, tm, tn, "in_proj_moba", casts=(w_ret_up[l], w_moba_up[l]))
        plain, (w_down16,) = _in_proj(h, w_in[l], plain_block, len(src_tiles) * tn, _plain_epilogue, (), [],
                                      tm, tn, "in_proj_plain", casts=(w_ffn_down[l],))

        y_ret = _retention(rq, rk, plain, ret_norm_w[l], batch, seq, cols)
        y_moba = _moba(mob_qk, plain, batch, seq, cols)
        x1, h2 = _merge(y_ret, y_moba, plain, xf, w_ret_up16, w_moba_up16, w_out16, ffn_norm_w[l], cols)
        act = _ffn_up(h2, w_ffn_up[l], conv_w[l], conv_b[l], seq)
        last = l == depth - 1
        xf = _ffn_down(act, w_down16, x1, final_norm_w, final_norm=last)
    return xf.reshape(batch, seq, d)
```
